```python
import math
import jax, jax.numpy as jnp
from jax import lax
import numpy as np

D_MODEL = 1024
BATCH = 4
SEQ = 4096
DEPTH = 4
DEC_BATCH = 128
DEC_SEQ = 1
PAST_LEN = 8192
PAGE_SIZE = 128

N_MIXERS = 2
N_MLSTM = (DEPTH + N_MIXERS - 1) // N_MIXERS
N_SWA = DEPTH // N_MIXERS
MLSTM_HEADS = 8
MLSTM_DV = D_MODEL // MLSTM_HEADS
MLSTM_DQK = MLSTM_DV // 2
MLSTM_CHUNK = 128
MLSTM_IN = 2 * MLSTM_HEADS * MLSTM_DQK + MLSTM_HEADS * MLSTM_DV + D_MODEL + 2 * MLSTM_HEADS
SWA_Q_HEADS = 16
SWA_KV_HEADS = 4
SWA_GROUP = SWA_Q_HEADS // SWA_KV_HEADS
SWA_HEAD_DIM = 64
SWA_QKV = (SWA_Q_HEADS + 2 * SWA_KV_HEADS) * SWA_HEAD_DIM
WINDOW = 128
SWA_BLOCK = WINDOW
REL_BUCKETS = 32
REL_MAX_DIST = 128
PLE_DIM = 256
D_FF = 4 * D_MODEL
EPS = 1e-6
NEG = -1e30

kernel_name = 'hybrid_mlstm_swa_sink_decoder_step'


def rmsnorm(x, w):
    xf = x.astype(jnp.float32)
    y = xf * lax.rsqrt(jnp.mean(xf * xf, axis=-1, keepdims=True) + EPS)
    if w is not None:
        y = y * w.astype(jnp.float32)
    return y.astype(x.dtype)


def rel_bucket(dist):
    max_exact = REL_BUCKETS // 2
    d = jnp.maximum(dist, 0)
    df = jnp.maximum(d, 1).astype(jnp.float32)
    large = max_exact + (jnp.log(df / max_exact) / math.log(REL_MAX_DIST / max_exact)
                         * (REL_BUCKETS - max_exact)).astype(jnp.int32)
    large = jnp.minimum(large, REL_BUCKETS - 1)
    return jnp.where(d < max_exact, d, large)


def rel_bias_lookup(rel_bias, dist):
    return jnp.moveaxis(rel_bias.astype(jnp.float32)[rel_bucket(dist)], -1, 0)


def sink_softmax(logits, sink):
    mx = jnp.maximum(jnp.max(logits, axis=-1, keepdims=True), sink)
    p = jnp.exp(logits - mx)
    return p / (jnp.sum(p, axis=-1, keepdims=True) + jnp.exp(sink - mx))


def _to_chunks(a, nc, L):
    B = a.shape[0]
    a = a.reshape((B, nc, L) + a.shape[2:])
    return jnp.moveaxis(jnp.moveaxis(a, 1, 0), 2, 3)


def mlstm_chunkwise(q, k, v, ig, lf, C0, n0, m0):
    B, S = q.shape[:2]
    L = min(MLSTM_CHUNK, S)
    nc = -(-S // L)
    pad = nc * L - S
    if pad:
        def padz(a, val):
            return jnp.pad(a, [(0, 0), (0, pad)] + [(0, 0)] * (a.ndim - 2), constant_values=val)
        q, k, v, lf = padz(q, 0.0), padz(k, 0.0), padz(v, 0.0), padz(lf, 0.0)
        ig = padz(ig, NEG)
    xs = (_to_chunks(q, nc, L), _to_chunks(k, nc, L), _to_chunks(v, nc, L),
          _to_chunks(ig, nc, L), _to_chunks(lf, nc, L))
    tri = jnp.tril(jnp.ones((L, L), dtype=bool))

    def step(carry, inp):
        C, n, m = carry
        qc, kc, vc, igc, lfc = inp
        b = jnp.cumsum(lfc, axis=-1)
        dmat = jnp.where(tri, b[..., :, None] - b[..., None, :] + igc[..., None, :], NEG)
        inter = b + m[..., None]
        m_t = jnp.maximum(inter, jnp.max(dmat, axis=-1))
        w_inter = jnp.exp(inter - m_t)
        s = jnp.einsum('bhtd,bhsd->bhts', qc, kc) * jnp.exp(dmat - m_t[..., None])
        num = (w_inter[..., None] * jnp.einsum('bhtd,bhde->bhte', qc, C)
               + jnp.einsum('bhts,bhse->bhte', s, vc))
        den = w_inter * jnp.einsum('bhtd,bhd->bht', qc, n) + jnp.sum(s, axis=-1)
        h = num / jnp.maximum(jnp.abs(den), jnp.exp(-m_t))[..., None]
        m_new = m_t[..., -1]
        decay = jnp.exp(b[..., -1] + m - m_new)
        wk = jnp.exp(b[..., -1:] - b + igc - m_new[..., None])
        kw = kc * wk[..., None]
        C_new = decay[..., None, None] * C + jnp.einsum('bhsd,bhse->bhde', kw, vc)
        n_new = decay[..., None] * n + jnp.sum(kw, axis=2)
        return (C_new, n_new, m_new), h

    (C, n, m), hs = lax.scan(step, (C0, n0, m0), xs)
    hs = jnp.transpose(hs, (1, 0, 3, 2, 4)).reshape(B, nc * L, q.shape[2], v.shape[-1])[:, :S]
    return hs, C, n, m


def mlstm_mixer(x, w_in, b_i, b_f, norm_w, w_out, C0, n0, m0):
    B, S, _ = x.shape
    H, DK, DV = MLSTM_HEADS, MLSTM_DQK, MLSTM_DV
    f32 = jnp.float32
    sizes = [H * DK, H * DK, H * DV, D_MODEL, H, H]
    cuts = np.cumsum(sizes)[:-1].tolist()
    q, k, v, o, ig, fg = jnp.split(x @ w_in, cuts, axis=-1)
    q = q.reshape(B, S, H, DK).astype(f32) * (DK ** -0.5)
    k = k.reshape(B, S, H, DK).astype(f32)
    v = v.reshape(B, S, H, DV).astype(f32)
    ig = ig.astype(f32) + b_i.astype(f32)
    lf = jax.nn.log_sigmoid(fg.astype(f32) + b_f.astype(f32))
    h, C, n, m = mlstm_chunkwise(q, k, v, ig, lf, C0.astype(f32), n0.astype(f32), m0.astype(f32))
    h = h * lax.rsqrt(jnp.mean(h * h, axis=-1, keepdims=True) + EPS) * norm_w.astype(f32).reshape(H, DV)
    h = h.reshape(B, S, H * DV) * jax.nn.sigmoid(o.astype(f32))
    return h.astype(x.dtype) @ w_out, C, n, m


def _split_qkv(x, w_qkv):
    HQ, HKV, HD = SWA_Q_HEADS, SWA_KV_HEADS, SWA_HEAD_DIM
    return jnp.split(x @ w_qkv, [HQ * HD, (HQ + HKV) * HD], axis=-1)


def swa_prompt(x, w_qkv, sinks, w_o, rel_bias, win_len):
    B, S, _ = x.shape
    HQ, HKV, G, HD, BLK = SWA_Q_HEADS, SWA_KV_HEADS, SWA_GROUP, SWA_HEAD_DIM, SWA_BLOCK
    nb = S // BLK
    q, k, v = _split_qkv(x, w_qkv)
    q = q.reshape(B, nb, BLK, HKV, G, HD)
    k = k.reshape(B, S, HKV, HD)
    v = v.reshape(B, S, HKV, HD)
    pad5 = ((0, 0), (1, 0), (0, 0), (0, 0), (0, 0))
    kb = k.reshape(B, nb, BLK, HKV, HD)
    vb = v.reshape(B, nb, BLK, HKV, HD)
    k_band = jnp.concatenate([jnp.pad(kb, pad5)[:, :-1], kb], axis=2)
    v_band = jnp.concatenate([jnp.pad(vb, pad5)[:, :-1], vb], axis=2)
    logits = jnp.einsum('bnqhgd,bnkhd->bnhgqk', q, k_band,
                        preferred_element_type=jnp.float32) * (HD ** -0.5)
    qi = jnp.arange(BLK)[:, None]
    kj = jnp.arange(2 * BLK)[None, :]
    dist = qi + BLK - kj
    bias = rel_bias_lookup(rel_bias, dist).reshape(HKV, G, BLK, 2 * BLK)
    exists = (jnp.arange(nb)[:, None, None] > 0) | (kj >= BLK)[None]
    valid = (dist >= 0) & (dist < WINDOW) & exists
    logits = jnp.where(valid[None, :, None, None], logits + bias, NEG)
    probs = sink_softmax(logits, sinks.astype(jnp.float32).reshape(HKV, G, 1, 1))
    out = jnp.einsum('bnhgqk,bnkhd->bnqhgd', probs.astype(v.dtype), v_band).reshape(B, S, HQ * HD)
    return out @ w_o, k[:, S - win_len:], v[:, S - win_len:]


def swa_sample(x, k_cache, v_cache, w_qkv, sinks, w_o, rel_bias):
    B, S, _ = x.shape
    HQ, HKV, G, HD = SWA_Q_HEADS, SWA_KV_HEADS, SWA_GROUP, SWA_HEAD_DIM
    W = k_cache.shape[1]
    q, k, v = _split_qkv(x, w_qkv)
    q = q.reshape(B, S, HKV, G, HD)
    k_all = jnp.concatenate([k_cache.astype(q.dtype), k.reshape(B, S, HKV, HD)], axis=1)
    v_all = jnp.concatenate([v_cache.astype(q.dtype), v.reshape(B, S, HKV, HD)], axis=1)
    logits = jnp.einsum('bqhgd,bkhd->bhgqk', q, k_all,
                        preferred_element_type=jnp.float32) * (HD ** -0.5)
    qpos = PAST_LEN + jnp.arange(S)
    kpos = PAST_LEN - W + jnp.arange(W + S)
    dist = qpos[:, None] - kpos[None, :]
    bias = rel_bias_lookup(rel_bias, dist).reshape(HKV, G, S, W + S)
    valid = (dist >= 0) & (dist < WINDOW)
    logits = jnp.where(valid, logits + bias, NEG)
    probs = sink_softmax(logits, sinks.astype(jnp.float32).reshape(HKV, G, 1, 1))
    out = jnp.einsum('bhgqk,bkhd->bqhgd', probs.astype(v_all.dtype), v_all).reshape(B, S, HQ * HD)
    return out @ w_o, k_all[:, -W:], v_all[:, -W:]


def sq_relu_mlp(u, w_up, w_down):
    a = jax.nn.relu(u @ w_up)
    return (a * a) @ w_down


def per_layer_embed(h, p, w_proj, norm_w, w_gate):
    e = rmsnorm(p.astype(h.dtype) @ w_proj, norm_w)
    g = jax.nn.sigmoid((rmsnorm(h, None) @ w_gate).astype(jnp.float32))
    return (e.astype(jnp.float32) * g).astype(h.dtype)


def run_trunk(x, p, mlstm_state, swa_cache, win_len, params):
    (rel_bias, n_pre_mix, n_post_mix, n_pre_ffn, n_post_ffn,
     a_w_in, a_b_i, a_b_f, a_norm, a_w_out,
     b_w_qkv, b_sinks, b_w_o,
     f_up, f_down, e_proj, e_norm, e_gate) = params
    B = x.shape[0]
    f32 = jnp.float32
    Cs, ns, ms, ks, vs = [], [], [], [], []
    h = x
    for i in range(DEPTH):
        j = i // N_MIXERS
        u = rmsnorm(h, n_pre_mix[i])
        if i % N_MIXERS == 0:
            if mlstm_state is None:
                C0 = jnp.zeros((B, MLSTM_HEADS, MLSTM_DQK, MLSTM_DV), f32)
                n0 = jnp.zeros((B, MLSTM_HEADS, MLSTM_DQK), f32)
                m0 = jnp.zeros((B, MLSTM_HEADS), f32)
            else:
                C0, n0, m0 = mlstm_state[0][j], mlstm_state[1][j], mlstm_state[2][j]
            y, C, n, m = mlstm_mixer(u, a_w_in[j], a_b_i[j], a_b_f[j], a_norm[j], a_w_out[j], C0, n0, m0)
            Cs.append(C)
            ns.append(n)
            ms.append(m)
        else:
            if swa_cache is None:
                y, kw, vw = swa_prompt(u, b_w_qkv[j], b_sinks[j], b_w_o[j], rel_bias, win_len)
            else:
                y, kw, vw = swa_sample(u, swa_cache[0][j], swa_cache[1][j],
                                       b_w_qkv[j], b_sinks[j], b_w_o[j], rel_bias)
            ks.append(kw)
            vs.append(vw)
        h = h + rmsnorm(y, n_post_mix[i])
        u = rmsnorm(h, n_pre_ffn[i])
        h = h + rmsnorm(sq_relu_mlp(u, f_up[i], f_down[i]), n_post_ffn[i])
        h = h + per_layer_embed(h, p[i], e_proj[i], e_norm[i], e_gate[i])
    return h, jnp.stack(Cs), jnp.stack(ns), jnp.stack(ms), jnp.stack(ks), jnp.stack(vs)


def setup_inputs(seed: int = 0) -> dict:
    key = jax.random.key(seed)
    ks = iter(jax.random.split(key, 64))

    def nrm(shape, scale):
        return scale * jax.random.normal(next(ks), shape, jnp.float32)

    def gain(shape):
        return 1.0 + 0.05 * jax.random.normal(next(ks), shape, jnp.float32)

    W = min(WINDOW, PAST_LEN)
    H = MLSTM_HEADS
    return {
        'x_prompt': nrm((BATCH, SEQ, D_MODEL), 1.0),
        'x_sample': nrm((DEC_BATCH, DEC_SEQ, D_MODEL), 1.0),
        'state_mlstm_C': nrm((N_MLSTM, DEC_BATCH, H, MLSTM_DQK, MLSTM_DV), 0.1),
        'state_mlstm_n': nrm((N_MLSTM, DEC_BATCH, H, MLSTM_DQK), 0.1),
        'state_mlstm_m': nrm((N_MLSTM, DEC_BATCH, H), 1.0),
        'cache_swa_k': nrm((N_SWA, DEC_BATCH, W, SWA_KV_HEADS, SWA_HEAD_DIM), 1.0),
        'cache_swa_v': nrm((N_SWA, DEC_BATCH, W, SWA_KV_HEADS, SWA_HEAD_DIM), 1.0),
        'p_prompt': nrm((DEPTH, BATCH, SEQ, PLE_DIM), 1.0),
        'p_sample': nrm((DEPTH, DEC_BATCH, DEC_SEQ, PLE_DIM), 1.0),
        'rel_bias': nrm((REL_BUCKETS, SWA_Q_HEADS), 0.5),
        'norm_pre_mix': gain((DEPTH, D_MODEL)),
        'norm_post_mix': gain((DEPTH, D_MODEL)),
        'norm_pre_ffn': gain((DEPTH, D_MODEL)),
        'norm_post_ffn': gain((DEPTH, D_MODEL)),
        'mlstm_w_in': nrm((N_MLSTM, D_MODEL, MLSTM_IN), D_MODEL ** -0.5),
        'mlstm_b_i': nrm((N_MLSTM, H), 0.1),
        'mlstm_b_f': 3.0 + nrm((N_MLSTM, H), 0.5),
        'mlstm_norm': gain((N_MLSTM, H * MLSTM_DV)),
        'mlstm_w_out': nrm((N_MLSTM, H * MLSTM_DV, D_MODEL), (H * MLSTM_DV) ** -0.5),
        'swa_w_qkv': nrm((N_SWA, D_MODEL, SWA_QKV), D_MODEL ** -0.5),
        'swa_sinks': nrm((N_SWA, SWA_Q_HEADS), 0.5),
        'swa_w_o': nrm((N_SWA, SWA_Q_HEADS * SWA_HEAD_DIM, D_MODEL), (SWA_Q_HEADS * SWA_HEAD_DIM) ** -0.5),
        'ffn_w_up': nrm((DEPTH, D_MODEL, D_FF), D_MODEL ** -0.5),
        'ffn_w_down': nrm((DEPTH, D_FF, D_MODEL), D_FF ** -0.5),
        'ple_w_proj': nrm((DEPTH, PLE_DIM, D_MODEL), PLE_DIM ** -0.5),
        'ple_norm': gain((DEPTH, D_MODEL)),
        'ple_w_gate': nrm((DEPTH, D_MODEL, D_MODEL), D_MODEL ** -0.5),
    }


def reference(x_prompt, x_sample, state_mlstm_C, state_mlstm_n, state_mlstm_m, cache_swa_k, cache_swa_v,
              p_prompt, p_sample, rel_bias, norm_pre_mix, norm_post_mix, norm_pre_ffn, norm_post_ffn,
              mlstm_w_in, mlstm_b_i, mlstm_b_f, mlstm_norm, mlstm_w_out, swa_w_qkv, swa_sinks, swa_w_o,
              ffn_w_up, ffn_w_down, ple_w_proj, ple_norm, ple_w_gate):
    params = (rel_bias, norm_pre_mix, norm_post_mix, norm_pre_ffn, norm_post_ffn,
              mlstm_w_in, mlstm_b_i, mlstm_b_f, mlstm_norm, mlstm_w_out,
              swa_w_qkv, swa_sinks, swa_w_o,
              ffn_w_up, ffn_w_down, ple_w_proj, ple_norm, ple_w_gate)
    win_len = cache_swa_k.shape[2]
    y_prompt, Cp, np_, mp, kp, vp = run_trunk(x_prompt, p_prompt, None, None, win_len, params)
    y_sample, Cs, ns, ms, ks, vs = run_trunk(
        x_sample, p_sample, (state_mlstm_C, state_mlstm_n, state_mlstm_m),
        (cache_swa_k, cache_swa_v), win_len, params)
    C_prompt = Cp.astype(state_mlstm_C.dtype)
    n_prompt = np_.astype(state_mlstm_n.dtype)
    m_prompt = mp.astype(state_mlstm_m.dtype)
    k_win_prompt = kp.astype(cache_swa_k.dtype)
    v_win_prompt = vp.astype(cache_swa_v.dtype)
    C_sample = Cs.astype(state_mlstm_C.dtype)
    n_sample = ns.astype(state_mlstm_n.dtype)
    m_sample = ms.astype(state_mlstm_m.dtype)
    k_win_sample = ks.astype(cache_swa_k.dtype)
    v_win_sample = vs.astype(cache_swa_v.dtype)
    return (y_prompt, y_sample, C_prompt, n_prompt, m_prompt, k_win_prompt, v_win_prompt,
            C_sample, n_sample, m_sample, k_win_sample, v_win_sample)
```

```python
import functools
import math

import jax
import jax.numpy as jnp
from jax import lax
from jax.experimental import pallas as pl
from jax.experimental.pallas import tpu as pltpu

F32 = jnp.float32
BF16 = jnp.bfloat16

D_MODEL = 1024
DEPTH = 4
N_MIXERS = 2
HEADS = 8
DQK = 64
DV = 128
CHUNK = 128
Q_HEADS = 16
KV_HEADS = 4
GROUP = Q_HEADS // KV_HEADS
HEAD_DIM = 64
KV_DIM = KV_HEADS * HEAD_DIM
WINDOW = 128
REL_BUCKETS = 32
REL_MAX_DIST = 128
PLE_DIM = 256
D_FF = 4 * D_MODEL
EPS = 1e-6
NEG = -1e30

V7X_VMEM_BYTES = 64 * 1024 * 1024
VMEM_LIMIT = V7X_VMEM_BYTES - 8 * 1024 * 1024
LANES = 128

_NT = (((1,), (1,)), ((), ()))


def _params(n_grid):
    return pltpu.CompilerParams(dimension_semantics=("arbitrary",) * n_grid, vmem_limit_bytes=VMEM_LIMIT)


def _resident(shape):
    nd = len(shape)
    return pl.BlockSpec(shape, lambda *_: (0,) * nd, pipeline_mode=pl.Buffered(1))


def _rms(x, w=None):
    y = x * lax.rsqrt(jnp.mean(x * x, axis=-1, keepdims=True) + EPS)
    return y if w is None else y * w


def _log_sigmoid(x):
    return -(jnp.maximum(-x, 0.0) + jnp.log1p(jnp.exp(-jnp.abs(x))))


def _sigmoid(x):
    return 1.0 / (1.0 + jnp.exp(-x))


def _dot(a, b):
    return jnp.dot(a, b, preferred_element_type=F32)


def _mlstm_in_body(h_ref, nw_ref, wq_ref, wk_ref, wv_ref, wo_ref, wg_ref, wgt_ref, bg_ref, bgt_ref,
                   q_ref, k_ref, v_ref, o_ref, g_ref, gt_ref, *, k_transposed):
    u = _rms(h_ref[...], nw_ref[...]).astype(BF16)
    q_ref[...] = _dot(u, wq_ref[...]).astype(q_ref.dtype)
    if k_transposed:
        k_ref[...] = lax.dot_general(wk_ref[...], u, _NT, preferred_element_type=F32).astype(k_ref.dtype)
    else:
        k_ref[...] = _dot(u, wk_ref[...]).astype(k_ref.dtype)
    v_ref[...] = _dot(u, wv_ref[...]).astype(v_ref.dtype)
    o_ref[...] = _dot(u, wo_ref[...])
    g = _dot(u, wg_ref[...]) + bg_ref[...]
    lane = lax.broadcasted_iota(jnp.int32, g.shape, 1)
    g_ref[...] = jnp.where(lane < HEADS, g, _log_sigmoid(g))
    gt = lax.dot_general(wgt_ref[...], u, _NT, preferred_element_type=F32) + bgt_ref[...]
    row = lax.broadcasted_iota(jnp.int32, gt.shape, 0)
    gt_ref[...] = jnp.where(row < HEADS, gt, _log_sigmoid(gt))


def _mlstm_in(h, nw, w, *, tm, k_transposed, act_dtype):
    t = h.shape[0]
    hk = HEADS * DQK
    kshape, kspec = ((hk, t), pl.BlockSpec((hk, tm), lambda i: (0, i))) if k_transposed else \
                    ((t, hk), pl.BlockSpec((tm, hk), lambda i: (i, 0)))
    row = lambda n: pl.BlockSpec((tm, n), lambda i: (i, 0))
    return pl.pallas_call(
        functools.partial(_mlstm_in_body, k_transposed=k_transposed),
        grid=(t // tm,),
        in_specs=[row(D_MODEL), _resident((1, D_MODEL)), _resident(w["wq"].shape), _resident(w["wk"].shape),
                  _resident(w["wv"].shape), _resident(w["wo"].shape), _resident(w["wg"].shape),
                  _resident(w["wgt"].shape), _resident((1, 2 * HEADS)), _resident((2 * HEADS, 1))],
        out_specs=[row(hk), kspec, row(HEADS * DV), row(D_MODEL), row(2 * HEADS),
                   pl.BlockSpec((2 * HEADS, tm), lambda i: (0, i))],
        out_shape=[jax.ShapeDtypeStruct((t, hk), act_dtype), jax.ShapeDtypeStruct(kshape, act_dtype),
                   jax.ShapeDtypeStruct((t, HEADS * DV), act_dtype), jax.ShapeDtypeStruct((t, D_MODEL), F32),
                   jax.ShapeDtypeStruct((t, 2 * HEADS), F32), jax.ShapeDtypeStruct((2 * HEADS, t), F32)],
        compiler_params=_params(1), name="mlstm_in",
    )(h, nw, w["wq"], w["wk"], w["wv"], w["wo"], w["wg"], w["wgt"], w["bg"], w["bgt"])


def _swa_in_body(h_ref, nw_ref, wq_ref, wk_ref, wkt_ref, wv_ref, q_ref, k_ref, kt_ref, v_ref):
    u = _rms(h_ref[...], nw_ref[...]).astype(BF16)
    q_ref[...] = _dot(u, wq_ref[...]).astype(q_ref.dtype)
    k_ref[...] = _dot(u, wk_ref[...])
    kt_ref[...] = lax.dot_general(wkt_ref[...], u, _NT, preferred_element_type=F32).astype(kt_ref.dtype)
    v_ref[...] = _dot(u, wv_ref[...])


def _swa_in(h, nw, w, *, tm):
    t = h.shape[0]
    row = lambda n: pl.BlockSpec((tm, n), lambda i: (i, 0))
    return pl.pallas_call(
        _swa_in_body,
        grid=(t // tm,),
        in_specs=[row(D_MODEL), _resident((1, D_MODEL)), _resident(w["wq"].shape), _resident(w["wk"].shape),
                  _resident(w["wkt"].shape), _resident(w["wv"].shape)],
        out_specs=[row(D_MODEL), row(KV_DIM), pl.BlockSpec((KV_DIM, tm), lambda i: (0, i)), row(KV_DIM)],
        out_shape=[jax.ShapeDtypeStruct((t, D_MODEL), BF16), jax.ShapeDtypeStruct((t, KV_DIM), F32),
                   jax.ShapeDtypeStruct((KV_DIM, t), BF16), jax.ShapeDtypeStruct((t, KV_DIM), F32)],
        compiler_params=_params(1), name="swa_in",
    )(h, nw, w["wq"], w["wk"], w["wkt"], w["wv"])


FF_CHUNK = 1024


def _post_body(a_ref, h_ref, p_ref, wo_ref, npm_ref, npf_ref, wup_ref, wdn_ref, npff_ref, wpj_ref, pln_ref,
               wgate_ref, out_ref):
    y = _dot(a_ref[...].astype(BF16), wo_ref[...])
    h1 = h_ref[...] + _rms(y, npm_ref[...])
    u = _rms(h1, npf_ref[...]).astype(BF16)
    acc = jnp.zeros(h1.shape, F32)
    for c in range(D_FF // FF_CHUNK):
        a = jnp.maximum(_dot(u, wup_ref[:, c * FF_CHUNK:(c + 1) * FF_CHUNK]), 0.0)
        acc = acc + _dot((a * a).astype(BF16), wdn_ref[c * FF_CHUNK:(c + 1) * FF_CHUNK, :])
    h2 = h1 + _rms(acc, npff_ref[...])
    e = _rms(_dot(p_ref[...].astype(BF16), wpj_ref[...]), pln_ref[...])
    g = _sigmoid(_dot(_rms(h2).astype(BF16), wgate_ref[...]))
    out_ref[...] = h2 + e * g


def _post(a, h, p, w, *, tm):
    t = h.shape[0]
    row = lambda n: pl.BlockSpec((tm, n), lambda i: (i, 0))
    vec = _resident((1, D_MODEL))
    return pl.pallas_call(
        _post_body,
        grid=(t // tm,),
        in_specs=[row(D_MODEL), row(D_MODEL), row(PLE_DIM), _resident((D_MODEL, D_MODEL)), vec, vec,
                  _resident((D_MODEL, D_FF)), _resident((D_FF, D_MODEL)), vec, _resident((PLE_DIM, D_MODEL)), vec,
                  _resident((D_MODEL, D_MODEL))],
        out_specs=row(D_MODEL),
        out_shape=jax.ShapeDtypeStruct((t, D_MODEL), F32),
        compiler_params=_params(1), name="post",
    )(a, h, p, w["wo"], w["npm"], w["npf"], w["wup"], w["wdn"], w["npff"], w["wpj"], w["pln"], w["wgate"])


STATE_W = 2 * DV


def _mlstm_scan_body(q_ref, kt_ref, v_ref, o_ref, g_ref, gt_ref, nw_ref,
                     hg_ref, c_out_ref, n_out_ref, m_out_ref, cn_scr, m_scr):
    c = pl.program_id(1)
    L = CHUNK

    @pl.when(c == 0)
    def _():
        cn_scr[...] = jnp.zeros(cn_scr.shape, F32)
        m_scr[...] = jnp.zeros(m_scr.shape, F32)

    ri = lax.broadcasted_iota(jnp.int32, (L, L), 0)
    ci = lax.broadcasted_iota(jnp.int32, (L, L), 1)
    tri = ci <= ri
    lower = tri.astype(F32)
    upper = (ri <= ci).astype(F32)
    g = g_ref[...]
    gt = gt_ref[...]
    hi = lax.Precision.HIGHEST
    b_cols = jnp.dot(lower, g, precision=hi, preferred_element_type=F32)
    b_rows = jnp.dot(gt, upper, precision=hi, preferred_element_type=F32)
    lane_s = lax.broadcasted_iota(jnp.int32, (L, STATE_W - DV), 1)
    zeros_half = jnp.zeros((DQK, L), BF16)

    for h in range(HEADS):
        j, par = h // 2, h % 2
        q_pair = q_ref[:, j * 128:(j + 1) * 128]
        kt_h = kt_ref[h * DQK:(h + 1) * DQK, :]
        kt_m = jnp.concatenate([kt_h, zeros_half] if par == 0 else [zeros_half, kt_h], axis=0)
        v_h = v_ref[:, h * DV:(h + 1) * DV]
        cn = cn_scr[h]
        m_prev = m_scr[h:h + 1, 0:1]

        b_col = b_cols[:, HEADS + h:HEADS + h + 1]
        b_row = b_rows[HEADS + h:HEADS + h + 1, :]
        ig_row = gt[h:h + 1, :]
        b_last = b_col[L - 1:L, :]

        dmat = jnp.where(tri, b_col - b_row + ig_row, NEG)
        inter = b_col + m_prev
        m_t = jnp.maximum(inter, jnp.max(dmat, axis=1, keepdims=True))
        w_inter = jnp.exp(inter - m_t)
        s = _dot(q_pair, kt_m) * jnp.exp(dmat - m_t)
        qc = _dot(q_pair, cn.astype(BF16))
        num = w_inter * qc[:, :DV] + _dot(s.astype(BF16), v_h)
        den = w_inter * qc[:, DV:DV + 1] + jnp.sum(s, axis=1, keepdims=True)
        hh = num * (1.0 / jnp.maximum(jnp.abs(den), jnp.exp(-m_t)))

        m_new = m_t[L - 1:L, :]
        decay = jnp.exp(b_last + m_prev - m_new)
        wk_row = jnp.exp(b_last - b_row + ig_row - m_new)
        kw_t = kt_m.astype(F32) * wk_row
        cn_scr[h, :, 0:DV] = decay * cn[:, :DV] + _dot(kw_t.astype(BF16), v_h)
        n_upd = jnp.sum(kw_t, axis=1, keepdims=True)
        cn_scr[h, :, DV:STATE_W] = decay * cn[:, DV:] + jnp.where(lane_s == 0, n_upd, 0.0)
        m_scr[h:h + 1, :] = jnp.broadcast_to(m_new, (1, LANES))

        hn = _rms(hh, nw_ref[:, h * DV:(h + 1) * DV])
        hg_ref[:, h * DV:(h + 1) * DV] = (hn * _sigmoid(o_ref[:, h * DV:(h + 1) * DV])).astype(hg_ref.dtype)

    @pl.when(c == pl.num_programs(1) - 1)
    def _():
        for h in range(HEADS):
            par = h % 2
            st = cn_scr[h, par * DQK:(par + 1) * DQK, :]
            c_out_ref[0, h] = st[:, :DV]
            n_out_ref[0, h] = st[:, DV:]
        m_out_ref[0] = m_scr[...]


def _mlstm_scan(q, kt, v, o, g, gt, nw, *, batch, seq):
    nc = seq // CHUNK
    t = batch * seq
    rows = lambda n: pl.BlockSpec((CHUNK, n), lambda b, c: (b * nc + c, 0))
    cols = lambda n: pl.BlockSpec((n, CHUNK), lambda b, c: (0, b * nc + c))
    return pl.pallas_call(
        _mlstm_scan_body,
        grid=(batch, nc),
        in_specs=[rows(HEADS * DQK), cols(HEADS * DQK), rows(HEADS * DV), rows(D_MODEL), rows(2 * HEADS),
                  cols(2 * HEADS), _resident((1, HEADS * DV))],
        out_specs=[rows(HEADS * DV),
                   pl.BlockSpec((1, HEADS, DQK, DV), lambda b, c: (b, 0, 0, 0)),
                   pl.BlockSpec((1, HEADS, DQK, STATE_W - DV), lambda b, c: (b, 0, 0, 0)),
                   pl.BlockSpec((1, HEADS, LANES), lambda b, c: (b, 0, 0))],
        out_shape=[jax.ShapeDtypeStruct((t, HEADS * DV), BF16),
                   jax.ShapeDtypeStruct((batch, HEADS, DQK, DV), F32),
                   jax.ShapeDtypeStruct((batch, HEADS, DQK, STATE_W - DV), F32),
                   jax.ShapeDtypeStruct((batch, HEADS, LANES), F32)],
        scratch_shapes=[pltpu.VMEM((HEADS, 2 * DQK, STATE_W), F32), pltpu.VMEM((HEADS, LANES), F32)],
        compiler_params=_params(2), name="mlstm_scan",
    )(q, kt, v, o, g, gt, nw)


def _mlstm_step_body(c_ref, n_ref, m_ref, q_ref, k_ref, v_ref, o_ref, g_ref, nw_ref,
                     hg_ref, c_out_ref, n_out_ref, m_out_ref):
    h = pl.program_id(0)
    q = q_ref[0]
    k = k_ref[0]
    v = v_ref[...]
    n = n_ref[0]
    g = g_ref[...]
    lane_g = lax.broadcasted_iota(jnp.int32, g.shape, 1)
    ig = jnp.sum(jnp.where(lane_g == h, g, 0.0), axis=1, keepdims=True)
    lf = jnp.sum(jnp.where(lane_g == h + HEADS, g, 0.0), axis=1, keepdims=True)
    m_all = m_ref[...]
    lane_m = lax.broadcasted_iota(jnp.int32, m_all.shape, 1)
    m_prev = jnp.sum(jnp.where(lane_m == h, m_all, 0.0), axis=1, keepdims=True)

    inter = lf + m_prev
    m_t = jnp.maximum(inter, ig)
    w_inter = jnp.exp(inter - m_t)
    s = jnp.sum(q * k, axis=1, keepdims=True) * jnp.exp(ig - m_t)
    decay = jnp.exp(lf + m_prev - m_t)
    kw = k * jnp.exp(ig - m_t)

    qc = jnp.zeros(v.shape, F32)
    for d in range(DQK):
        c_d = c_ref[:, d, :]
        qc = qc + q[:, d:d + 1] * c_d
        c_out_ref[:, d, :] = decay * c_d + kw[:, d:d + 1] * v
    num = w_inter * qc + s * v
    den = w_inter * jnp.sum(q * n, axis=1, keepdims=True) + s
    hh = num * (1.0 / jnp.maximum(jnp.abs(den), jnp.exp(-m_t)))
    n_out_ref[0] = decay * n + kw

    @pl.when(h == 0)
    def _():
        m_out_ref[...] = jnp.zeros(m_out_ref.shape, F32)

    m_out_ref[...] = jnp.where(lane_m == h, m_t, m_out_ref[...])
    hg_ref[...] = (_rms(hh, nw_ref[...]) * _sigmoid(o_ref[...])).astype(hg_ref.dtype)


def _mlstm_step(c, n, m, q, k, v, o, g, nw):
    b = c.shape[0]
    head3 = pl.BlockSpec((1, b, DQK), lambda h: (h, 0, 0))
    lanes = pl.BlockSpec((b, DV), lambda h: (0, h))
    cspec = pl.BlockSpec((b, DQK, DV), lambda h: (0, h, 0))
    whole = lambda shape: pl.BlockSpec(shape, lambda h: (0,) * len(shape))
    return pl.pallas_call(
        _mlstm_step_body,
        grid=(HEADS,),
        in_specs=[cspec, head3, whole((b, HEADS)), head3, head3, lanes, lanes, whole((b, 2 * HEADS)),
                  pl.BlockSpec((1, DV), lambda h: (0, h))],
        out_specs=[lanes, cspec, head3, whole((b, HEADS))],
        out_shape=[jax.ShapeDtypeStruct((b, HEADS * DV), BF16), jax.ShapeDtypeStruct(c.shape, F32),
                   jax.ShapeDtypeStruct(n.shape, F32), jax.ShapeDtypeStruct((b, HEADS), F32)],
        compiler_params=_params(1), name="mlstm_step",
    )(c, n, m, q, k, v, o, g, nw)


def _bias_body(bucket_ref, rel_ref, out_ref):
    bucket = bucket_ref[...]
    for h in range(Q_HEADS):
        acc = jnp.zeros(bucket.shape, F32)
        for b in range(REL_BUCKETS):
            acc = jnp.where(bucket == b, rel_ref[b, h], acc)
        out_ref[h] = acc


def _bias_table(bucket, rel_bias):
    shape = bucket.shape
    return pl.pallas_call(
        _bias_body,
        in_specs=[pl.BlockSpec(memory_space=pltpu.VMEM), pl.BlockSpec(memory_space=pltpu.SMEM)],
        out_specs=pl.BlockSpec(memory_space=pltpu.VMEM),
        out_shape=jax.ShapeDtypeStruct((Q_HEADS,) + shape, F32),
        name="rel_bias_table",
    )(bucket, rel_bias)


def _rel_bucket(dist):
    max_exact = REL_BUCKETS // 2
    d = jnp.maximum(dist, 0)
    df = jnp.maximum(d, 1).astype(F32)
    large = max_exact + (jnp.log(df / max_exact) / math.log(REL_MAX_DIST / max_exact)
                         * (REL_BUCKETS - max_exact)).astype(jnp.int32)
    large = jnp.minimum(large, REL_BUCKETS - 1)
    return jnp.where(d < max_exact, d, large)


def _swa_prompt_body(q_ref, ktp_ref, ktc_ref, vp_ref, vc_ref, bias_ref, sink_ref, out_ref):
    n = pl.program_id(1)
    blk = WINDOW
    qi = lax.broadcasted_iota(jnp.int32, (blk, 2 * blk), 0)
    kj = lax.broadcasted_iota(jnp.int32, (blk, 2 * blk), 1)
    dist = qi + blk - kj
    valid = (dist >= 0) & (dist < WINDOW) & ((kj >= blk) | (n > 0))
    lane = lax.broadcasted_iota(jnp.int32, (2 * blk, LANES), 1)
    zero_k = jnp.zeros((HEAD_DIM, 2 * blk), BF16)

    for g in range(KV_HEADS):
        kt_g = jnp.concatenate([ktp_ref[g * HEAD_DIM:(g + 1) * HEAD_DIM, :],
                                ktc_ref[g * HEAD_DIM:(g + 1) * HEAD_DIM, :]], axis=1)
        rhs = jnp.concatenate([jnp.concatenate([kt_g, zero_k], axis=0),
                               jnp.concatenate([zero_k, kt_g], axis=0)], axis=1)
        vt = jnp.concatenate([vp_ref[:, (g // 2) * LANES:(g // 2 + 1) * LANES],
                              vc_ref[:, (g // 2) * LANES:(g // 2 + 1) * LANES]], axis=0)
        vr = pltpu.roll(vt, HEAD_DIM, axis=1)
        v_lo = jnp.where(lane < HEAD_DIM, vt if g % 2 == 0 else vr, 0.0)
        v_hi = jnp.where(lane >= HEAD_DIM, vr if g % 2 == 0 else vt, 0.0)
        v_bd = jnp.concatenate([v_lo, v_hi], axis=0).astype(BF16)
        for pair in range(GROUP // 2):
            jp = g * (GROUP // 2) + pair
            logits = _dot(q_ref[:, jp * LANES:(jp + 1) * LANES], rhs)
            probs = []
            for i in range(2):
                hd = 2 * jp + i
                lg = jnp.where(valid, logits[:, i * 2 * blk:(i + 1) * 2 * blk] + bias_ref[hd], NEG)
                sink = sink_ref[hd]
                mx = jnp.maximum(jnp.max(lg, axis=1, keepdims=True), sink)
                p = jnp.exp(lg - mx)
                inv = 1.0 / (jnp.sum(p, axis=1, keepdims=True) + jnp.exp(sink - mx))
                probs.append((p * inv).astype(BF16))
            out_ref[:, jp * LANES:(jp + 1) * LANES] = _dot(jnp.concatenate(probs, axis=1), v_bd).astype(out_ref.dtype)


def _swa_prompt(q, kt, v, bias, sinks, *, batch, seq):
    nb = seq // WINDOW
    t = batch * seq
    cur = lambda b, n: b * nb + n
    prev = lambda b, n: jnp.maximum(b * nb + n - 1, 0)
    return pl.pallas_call(
        _swa_prompt_body,
        grid=(batch, nb),
        in_specs=[pl.BlockSpec((WINDOW, D_MODEL), lambda b, n: (cur(b, n), 0)),
                  pl.BlockSpec((KV_DIM, WINDOW), lambda b, n: (0, prev(b, n))),
                  pl.BlockSpec((KV_DIM, WINDOW), lambda b, n: (0, cur(b, n))),
                  pl.BlockSpec((WINDOW, KV_DIM), lambda b, n: (prev(b, n), 0)),
                  pl.BlockSpec((WINDOW, KV_DIM), lambda b, n: (cur(b, n), 0)),
                  _resident(bias.shape),
                  pl.BlockSpec(memory_space=pltpu.SMEM)],
        out_specs=pl.BlockSpec((WINDOW, D_MODEL), lambda b, n: (cur(b, n), 0)),
        out_shape=jax.ShapeDtypeStruct((t, D_MODEL), BF16),
        compiler_params=_params(2), name="swa_prompt",
    )(q, kt, kt, v, v, bias, sinks)


SAMPLE_BB = 8


def _swa_sample_body(q_ref, kc_ref, vc_ref, kn_ref, vn_ref, bias_ref, sink_ref, out_ref, ko_ref, vo_ref):
    w = WINDOW
    row = lax.broadcasted_iota(jnp.int32, (w, KV_DIM), 0)
    rg = lax.broadcasted_iota(jnp.int32, (Q_HEADS, KV_DIM), 0) // GROUP
    lg = lax.broadcasted_iota(jnp.int32, (Q_HEADS, KV_DIM), 1) // HEAD_DIM
    rsel = lax.broadcasted_iota(jnp.int32, (Q_HEADS, HEAD_DIM), 0) // GROUP
    bias = bias_ref[...]
    sink = sink_ref[...]
    for i in range(SAMPLE_BB):
        k_win = jnp.where(row == w - 1, kn_ref[i], pltpu.roll(kc_ref[i], w - 1, axis=0))
        v_win = jnp.where(row == w - 1, vn_ref[i], pltpu.roll(vc_ref[i], w - 1, axis=0))
        ko_ref[i] = k_win
        vo_ref[i] = v_win
        qb = q_ref[i].astype(F32)
        q_bd = jnp.where(rg == lg, jnp.concatenate([qb] * KV_HEADS, axis=1), 0.0).astype(BF16)
        logits = lax.dot_general(q_bd, k_win.astype(BF16), _NT, preferred_element_type=F32) + bias
        mx = jnp.maximum(jnp.max(logits, axis=1, keepdims=True), sink)
        p = jnp.exp(logits - mx)
        inv = 1.0 / (jnp.sum(p, axis=1, keepdims=True) + jnp.exp(sink - mx))
        o = _dot((p * inv).astype(BF16), v_win.astype(BF16))
        sel = o[:, 0:HEAD_DIM]
        for g in range(1, KV_HEADS):
            sel = jnp.where(rsel == g, o[:, g * HEAD_DIM:(g + 1) * HEAD_DIM], sel)
        out_ref[i] = sel.astype(out_ref.dtype)


def _swa_sample(q, kc, vc, kn, vn, bias, sinks):
    b = q.shape[0]
    bb = SAMPLE_BB
    blk = lambda shape: pl.BlockSpec((bb,) + shape, lambda i: (i, 0, 0))
    return pl.pallas_call(
        _swa_sample_body,
        grid=(b // bb,),
        in_specs=[blk((Q_HEADS, HEAD_DIM)), blk((WINDOW, KV_DIM)), blk((WINDOW, KV_DIM)), blk((1, KV_DIM)),
                  blk((1, KV_DIM)), _resident(bias.shape), _resident((Q_HEADS, 1))],
        out_specs=[blk((Q_HEADS, HEAD_DIM)), blk((WINDOW, KV_DIM)), blk((WINDOW, KV_DIM))],
        out_shape=[jax.ShapeDtypeStruct((b, Q_HEADS, HEAD_DIM), BF16), jax.ShapeDtypeStruct(kc.shape, F32),
                   jax.ShapeDtypeStruct(vc.shape, F32)],
        compiler_params=_params(1), name="swa_sample",
    )(q, kc, vc, kn, vn, bias, sinks)


def _prep_mlstm(w_in, b_i, b_f, k_transposed):
    hk, hv = HEADS * DQK, HEADS * DV
    wq = (w_in[:, :hk] * (DQK ** -0.5)).astype(BF16)
    wk = w_in[:, hk:2 * hk]
    wg = w_in[:, 2 * hk + hv + D_MODEL:]
    bg = jnp.concatenate([b_i, b_f]).astype(F32)
    return dict(wq=wq, wk=(wk.T if k_transposed else wk).astype(BF16),
                wv=w_in[:, 2 * hk:2 * hk + hv].astype(BF16),
                wo=w_in[:, 2 * hk + hv:2 * hk + hv + D_MODEL].astype(BF16),
                wg=wg.astype(BF16), wgt=wg.T.astype(BF16), bg=bg.reshape(1, -1), bgt=bg.reshape(-1, 1))


def _prep_swa(w_qkv):
    wk = w_qkv[:, D_MODEL:D_MODEL + KV_DIM]
    return dict(wq=(w_qkv[:, :D_MODEL] * (HEAD_DIM ** -0.5)).astype(BF16),
                wk=wk.astype(BF16), wkt=wk.T.astype(BF16), wv=w_qkv[:, D_MODEL + KV_DIM:].astype(BF16))


def kernel(x_prompt, x_sample, state_mlstm_C, state_mlstm_n, state_mlstm_m, cache_swa_k, cache_swa_v, p_prompt, p_sample, rel_bias, norm_pre_mix, norm_post_mix, norm_pre_ffn, norm_post_ffn, mlstm_w_in, mlstm_b_i, mlstm_b_f, mlstm_norm, mlstm_w_out, swa_w_qkv, swa_sinks, swa_w_o, ffn_w_up, ffn_w_down, ple_w_proj, ple_norm, ple_w_gate):
    bp, sp, _ = x_prompt.shape
    bs = x_sample.shape[0]
    win = cache_swa_k.shape[2]
    past = 8192
    vec = lambda a: a.reshape(1, -1).astype(F32)

    qi = jnp.arange(WINDOW)[:, None]
    kj = jnp.arange(2 * WINDOW)[None, :]
    bias_p = _bias_table(_rel_bucket(qi + WINDOW - kj).astype(jnp.int32), rel_bias.astype(F32))
    dist_s = (win - 1) - jnp.arange(win)
    bias_s = _bias_table(jnp.broadcast_to(_rel_bucket(dist_s)[None, :], (8, win)).astype(jnp.int32),
                         rel_bias.astype(F32))[:, 0, :]

    hp = x_prompt.reshape(bp * sp, D_MODEL)
    hs = x_sample.reshape(bs, D_MODEL)
    tm_p = 512
    outs = dict(Cp=[], np=[], mp=[], kp=[], vp=[], Cs=[], ns=[], ms=[], ks=[], vs=[])

    for i in range(DEPTH):
        j = i // N_MIXERS
        post_w = dict(npm=vec(norm_post_mix[i]), npf=vec(norm_pre_ffn[i]), wup=ffn_w_up[i].astype(BF16),
                      wdn=ffn_w_down[i].astype(BF16), npff=vec(norm_post_ffn[i]), wpj=ple_w_proj[i].astype(BF16),
                      pln=vec(ple_norm[i]), wgate=ple_w_gate[i].astype(BF16))
        nw = vec(norm_pre_mix[i])
        if i % N_MIXERS == 0:
            post_w["wo"] = mlstm_w_out[j].astype(BF16)
            mnw = vec(mlstm_norm[j])
            q, kt, v, o, g, gt = _mlstm_in(hp, nw, _prep_mlstm(mlstm_w_in[j], mlstm_b_i[j], mlstm_b_f[j], True),
                                           tm=tm_p, k_transposed=True, act_dtype=BF16)
            a_p, c_fin, n_fin, m_fin = _mlstm_scan(q, kt, v, o, g, gt, mnw, batch=bp, seq=sp)
            outs["Cp"].append(c_fin)
            outs["np"].append(n_fin[..., 0])
            outs["mp"].append(m_fin[..., 0])
            q, k, v, o, g, _ = _mlstm_in(hs, nw, _prep_mlstm(mlstm_w_in[j], mlstm_b_i[j], mlstm_b_f[j], False),
                                         tm=bs, k_transposed=False, act_dtype=F32)
            heads_first = lambda a: a.reshape(bs, HEADS, DQK).transpose(1, 0, 2)
            a_s, c_new, n_new, m_new = _mlstm_step(
                state_mlstm_C[j].astype(F32).reshape(bs, HEADS * DQK, DV),
                state_mlstm_n[j].astype(F32).transpose(1, 0, 2), state_mlstm_m[j].astype(F32),
                heads_first(q), heads_first(k), v, o, g, mnw)
            outs["Cs"].append(c_new.reshape(bs, HEADS, DQK, DV))
            outs["ns"].append(n_new.transpose(1, 0, 2))
            outs["ms"].append(m_new)
        else:
            post_w["wo"] = swa_w_o[j].astype(BF16)
            sw = _prep_swa(swa_w_qkv[j])
            sinks = swa_sinks[j].astype(F32)
            q, k, kt, v = _swa_in(hp, nw, sw, tm=tm_p)
            a_p = _swa_prompt(q, kt, v, bias_p, sinks, batch=bp, seq=sp)
            outs["kp"].append(k.reshape(bp, sp, KV_HEADS, HEAD_DIM)[:, sp - win:])
            outs["vp"].append(v.reshape(bp, sp, KV_HEADS, HEAD_DIM)[:, sp - win:])
            q, k, _, v = _swa_in(hs, nw, sw, tm=bs)
            a_s, k_win, v_win = _swa_sample(
                q.reshape(bs, Q_HEADS, HEAD_DIM),
                cache_swa_k[j].reshape(bs, win, KV_DIM), cache_swa_v[j].reshape(bs, win, KV_DIM),
                k.reshape(bs, 1, KV_DIM), v.reshape(bs, 1, KV_DIM), bias_s, sinks.reshape(Q_HEADS, 1))
            a_s = a_s.reshape(bs, D_MODEL)
            outs["ks"].append(k_win.reshape(bs, win, KV_HEADS, HEAD_DIM))
            outs["vs"].append(v_win.reshape(bs, win, KV_HEADS, HEAD_DIM))
        hp = _post(a_p, hp, p_prompt[i].reshape(bp * sp, PLE_DIM), post_w, tm=tm_p)
        hs = _post(a_s, hs, p_sample[i].reshape(bs, PLE_DIM), post_w, tm=bs)

    st = lambda key, like: jnp.stack(outs[key]).astype(like.dtype)
    return (hp.reshape(x_prompt.shape), hs.reshape(x_sample.shape),
            st("Cp", state_mlstm_C), st("np", state_mlstm_n), st("mp", state_mlstm_m),
            st("kp", cache_swa_k), st("vp", cache_swa_v),
            st("Cs", state_mlstm_C), st("ns", state_mlstm_n), st("ms", state_mlstm_m),
            st("ks", cache_swa_k), st("vs", cache_swa_v))
```

```python
import functools
import math

import jax
import jax.numpy as jnp
from jax import lax
from jax.experimental import pallas as pl
from jax.experimental.pallas import tpu as pltpu

F32 = jnp.float32
BF16 = jnp.bfloat16

D_MODEL = 1024
DEPTH = 4
N_MIXERS = 2
HEADS = 8
DQK = 64
DV = 128
CHUNK = 128
Q_HEADS = 16
KV_HEADS = 4
GROUP = Q_HEADS // KV_HEADS
HEAD_DIM = 64
KV_DIM = KV_HEADS * HEAD_DIM
WINDOW = 128
REL_BUCKETS = 32
REL_MAX_DIST = 128
PLE_DIM = 256
D_FF = 4 * D_MODEL
EPS = 1e-6
NEG = -1e30

V7X_VMEM_BYTES = 64 * 1024 * 1024
VMEM_LIMIT = V7X_VMEM_BYTES - 8 * 1024 * 1024
LANES = 128

_NT = (((1,), (1,)), ((), ()))


def _params(n_grid):
    return pltpu.CompilerParams(dimension_semantics=("arbitrary",) * n_grid, vmem_limit_bytes=VMEM_LIMIT)


def _resident(shape):
    nd = len(shape)
    return pl.BlockSpec(shape, lambda *_: (0,) * nd, pipeline_mode=pl.Buffered(1))


def _rms(x, w=None):
    y = x * lax.rsqrt(jnp.mean(x * x, axis=-1, keepdims=True) + EPS)
    return y if w is None else y * w


def _log_sigmoid(x):
    return -(jnp.maximum(-x, 0.0) + jnp.log1p(jnp.exp(-jnp.abs(x))))


def _sigmoid(x):
    return 1.0 / (1.0 + jnp.exp(-x))


def _dot(a, b):
    return jnp.dot(a, b, preferred_element_type=F32)


def _mlstm_in_body(h_ref, nw_ref, wq_ref, wk_ref, wv_ref, wo_ref, wg_ref, bg_ref,
                   q_ref, k_ref, v_ref, o_ref, g_ref, *, prompt):
    u = _rms(h_ref[...], nw_ref[...]).astype(BF16)
    q_ref[...] = _dot(u, wq_ref[...]).astype(q_ref.dtype)
    v_ref[...] = _dot(u, wv_ref[...]).astype(v_ref.dtype)
    o_ref[...] = _dot(u, wo_ref[...])
    if prompt:
        k_ref[...] = lax.dot_general(wk_ref[...], u, _NT, preferred_element_type=F32).astype(k_ref.dtype)
        gt = lax.dot_general(wg_ref[...], u, _NT, preferred_element_type=F32) + bg_ref[...]
        ig, lf = gt[:HEADS], _log_sigmoid(gt[HEADS:])
        tm = lf.shape[1]
        r = lax.broadcasted_iota(jnp.int32, (tm, tm), 0)
        t = lax.broadcasted_iota(jnp.int32, (tm, tm), 1)
        same_chunk_upper = ((r <= t) & (r // CHUNK == t // CHUNK)).astype(F32)
        b = jnp.dot(lf, same_chunk_upper, precision=lax.Precision.HIGHEST, preferred_element_type=F32)
        g_ref[...] = jnp.concatenate([ig, lf, b], axis=0)
    else:
        k_ref[...] = _dot(u, wk_ref[...]).astype(k_ref.dtype)
        g = _dot(u, wg_ref[...]) + bg_ref[...]
        lane = lax.broadcasted_iota(jnp.int32, g.shape, 1)
        g_ref[...] = jnp.where(lane < HEADS, g, _log_sigmoid(g))


def _mlstm_in(h, nw, w, *, tm, prompt, act_dtype):
    t = h.shape[0]
    hk = HEADS * DQK
    row = lambda n: pl.BlockSpec((tm, n), lambda i: (i, 0))
    col = lambda n: pl.BlockSpec((n, tm), lambda i: (0, i))
    kshape, kspec = ((hk, t), col(hk)) if prompt else ((t, hk), row(hk))
    gshape, gspec = ((3 * HEADS, t), col(3 * HEADS)) if prompt else ((t, 2 * HEADS), row(2 * HEADS))
    return pl.pallas_call(
        functools.partial(_mlstm_in_body, prompt=prompt),
        grid=(t // tm,),
        in_specs=[row(D_MODEL), _resident((1, D_MODEL)), _resident(w["wq"].shape), _resident(w["wk"].shape),
                  _resident(w["wv"].shape), _resident(w["wo"].shape), _resident(w["wg"].shape),
                  _resident(w["bg"].shape)],
        out_specs=[row(hk), kspec, row(HEADS * DV), row(D_MODEL), gspec],
        out_shape=[jax.ShapeDtypeStruct((t, hk), act_dtype), jax.ShapeDtypeStruct(kshape, act_dtype),
                   jax.ShapeDtypeStruct((t, HEADS * DV), act_dtype), jax.ShapeDtypeStruct((t, D_MODEL), F32),
                   jax.ShapeDtypeStruct(gshape, F32)],
        compiler_params=_params(1), name="mlstm_in",
    )(h, nw, w["wq"], w["wk"], w["wv"], w["wo"], w["wg"], w["bg"])


def _swa_in_body(h_ref, nw_ref, wq_ref, wk_ref, wkt_ref, wv_ref, q_ref, k_ref, kt_ref, v_ref):
    u = _rms(h_ref[...], nw_ref[...]).astype(BF16)
    q_ref[...] = _dot(u, wq_ref[...]).astype(q_ref.dtype)
    k_ref[...] = _dot(u, wk_ref[...])
    kt_ref[...] = lax.dot_general(wkt_ref[...], u, _NT, preferred_element_type=F32).astype(kt_ref.dtype)
    v_ref[...] = _dot(u, wv_ref[...])


def _swa_in(h, nw, w, *, tm):
    t = h.shape[0]
    row = lambda n: pl.BlockSpec((tm, n), lambda i: (i, 0))
    return pl.pallas_call(
        _swa_in_body,
        grid=(t // tm,),
        in_specs=[row(D_MODEL), _resident((1, D_MODEL)), _resident(w["wq"].shape), _resident(w["wk"].shape),
                  _resident(w["wkt"].shape), _resident(w["wv"].shape)],
        out_specs=[row(D_MODEL), row(KV_DIM), pl.BlockSpec((KV_DIM, tm), lambda i: (0, i)), row(KV_DIM)],
        out_shape=[jax.ShapeDtypeStruct((t, D_MODEL), BF16), jax.ShapeDtypeStruct((t, KV_DIM), F32),
                   jax.ShapeDtypeStruct((KV_DIM, t), BF16), jax.ShapeDtypeStruct((t, KV_DIM), F32)],
        compiler_params=_params(1), name="swa_in",
    )(h, nw, w["wq"], w["wk"], w["wkt"], w["wv"])


FF_CHUNK = 1024


def _post_body(a_ref, h_ref, p_ref, wo_ref, npm_ref, npf_ref, wup_ref, wdn_ref, npff_ref, wpj_ref, pln_ref,
               wgate_ref, out_ref):
    y = _dot(a_ref[...].astype(BF16), wo_ref[...])
    h1 = h_ref[...] + _rms(y, npm_ref[...])
    u = _rms(h1, npf_ref[...]).astype(BF16)
    acc = jnp.zeros(h1.shape, F32)
    for c in range(D_FF // FF_CHUNK):
        a = jnp.maximum(_dot(u, wup_ref[:, c * FF_CHUNK:(c + 1) * FF_CHUNK]), 0.0)
        acc = acc + _dot((a * a).astype(BF16), wdn_ref[c * FF_CHUNK:(c + 1) * FF_CHUNK, :])
    h2 = h1 + _rms(acc, npff_ref[...])
    e = _rms(_dot(p_ref[...].astype(BF16), wpj_ref[...]), pln_ref[...])
    g = _sigmoid(_dot(_rms(h2).astype(BF16), wgate_ref[...]))
    out_ref[...] = h2 + e * g


def _post(a, h, p, w, *, tm):
    t = h.shape[0]
    row = lambda n: pl.BlockSpec((tm, n), lambda i: (i, 0))
    vec = _resident((1, D_MODEL))
    return pl.pallas_call(
        _post_body,
        grid=(t // tm,),
        in_specs=[row(D_MODEL), row(D_MODEL), row(PLE_DIM), _resident((D_MODEL, D_MODEL)), vec, vec,
                  _resident((D_MODEL, D_FF)), _resident((D_FF, D_MODEL)), vec, _resident((PLE_DIM, D_MODEL)), vec,
                  _resident((D_MODEL, D_MODEL))],
        out_specs=row(D_MODEL),
        out_shape=jax.ShapeDtypeStruct((t, D_MODEL), F32),
        compiler_params=_params(1), name="post",
    )(a, h, p, w["wo"], w["npm"], w["npf"], w["wup"], w["wdn"], w["npff"], w["wpj"], w["pln"], w["wgate"])


STATE_W = 2 * DV


def _mlstm_scan_body(q_ref, kt_ref, v_ref, o_ref, gt_ref, nw_ref,
                     hg_ref, c_out_ref, n_out_ref, m_out_ref, cn_scr, m_scr):
    c = pl.program_id(1)
    L = CHUNK

    @pl.when(c == 0)
    def _():
        cn_scr[...] = jnp.zeros(cn_scr.shape, F32)
        m_scr[...] = jnp.zeros(m_scr.shape, F32)

    ri = lax.broadcasted_iota(jnp.int32, (L, L), 0)
    ci = lax.broadcasted_iota(jnp.int32, (L, L), 1)
    tri = ci <= ri
    gt = gt_ref[...]
    ig_rows, lf_rows, b_rows = gt[:HEADS], gt[HEADS:2 * HEADS], gt[2 * HEADS:]
    a_rows = ig_rows - b_rows
    m_prev = m_scr[...]
    b_last = jnp.broadcast_to(b_rows[:, L - 1:L], (HEADS, L))
    m_new = b_last + jnp.maximum(m_prev, jnp.max(a_rows, axis=1, keepdims=True))
    decay_rows = jnp.exp(b_last + m_prev - m_new)
    wk_rows = jnp.exp(b_last - b_rows + ig_rows - m_new)
    m_scr[...] = m_new

    ones_rhs = jnp.ones((L, DV), BF16)
    zeros_half = jnp.zeros((DQK, L), BF16)
    for h in range(HEADS):
        j, par = h // 2, h % 2
        q_pair = q_ref[:, j * 128:(j + 1) * 128]
        kt_h = kt_ref[h * DQK:(h + 1) * DQK, :]
        kt_m = jnp.concatenate([kt_h, zeros_half] if par == 0 else [zeros_half, kt_h], axis=0)
        v_ext = jnp.concatenate([v_ref[:, h * DV:(h + 1) * DV], ones_rhs], axis=1)
        cn = cn_scr[h]
        a_row, m_row = a_rows[h:h + 1], m_prev[h:h + 1]

        big_m = jnp.maximum(jnp.max(jnp.where(tri, a_row, -jnp.inf), axis=1, keepdims=True), m_row)
        b_col = jnp.sum(jnp.where(tri, lf_rows[h:h + 1], 0.0), axis=1, keepdims=True)
        w_inter = jnp.exp(m_row - big_m)
        s = _dot(q_pair, kt_m) * jnp.exp(jnp.where(tri, a_row - big_m, NEG))
        sv = _dot(s.astype(BF16), v_ext)
        qc = _dot(q_pair, cn.astype(BF16))
        num = w_inter * qc[:, :DV] + sv[:, :DV]
        den = w_inter * qc[:, DV:] + sv[:, DV:]
        hh = num * (1.0 / jnp.maximum(jnp.abs(den), jnp.exp(-(b_col + big_m))))

        kw_t = kt_m.astype(F32) * wk_rows[h:h + 1]
        dec = decay_rows[h:h + 1]
        cn_scr[h] = jnp.concatenate([dec, dec], axis=1) * cn + _dot(kw_t.astype(BF16), v_ext)

        hn = _rms(hh, nw_ref[:, h * DV:(h + 1) * DV])
        hg_ref[:, h * DV:(h + 1) * DV] = (hn * _sigmoid(o_ref[:, h * DV:(h + 1) * DV])).astype(hg_ref.dtype)

    @pl.when(c == pl.num_programs(1) - 1)
    def _():
        for h in range(HEADS):
            par = h % 2
            st = cn_scr[h, par * DQK:(par + 1) * DQK, :]
            c_out_ref[0, h] = st[:, :DV]
            n_out_ref[0, h] = st[:, DV:]
        m_out_ref[0] = m_scr[...]


def _mlstm_scan(q, kt, v, o, gt, nw, *, batch, seq):
    nc = seq // CHUNK
    t = batch * seq
    rows = lambda n: pl.BlockSpec((CHUNK, n), lambda b, c: (b * nc + c, 0))
    cols = lambda n: pl.BlockSpec((n, CHUNK), lambda b, c: (0, b * nc + c))
    return pl.pallas_call(
        _mlstm_scan_body,
        grid=(batch, nc),
        in_specs=[rows(HEADS * DQK), cols(HEADS * DQK), rows(HEADS * DV), rows(D_MODEL), cols(3 * HEADS),
                  _resident((1, HEADS * DV))],
        out_specs=[rows(HEADS * DV),
                   pl.BlockSpec((1, HEADS, DQK, DV), lambda b, c: (b, 0, 0, 0)),
                   pl.BlockSpec((1, HEADS, DQK, STATE_W - DV), lambda b, c: (b, 0, 0, 0)),
                   pl.BlockSpec((1, HEADS, LANES), lambda b, c: (b, 0, 0))],
        out_shape=[jax.ShapeDtypeStruct((t, HEADS * DV), BF16),
                   jax.ShapeDtypeStruct((batch, HEADS, DQK, DV), F32),
                   jax.ShapeDtypeStruct((batch, HEADS, DQK, STATE_W - DV), F32),
                   jax.ShapeDtypeStruct((batch, HEADS, LANES), F32)],
        scratch_shapes=[pltpu.VMEM((HEADS, 2 * DQK, STATE_W), F32), pltpu.VMEM((HEADS, LANES), F32)],
        compiler_params=_params(2), name="mlstm_scan",
    )(q, kt, v, o, gt, nw)


def _mlstm_step_body(c_ref, n_ref, m_ref, q_ref, k_ref, v_ref, o_ref, g_ref, nw_ref,
                     hg_ref, c_out_ref, n_out_ref, m_out_ref):
    h = pl.program_id(0)
    q = q_ref[0]
    k = k_ref[0]
    v = v_ref[...]
    n = n_ref[0]
    g = g_ref[...]
    lane_g = lax.broadcasted_iota(jnp.int32, g.shape, 1)
    ig = jnp.sum(jnp.where(lane_g == h, g, 0.0), axis=1, keepdims=True)
    lf = jnp.sum(jnp.where(lane_g == h + HEADS, g, 0.0), axis=1, keepdims=True)
    m_all = m_ref[...]
    lane_m = lax.broadcasted_iota(jnp.int32, m_all.shape, 1)
    m_prev = jnp.sum(jnp.where(lane_m == h, m_all, 0.0), axis=1, keepdims=True)

    inter = lf + m_prev
    m_t = jnp.maximum(inter, ig)
    w_inter = jnp.exp(inter - m_t)
    s = jnp.sum(q * k, axis=1, keepdims=True) * jnp.exp(ig - m_t)
    decay = jnp.exp(lf + m_prev - m_t)
    kw = k * jnp.exp(ig - m_t)

    qc = jnp.zeros(v.shape, F32)
    for d in range(DQK):
        c_d = c_ref[:, d, :]
        qc = qc + q[:, d:d + 1] * c_d
        c_out_ref[:, d, :] = decay * c_d + kw[:, d:d + 1] * v
    num = w_inter * qc + s * v
    den = w_inter * jnp.sum(q * n, axis=1, keepdims=True) + s
    hh = num * (1.0 / jnp.maximum(jnp.abs(den), jnp.exp(-m_t)))
    n_out_ref[0] = decay * n + kw

    @pl.when(h == 0)
    def _():
        m_out_ref[...] = jnp.zeros(m_out_ref.shape, F32)

    m_out_ref[...] = jnp.where(lane_m == h, m_t, m_out_ref[...])
    hg_ref[...] = (_rms(hh, nw_ref[...]) * _sigmoid(o_ref[...])).astype(hg_ref.dtype)


def _mlstm_step(c, n, m, q, k, v, o, g, nw):
    b = c.shape[0]
    head3 = pl.BlockSpec((1, b, DQK), lambda h: (h, 0, 0))
    lanes = pl.BlockSpec((b, DV), lambda h: (0, h))
    cspec = pl.BlockSpec((b, DQK, DV), lambda h: (0, h, 0))
    whole = lambda shape: pl.BlockSpec(shape, lambda h: (0,) * len(shape))
    return pl.pallas_call(
        _mlstm_step_body,
        grid=(HEADS,),
        in_specs=[cspec, head3, whole((b, HEADS)), head3, head3, lanes, lanes, whole((b, 2 * HEADS)),
                  pl.BlockSpec((1, DV), lambda h: (0, h))],
        out_specs=[lanes, cspec, head3, whole((b, HEADS))],
        out_shape=[jax.ShapeDtypeStruct((b, HEADS * DV), BF16), jax.ShapeDtypeStruct(c.shape, F32),
                   jax.ShapeDtypeStruct(n.shape, F32), jax.ShapeDtypeStruct((b, HEADS), F32)],
        compiler_params=_params(1), name="mlstm_step",
    )(c, n, m, q, k, v, o, g, nw)


def _bias_body(bucket_ref, rel_ref, out_ref):
    bucket = bucket_ref[...]
    for h in range(Q_HEADS):
        acc = jnp.zeros(bucket.shape, F32)
        for b in range(REL_BUCKETS):
            acc = jnp.where(bucket == b, rel_ref[b, h], acc)
        out_ref[h] = acc


def _bias_table(bucket, rel_bias):
    shape = bucket.shape
    return pl.pallas_call(
        _bias_body,
        in_specs=[pl.BlockSpec(memory_space=pltpu.VMEM), pl.BlockSpec(memory_space=pltpu.SMEM)],
        out_specs=pl.BlockSpec(memory_space=pltpu.VMEM),
        out_shape=jax.ShapeDtypeStruct((Q_HEADS,) + shape, F32),
        name="rel_bias_table",
    )(bucket, rel_bias)


def _rel_bucket(dist):
    max_exact = REL_BUCKETS // 2
    d = jnp.maximum(dist, 0)
    df = jnp.maximum(d, 1).astype(F32)
    large = max_exact + (jnp.log(df / max_exact) / math.log(REL_MAX_DIST / max_exact)
                         * (REL_BUCKETS - max_exact)).astype(jnp.int32)
    large = jnp.minimum(large, REL_BUCKETS - 1)
    return jnp.where(d < max_exact, d, large)


def _swa_prompt_body(q_ref, ktp_ref, ktc_ref, vp_ref, vc_ref, bias_ref, sink_ref, out_ref):
    n = pl.program_id(1)
    blk = WINDOW
    qi = lax.broadcasted_iota(jnp.int32, (blk, 2 * blk), 0)
    kj = lax.broadcasted_iota(jnp.int32, (blk, 2 * blk), 1)
    dist = qi + blk - kj
    valid = (dist >= 0) & (dist < WINDOW) & ((kj >= blk) | (n > 0))
    lane = lax.broadcasted_iota(jnp.int32, (2 * blk, LANES), 1)
    lane_q = lax.broadcasted_iota(jnp.int32, (blk, LANES), 1)
    zero_k = jnp.zeros((HEAD_DIM, 2 * blk), BF16)
    ones_bd = jnp.concatenate([(lane < HEAD_DIM).astype(BF16), (lane >= HEAD_DIM).astype(BF16)], axis=0)

    for g in range(KV_HEADS):
        kt_g = jnp.concatenate([ktp_ref[g * HEAD_DIM:(g + 1) * HEAD_DIM, :],
                                ktc_ref[g * HEAD_DIM:(g + 1) * HEAD_DIM, :]], axis=1)
        rhs = jnp.concatenate([jnp.concatenate([kt_g, zero_k], axis=0),
                               jnp.concatenate([zero_k, kt_g], axis=0)], axis=1)
        vt = jnp.concatenate([vp_ref[:, (g // 2) * LANES:(g // 2 + 1) * LANES],
                              vc_ref[:, (g // 2) * LANES:(g // 2 + 1) * LANES]], axis=0)
        vr = pltpu.roll(vt, HEAD_DIM, axis=1)
        v_lo = jnp.where(lane < HEAD_DIM, vt if g % 2 == 0 else vr, 0.0)
        v_hi = jnp.where(lane >= HEAD_DIM, vr if g % 2 == 0 else vt, 0.0)
        v_bd = jnp.concatenate([jnp.concatenate([v_lo, v_hi], axis=0).astype(BF16), ones_bd], axis=1)
        for pair in range(GROUP // 2):
            jp = g * (GROUP // 2) + pair
            logits = _dot(q_ref[:, jp * LANES:(jp + 1) * LANES], rhs)
            ps, sk = [], []
            for i in range(2):
                hd = 2 * jp + i
                lg = jnp.where(valid, logits[:, i * 2 * blk:(i + 1) * 2 * blk] + bias_ref[hd], NEG)
                sink = sink_ref[hd]
                mx = jnp.maximum(jnp.max(lg, axis=1, keepdims=True), sink)
                ps.append(jnp.exp(lg - mx).astype(BF16))
                sk.append(jnp.exp(sink - mx))
            pv = _dot(jnp.concatenate(ps, axis=1), v_bd)
            den = pv[:, LANES:] + jnp.where(lane_q < HEAD_DIM, sk[0], sk[1])
            out_ref[:, jp * LANES:(jp + 1) * LANES] = (pv[:, :LANES] * (1.0 / den)).astype(out_ref.dtype)


def _swa_prompt(q, kt, v, bias, sinks, *, batch, seq):
    nb = seq // WINDOW
    t = batch * seq
    cur = lambda b, n: b * nb + n
    prev = lambda b, n: jnp.maximum(b * nb + n - 1, 0)
    return pl.pallas_call(
        _swa_prompt_body,
        grid=(batch, nb),
        in_specs=[pl.BlockSpec((WINDOW, D_MODEL), lambda b, n: (cur(b, n), 0)),
                  pl.BlockSpec((KV_DIM, WINDOW), lambda b, n: (0, prev(b, n))),
                  pl.BlockSpec((KV_DIM, WINDOW), lambda b, n: (0, cur(b, n))),
                  pl.BlockSpec((WINDOW, KV_DIM), lambda b, n: (prev(b, n), 0)),
                  pl.BlockSpec((WINDOW, KV_DIM), lambda b, n: (cur(b, n), 0)),
                  _resident(bias.shape),
                  pl.BlockSpec(memory_space=pltpu.SMEM)],
        out_specs=pl.BlockSpec((WINDOW, D_MODEL), lambda b, n: (cur(b, n), 0)),
        out_shape=jax.ShapeDtypeStruct((t, D_MODEL), BF16),
        compiler_params=_params(2), name="swa_prompt",
    )(q, kt, kt, v, v, bias, sinks)


SAMPLE_BB = 8


def _swa_sample_body(q_ref, kc_ref, vc_ref, kn_ref, vn_ref, bias_ref, sink_ref, out_ref, ko_ref, vo_ref):
    w = WINDOW
    row = lax.broadcasted_iota(jnp.int32, (w, KV_DIM), 0)
    rg = lax.broadcasted_iota(jnp.int32, (Q_HEADS, KV_DIM), 0) // GROUP
    lg = lax.broadcasted_iota(jnp.int32, (Q_HEADS, KV_DIM), 1) // HEAD_DIM
    rsel = lax.broadcasted_iota(jnp.int32, (Q_HEADS, HEAD_DIM), 0) // GROUP
    bias = bias_ref[...]
    sink = sink_ref[...]
    for i in range(SAMPLE_BB):
        k_win = jnp.where(row == w - 1, kn_ref[i], pltpu.roll(kc_ref[i], w - 1, axis=0))
        v_win = jnp.where(row == w - 1, vn_ref[i], pltpu.roll(vc_ref[i], w - 1, axis=0))
        ko_ref[i] = k_win
        vo_ref[i] = v_win
        qb = q_ref[i].astype(F32)
        q_bd = jnp.where(rg == lg, jnp.concatenate([qb] * KV_HEADS, axis=1), 0.0).astype(BF16)
        logits = lax.dot_general(q_bd, k_win.astype(BF16), _NT, preferred_element_type=F32) + bias
        mx = jnp.maximum(jnp.max(logits, axis=1, keepdims=True), sink)
        p = jnp.exp(logits - mx)
        inv = 1.0 / (jnp.sum(p, axis=1, keepdims=True) + jnp.exp(sink - mx))
        o = _dot((p * inv).astype(BF16), v_win.astype(BF16))
        sel = o[:, 0:HEAD_DIM]
        for g in range(1, KV_HEADS):
            sel = jnp.where(rsel == g, o[:, g * HEAD_DIM:(g + 1) * HEAD_DIM], sel)
        out_ref[i] = sel.astype(out_ref.dtype)


def _swa_sample(q, kc, vc, kn, vn, bias, sinks):
    b = q.shape[0]
    bb = SAMPLE_BB
    blk = lambda shape: pl.BlockSpec((bb,) + shape, lambda i: (i, 0, 0))
    return pl.pallas_call(
        _swa_sample_body,
        grid=(b // bb,),
        in_specs=[blk((Q_HEADS, HEAD_DIM)), blk((WINDOW, KV_DIM)), blk((WINDOW, KV_DIM)), blk((1, KV_DIM)),
                  blk((1, KV_DIM)), _resident(bias.shape), _resident((Q_HEADS, 1))],
        out_specs=[blk((Q_HEADS, HEAD_DIM)), blk((WINDOW, KV_DIM)), blk((WINDOW, KV_DIM))],
        out_shape=[jax.ShapeDtypeStruct((b, Q_HEADS, HEAD_DIM), BF16), jax.ShapeDtypeStruct(kc.shape, F32),
                   jax.ShapeDtypeStruct(vc.shape, F32)],
        compiler_params=_params(1), name="swa_sample",
    )(q, kc, vc, kn, vn, bias, sinks)


def _prep_mlstm(w_in, b_i, b_f, prompt):
    hk, hv = HEADS * DQK, HEADS * DV
    wq = (w_in[:, :hk] * (DQK ** -0.5)).astype(BF16)
    wk = w_in[:, hk:2 * hk]
    wg = w_in[:, 2 * hk + hv + D_MODEL:]
    bg = jnp.concatenate([b_i, b_f]).astype(F32)
    return dict(wq=wq, wk=(wk.T if prompt else wk).astype(BF16),
                wv=w_in[:, 2 * hk:2 * hk + hv].astype(BF16),
                wo=w_in[:, 2 * hk + hv:2 * hk + hv + D_MODEL].astype(BF16),
                wg=(wg.T if prompt else wg).astype(BF16), bg=bg.reshape(-1, 1) if prompt else bg.reshape(1, -1))


def _prep_swa(w_qkv):
    wk = w_qkv[:, D_MODEL:D_MODEL + KV_DIM]
    return dict(wq=(w_qkv[:, :D_MODEL] * (HEAD_DIM ** -0.5)).astype(BF16),
                wk=wk.astype(BF16), wkt=wk.T.astype(BF16), wv=w_qkv[:, D_MODEL + KV_DIM:].astype(BF16))


def kernel(x_prompt, x_sample, state_mlstm_C, state_mlstm_n, state_mlstm_m, cache_swa_k, cache_swa_v, p_prompt, p_sample, rel_bias, norm_pre_mix, norm_post_mix, norm_pre_ffn, norm_post_ffn, mlstm_w_in, mlstm_b_i, mlstm_b_f, mlstm_norm, mlstm_w_out, swa_w_qkv, swa_sinks, swa_w_o, ffn_w_up, ffn_w_down, ple_w_proj, ple_norm, ple_w_gate):
    bp, sp, _ = x_prompt.shape
    bs = x_sample.shape[0]
    win = cache_swa_k.shape[2]
    past = 8192
    vec = lambda a: a.reshape(1, -1).astype(F32)

    qi = jnp.arange(WINDOW)[:, None]
    kj = jnp.arange(2 * WINDOW)[None, :]
    bias_p = _bias_table(_rel_bucket(qi + WINDOW - kj).astype(jnp.int32), rel_bias.astype(F32))
    dist_s = (win - 1) - jnp.arange(win)
    bias_s = _bias_table(jnp.broadcast_to(_rel_bucket(dist_s)[None, :], (8, win)).astype(jnp.int32),
                         rel_bias.astype(F32))[:, 0, :]

    hp = x_prompt.reshape(bp * sp, D_MODEL)
    hs = x_sample.reshape(bs, D_MODEL)
    tm_p = 512
    outs = dict(Cp=[], np=[], mp=[], kp=[], vp=[], Cs=[], ns=[], ms=[], ks=[], vs=[])

    for i in range(DEPTH):
        j = i // N_MIXERS
        post_w = dict(npm=vec(norm_post_mix[i]), npf=vec(norm_pre_ffn[i]), wup=ffn_w_up[i].astype(BF16),
                      wdn=ffn_w_down[i].astype(BF16), npff=vec(norm_post_ffn[i]), wpj=ple_w_proj[i].astype(BF16),
                      pln=vec(ple_norm[i]), wgate=ple_w_gate[i].astype(BF16))
        nw = vec(norm_pre_mix[i])
        if i % N_MIXERS == 0:
            post_w["wo"] = mlstm_w_out[j].astype(BF16)
            mnw = vec(mlstm_norm[j])
            q, kt, v, o, gt = _mlstm_in(hp, nw, _prep_mlstm(mlstm_w_in[j], mlstm_b_i[j], mlstm_b_f[j], True),
                                        tm=tm_p, prompt=True, act_dtype=BF16)
            a_p, c_fin, n_fin, m_fin = _mlstm_scan(q, kt, v, o, gt, mnw, batch=bp, seq=sp)
            outs["Cp"].append(c_fin)
            outs["np"].append(n_fin[..., 0])
            outs["mp"].append(m_fin[..., 0])
            q, k, v, o, g = _mlstm_in(hs, nw, _prep_mlstm(mlstm_w_in[j], mlstm_b_i[j], mlstm_b_f[j], False),
                                      tm=bs, prompt=False, act_dtype=F32)
            heads_first = lambda a: a.reshape(bs, HEADS, DQK).transpose(1, 0, 2)
            a_s, c_new, n_new, m_new = _mlstm_step(
                state_mlstm_C[j].astype(F32).reshape(bs, HEADS * DQK, DV),
                state_mlstm_n[j].astype(F32).transpose(1, 0, 2), state_mlstm_m[j].astype(F32),
                heads_first(q), heads_first(k), v, o, g, mnw)
            outs["Cs"].append(c_new.reshape(bs, HEADS, DQK, DV))
            outs["ns"].append(n_new.transpose(1, 0, 2))
            outs["ms"].append(m_new)
        else:
            post_w["wo"] = swa_w_o[j].astype(BF16)
            sw = _prep_swa(swa_w_qkv[j])
            sinks = swa_sinks[j].astype(F32)
            q, k, kt, v = _swa_in(hp, nw, sw, tm=tm_p)
            a_p = _swa_prompt(q, kt, v, bias_p, sinks, batch=bp, seq=sp)
            outs["kp"].append(k.reshape(bp, sp, KV_HEADS, HEAD_DIM)[:, sp - win:])
            outs["vp"].append(v.reshape(bp, sp, KV_HEADS, HEAD_DIM)[:, sp - win:])
            q, k, _, v = _swa_in(hs, nw, sw, tm=bs)
            a_s, k_win, v_win = _swa_sample(
                q.reshape(bs, Q_HEADS, HEAD_DIM),
                cache_swa_k[j].reshape(bs, win, KV_DIM), cache_swa_v[j].reshape(bs, win, KV_DIM),
                k.reshape(bs, 1, KV_DIM), v.reshape(bs, 1, KV_DIM), bias_s, sinks.reshape(Q_HEADS, 1))
            a_s = a_s.reshape(bs, D_MODEL)
            outs["ks"].append(k_win.reshape(bs, win, KV_HEADS, HEAD_DIM))
            outs["vs"].append(v_win.reshape(bs, win, KV_HEADS, HEAD_DIM))
        hp = _post(a_p, hp, p_prompt[i].reshape(bp * sp, PLE_DIM), post_w, tm=tm_p)
        hs = _post(a_s, hs, p_sample[i].reshape(bs, PLE_DIM), post_w, tm=bs)

    st = lambda key, like: jnp.stack(outs[key]).astype(like.dtype)
    return (hp.reshape(x_prompt.shape), hs.reshape(x_sample.shape),
            st("Cp", state_mlstm_C), st("np", state_mlstm_n), st("mp", state_mlstm_m),
            st("kp", cache_swa_k), st("vp", cache_swa_v),
            st("Cs", state_mlstm_C), st("ns", state_mlstm_n), st("ms", state_mlstm_m),
            st("ks", cache_swa_k), st("vs", cache_swa_v))
```

```python
import functools
import math

import jax
import jax.numpy as jnp
from jax import lax
from jax.experimental import pallas as pl
from jax.experimental.pallas import tpu as pltpu

F32 = jnp.float32
BF16 = jnp.bfloat16

D_MODEL = 1024
DEPTH = 4
N_MIXERS = 2
HEADS = 8
DQK = 64
DV = 128
CHUNK = 128
Q_HEADS = 16
KV_HEADS = 4
GROUP = Q_HEADS // KV_HEADS
HEAD_DIM = 64
KV_DIM = KV_HEADS * HEAD_DIM
WINDOW = 128
REL_BUCKETS = 32
REL_MAX_DIST = 128
PLE_DIM = 256
D_FF = 4 * D_MODEL
EPS = 1e-6
NEG = -1e30

HK, HV = HEADS * DQK, HEADS * DV
Q0, K0, V0, O0, G0 = 0, HK, 2 * HK, 2 * HK + HV, 2 * HK + HV + D_MODEL
MLSTM_IN = G0 + 2 * HEADS

V7X_VMEM_BYTES = 64 * 1024 * 1024
VMEM_LIMIT = V7X_VMEM_BYTES - 8 * 1024 * 1024
LANES = 128
ROW_TILE = 512

_NT = (((1,), (1,)), ((), ()))


def _row_tile(t):
    return ROW_TILE if t % ROW_TILE == 0 else t


def _params(n_grid):
    return pltpu.CompilerParams(dimension_semantics=("arbitrary",) * n_grid, vmem_limit_bytes=VMEM_LIMIT)


def _resident(shape):
    nd = len(shape)
    return pl.BlockSpec(shape, lambda *_: (0,) * nd, pipeline_mode=pl.Buffered(1))


def _layer(arr, layer):
    tail = arr.shape[1:]
    return pl.BlockSpec((None,) + tail, lambda *_: (layer,) + (0,) * len(tail), pipeline_mode=pl.Buffered(1))


def _skip_ref(body, pos):
    def wrapped(*refs):
        return body(*refs[:pos], *refs[pos + 1:])
    return wrapped


def _rms(x, w=None):
    y = x * lax.rsqrt(jnp.mean(x * x, axis=-1, keepdims=True) + EPS)
    return y if w is None else y * w


def _log_sigmoid(x):
    return -(jnp.maximum(-x, 0.0) + jnp.log1p(jnp.exp(-jnp.abs(x))))


def _sigmoid(x):
    return 1.0 / (1.0 + jnp.exp(-x))


def _dot(a, b):
    return jnp.dot(a, b, preferred_element_type=F32)


def _mlstm_in_body(h_ref, nw_ref, w_ref, wkt_ref, wgt_ref, bg_ref, q_ref, k_ref, v_ref, o_ref, g_ref, *, prompt):
    u = _rms(h_ref[...], nw_ref[...]).astype(BF16)
    q_ref[...] = (_dot(u, w_ref[:, Q0:K0]) * (DQK ** -0.5)).astype(q_ref.dtype)
    v_ref[...] = _dot(u, w_ref[:, V0:O0]).astype(v_ref.dtype)
    o_ref[...] = _dot(u, w_ref[:, O0:G0])
    if prompt:
        k_ref[...] = lax.dot_general(wkt_ref[...], u, _NT, preferred_element_type=F32).astype(k_ref.dtype)
        gt = lax.dot_general(wgt_ref[...], u, _NT, preferred_element_type=F32) + bg_ref[...]
        ig, lf = gt[:HEADS], _log_sigmoid(gt[HEADS:])
        tm = lf.shape[1]
        r = lax.broadcasted_iota(jnp.int32, (tm, tm), 0)
        t = lax.broadcasted_iota(jnp.int32, (tm, tm), 1)
        same_chunk_upper = ((r <= t) & (r // CHUNK == t // CHUNK)).astype(F32)
        b = jnp.dot(lf, same_chunk_upper, precision=lax.Precision.HIGHEST, preferred_element_type=F32)
        g_ref[...] = jnp.concatenate([ig, lf, b], axis=0)
    else:
        k_ref[...] = _dot(u, w_ref[:, K0:V0]).astype(k_ref.dtype)
        g = _dot(u, w_ref[:, G0:MLSTM_IN]) + bg_ref[...]
        lane = lax.broadcasted_iota(jnp.int32, g.shape, 1)
        g_ref[...] = jnp.where(lane < HEADS, g, _log_sigmoid(g))


def _mlstm_in(h, nw, w, wkt, wgt, bg, *, layer, mixer, prompt, act_dtype):
    t = h.shape[0]
    tm = _row_tile(t)
    row = lambda n: pl.BlockSpec((tm, n), lambda i: (i, 0))
    col = lambda n: pl.BlockSpec((n, tm), lambda i: (0, i))
    kshape, kspec = ((HK, t), col(HK)) if prompt else ((t, HK), row(HK))
    gshape, gspec = ((3 * HEADS, t), col(3 * HEADS)) if prompt else ((t, 2 * HEADS), row(2 * HEADS))
    return pl.pallas_call(
        functools.partial(_mlstm_in_body, prompt=prompt),
        grid=(t // tm,),
        in_specs=[row(D_MODEL), _layer(nw, layer), _layer(w, mixer), _layer(wkt, mixer), _layer(wgt, mixer),
                  _resident(bg.shape)],
        out_specs=[row(HK), kspec, row(HV), row(D_MODEL), gspec],
        out_shape=[jax.ShapeDtypeStruct((t, HK), act_dtype), jax.ShapeDtypeStruct(kshape, act_dtype),
                   jax.ShapeDtypeStruct((t, HV), act_dtype), jax.ShapeDtypeStruct((t, D_MODEL), F32),
                   jax.ShapeDtypeStruct(gshape, F32)],
        compiler_params=_params(1), name="mlstm_in",
    )(h, nw, w, wkt, wgt, bg)


def _swa_in_body(h_ref, nw_ref, w_ref, wkt_ref, q_ref, k_ref, kt_ref, v_ref):
    u = _rms(h_ref[...], nw_ref[...]).astype(BF16)
    q_ref[...] = (_dot(u, w_ref[:, :D_MODEL]) * (HEAD_DIM ** -0.5)).astype(q_ref.dtype)
    k_ref[...] = _dot(u, w_ref[:, D_MODEL:D_MODEL + KV_DIM])
    kt_ref[...] = lax.dot_general(wkt_ref[...], u, _NT, preferred_element_type=F32).astype(kt_ref.dtype)
    v_ref[...] = _dot(u, w_ref[:, D_MODEL + KV_DIM:])


def _swa_in(h, nw, w, wkt, *, layer, mixer):
    t = h.shape[0]
    tm = _row_tile(t)
    row = lambda n: pl.BlockSpec((tm, n), lambda i: (i, 0))
    return pl.pallas_call(
        _swa_in_body,
        grid=(t // tm,),
        in_specs=[row(D_MODEL), _layer(nw, layer), _layer(w, mixer), _layer(wkt, mixer)],
        out_specs=[row(D_MODEL), row(KV_DIM), pl.BlockSpec((KV_DIM, tm), lambda i: (0, i)), row(KV_DIM)],
        out_shape=[jax.ShapeDtypeStruct((t, D_MODEL), BF16), jax.ShapeDtypeStruct((t, KV_DIM), F32),
                   jax.ShapeDtypeStruct((KV_DIM, t), BF16), jax.ShapeDtypeStruct((t, KV_DIM), F32)],
        compiler_params=_params(1), name="swa_in",
    )(h, nw, w, wkt)


FF_CHUNK = 1024


def _post_body(a_ref, h_ref, p_ref, wo_ref, npm_ref, npf_ref, wup_ref, wdn_ref, npff_ref, wpj_ref, pln_ref,
               wgate_ref, out_ref):
    y = _dot(a_ref[...].astype(BF16), wo_ref[...])
    h1 = h_ref[...] + _rms(y, npm_ref[...])
    u = _rms(h1, npf_ref[...]).astype(BF16)
    acc = jnp.zeros(h1.shape, F32)
    for c in range(D_FF // FF_CHUNK):
        a = jnp.maximum(_dot(u, wup_ref[:, c * FF_CHUNK:(c + 1) * FF_CHUNK]), 0.0)
        acc = acc + _dot((a * a).astype(BF16), wdn_ref[c * FF_CHUNK:(c + 1) * FF_CHUNK, :])
    h2 = h1 + _rms(acc, npff_ref[...])
    e = _rms(_dot(p_ref[...].astype(BF16), wpj_ref[...]), pln_ref[...])
    g = _sigmoid(_dot(_rms(h2).astype(BF16), wgate_ref[...]))
    out_ref[...] = h2 + e * g


def _post(a, h, p, wo, w, *, layer, mixer):
    t = h.shape[0]
    tm = _row_tile(t)
    row = lambda n: pl.BlockSpec((tm, n), lambda i: (i, 0))
    lay = lambda key: _layer(w[key], layer)
    return pl.pallas_call(
        _post_body,
        grid=(t // tm,),
        in_specs=[row(D_MODEL), row(D_MODEL), pl.BlockSpec((None, tm, PLE_DIM), lambda i: (layer, i, 0)),
                  _layer(wo, mixer), lay("npm"), lay("npf"), lay("wup"), lay("wdn"), lay("npff"), lay("wpj"),
                  lay("pln"), lay("wgate")],
        out_specs=row(D_MODEL),
        out_shape=jax.ShapeDtypeStruct((t, D_MODEL), F32),
        compiler_params=_params(1), name="post",
    )(a, h, p, wo, w["npm"], w["npf"], w["wup"], w["wdn"], w["npff"], w["wpj"], w["pln"], w["wgate"])


STATE_W = 2 * DV


def _mlstm_scan_body(q_ref, kt_ref, v_ref, o_ref, gt_ref, nw_ref,
                     hg_ref, c_out_ref, n_out_ref, m_out_ref, cn_scr, m_scr):
    c = pl.program_id(1)
    L = CHUNK

    @pl.when(c == 0)
    def _():
        cn_scr[...] = jnp.zeros(cn_scr.shape, F32)
        m_scr[...] = jnp.zeros(m_scr.shape, F32)

    ri = lax.broadcasted_iota(jnp.int32, (L, L), 0)
    ci = lax.broadcasted_iota(jnp.int32, (L, L), 1)
    tri = ci <= ri
    gt = gt_ref[...]
    ig_rows, lf_rows, b_rows = gt[:HEADS], gt[HEADS:2 * HEADS], gt[2 * HEADS:]
    a_rows = ig_rows - b_rows
    m_prev = m_scr[...]
    b_last = jnp.broadcast_to(b_rows[:, L - 1:L], (HEADS, L))
    m_new = b_last + jnp.maximum(m_prev, jnp.max(a_rows, axis=1, keepdims=True))
    decay_rows = jnp.exp(b_last + m_prev - m_new)
    wk_rows = jnp.exp(b_last - b_rows + ig_rows - m_new)
    m_scr[...] = m_new

    ones_rhs = jnp.ones((L, DV), BF16)
    zeros_half = jnp.zeros((DQK, L), BF16)
    for h in range(HEADS):
        j, par = h // 2, h % 2
        q_pair = q_ref[:, j * 128:(j + 1) * 128]
        kt_h = kt_ref[h * DQK:(h + 1) * DQK, :]
        kt_m = jnp.concatenate([kt_h, zeros_half] if par == 0 else [zeros_half, kt_h], axis=0)
        v_ext = jnp.concatenate([v_ref[:, h * DV:(h + 1) * DV], ones_rhs], axis=1)
        cn = cn_scr[h]
        a_row, m_row = a_rows[h:h + 1], m_prev[h:h + 1]

        big_m = jnp.maximum(jnp.max(jnp.where(tri, a_row, -jnp.inf), axis=1, keepdims=True), m_row)
        b_col = jnp.sum(jnp.where(tri, lf_rows[h:h + 1], 0.0), axis=1, keepdims=True)
        w_inter = jnp.exp(m_row - big_m)
        s = _dot(q_pair, kt_m) * jnp.exp(jnp.where(tri, a_row - big_m, NEG))
        sv = _dot(s.astype(BF16), v_ext)
        qc = _dot(q_pair, cn.astype(BF16))
        num = w_inter * qc[:, :DV] + sv[:, :DV]
        den = w_inter * qc[:, DV:] + sv[:, DV:]
        hh = num * (1.0 / jnp.maximum(jnp.abs(den), jnp.exp(-(b_col + big_m))))

        kw_t = kt_m.astype(F32) * wk_rows[h:h + 1]
        dec = decay_rows[h:h + 1]
        cn_scr[h] = jnp.concatenate([dec, dec], axis=1) * cn + _dot(kw_t.astype(BF16), v_ext)

        hn = _rms(hh, nw_ref[:, h * DV:(h + 1) * DV])
        hg_ref[:, h * DV:(h + 1) * DV] = (hn * _sigmoid(o_ref[:, h * DV:(h + 1) * DV])).astype(hg_ref.dtype)

    @pl.when(c == pl.num_programs(1) - 1)
    def _():
        for h in range(HEADS):
            par = h % 2
            st = cn_scr[h, par * DQK:(par + 1) * DQK, :]
            c_out_ref[0, h] = st[:, :DV]
            n_out_ref[0, h] = st[:, DV:]
        m_out_ref[0] = m_scr[...]


def _mlstm_scan(q, kt, v, o, gt, nw, *, mixer, batch, seq):
    nc = seq // CHUNK
    t = batch * seq
    rows = lambda n: pl.BlockSpec((CHUNK, n), lambda b, c: (b * nc + c, 0))
    cols = lambda n: pl.BlockSpec((n, CHUNK), lambda b, c: (0, b * nc + c))
    return pl.pallas_call(
        _mlstm_scan_body,
        grid=(batch, nc),
        in_specs=[rows(HK), cols(HK), rows(HV), rows(D_MODEL), cols(3 * HEADS), _layer(nw, mixer)],
        out_specs=[rows(HV),
                   pl.BlockSpec((1, HEADS, DQK, DV), lambda b, c: (b, 0, 0, 0)),
                   pl.BlockSpec((1, HEADS, DQK, STATE_W - DV), lambda b, c: (b, 0, 0, 0)),
                   pl.BlockSpec((1, HEADS, LANES), lambda b, c: (b, 0, 0))],
        out_shape=[jax.ShapeDtypeStruct((t, HV), BF16),
                   jax.ShapeDtypeStruct((batch, HEADS, DQK, DV), F32),
                   jax.ShapeDtypeStruct((batch, HEADS, DQK, STATE_W - DV), F32),
                   jax.ShapeDtypeStruct((batch, HEADS, LANES), F32)],
        scratch_shapes=[pltpu.VMEM((HEADS, 2 * DQK, STATE_W), F32), pltpu.VMEM((HEADS, LANES), F32)],
        compiler_params=_params(2), name="mlstm_scan",
    )(q, kt, v, o, gt, nw)


def _mlstm_step_body(c_ref, n_ref, m_ref, q_ref, k_ref, v_ref, o_ref, g_ref, nw_ref,
                     hg_ref, c_out_ref, n_out_ref, m_out_ref):
    h = pl.program_id(0)
    q = q_ref[0]
    k = k_ref[0]
    v = v_ref[...]
    n = n_ref[0]
    g = g_ref[...]
    lane_g = lax.broadcasted_iota(jnp.int32, g.shape, 1)
    ig = jnp.sum(jnp.where(lane_g == h, g, 0.0), axis=1, keepdims=True)
    lf = jnp.sum(jnp.where(lane_g == h + HEADS, g, 0.0), axis=1, keepdims=True)
    m_all = m_ref[...]
    lane_m = lax.broadcasted_iota(jnp.int32, m_all.shape, 1)
    m_prev = jnp.sum(jnp.where(lane_m == h, m_all, 0.0), axis=1, keepdims=True)

    inter = lf + m_prev
    m_t = jnp.maximum(inter, ig)
    w_inter = jnp.exp(inter - m_t)
    s = jnp.sum(q * k, axis=1, keepdims=True) * jnp.exp(ig - m_t)
    decay = jnp.exp(lf + m_prev - m_t)
    kw = k * jnp.exp(ig - m_t)

    qc = jnp.zeros(v.shape, F32)
    for d in range(DQK):
        c_d = c_ref[:, d, :]
        qc = qc + q[:, d:d + 1] * c_d
        c_out_ref[:, d, :] = decay * c_d + kw[:, d:d + 1] * v
    num = w_inter * qc + s * v
    den = w_inter * jnp.sum(q * n, axis=1, keepdims=True) + s
    hh = num * (1.0 / jnp.maximum(jnp.abs(den), jnp.exp(-m_t)))
    n_out_ref[0] = decay * n + kw

    @pl.when(h == 0)
    def _():
        m_out_ref[...] = jnp.zeros(m_out_ref.shape, F32)

    m_out_ref[...] = jnp.where(lane_m == h, m_t, m_out_ref[...])
    hg_ref[...] = (_rms(hh, nw_ref[...]) * _sigmoid(o_ref[...])).astype(hg_ref.dtype)


def _mlstm_step(c_all, c_new_prev, n, m, q, k, v, o, g, nw, *, mixer):
    b = c_all.shape[1]
    head3 = pl.BlockSpec((1, b, DQK), lambda h: (h, 0, 0))
    lanes = pl.BlockSpec((b, DV), lambda h: (0, h))
    cspec = pl.BlockSpec((None, b, DQK, DV), lambda h: (mixer, 0, h, 0))
    whole = lambda shape: pl.BlockSpec(shape, lambda h: (0,) * len(shape))
    in_specs = [cspec, head3, whole((b, HEADS)), head3, head3, lanes, lanes, whole((b, 2 * HEADS)),
                pl.BlockSpec((None, 1, DV), lambda h: (mixer, 0, h))]
    args = [c_all, n, m, q, k, v, o, g, nw]
    body, aliases = _mlstm_step_body, {}
    if c_new_prev is not None:
        in_specs.append(pl.BlockSpec(memory_space=pl.ANY))
        body, aliases = _skip_ref(body, len(args)), {len(args): 1}
        args.append(c_new_prev)
    return pl.pallas_call(
        body,
        grid=(HEADS,),
        in_specs=in_specs,
        out_specs=[lanes, cspec, head3, whole((b, HEADS))],
        out_shape=[jax.ShapeDtypeStruct((b, HV), BF16), jax.ShapeDtypeStruct(c_all.shape, F32),
                   jax.ShapeDtypeStruct(n.shape, F32), jax.ShapeDtypeStruct((b, HEADS), F32)],
        input_output_aliases=aliases,
        compiler_params=_params(1), name="mlstm_step",
    )(*args)


def _bias_body(bucket_ref, rel_ref, out_ref):
    bucket = bucket_ref[...]
    for h in range(Q_HEADS):
        acc = jnp.zeros(bucket.shape, F32)
        for b in range(REL_BUCKETS):
            acc = jnp.where(bucket == b, rel_ref[b, h], acc)
        out_ref[h] = acc


def _bias_table(bucket, rel_bias):
    shape = bucket.shape
    return pl.pallas_call(
        _bias_body,
        in_specs=[pl.BlockSpec(memory_space=pltpu.VMEM), pl.BlockSpec(memory_space=pltpu.SMEM)],
        out_specs=pl.BlockSpec(memory_space=pltpu.VMEM),
        out_shape=jax.ShapeDtypeStruct((Q_HEADS,) + shape, F32),
        name="rel_bias_table",
    )(bucket, rel_bias)


def _rel_bucket(dist):
    max_exact = REL_BUCKETS // 2
    d = jnp.maximum(dist, 0)
    df = jnp.maximum(d, 1).astype(F32)
    large = max_exact + (jnp.log(df / max_exact) / math.log(REL_MAX_DIST / max_exact)
                         * (REL_BUCKETS - max_exact)).astype(jnp.int32)
    large = jnp.minimum(large, REL_BUCKETS - 1)
    return jnp.where(d < max_exact, d, large)


def _swa_prompt_body(q_ref, ktp_ref, ktc_ref, vp_ref, vc_ref, bias_ref, sink_ref, out_ref):
    n = pl.program_id(1)
    blk = WINDOW
    qi = lax.broadcasted_iota(jnp.int32, (blk, 2 * blk), 0)
    kj = lax.broadcasted_iota(jnp.int32, (blk, 2 * blk), 1)
    dist = qi + blk - kj
    valid = (dist >= 0) & (dist < WINDOW) & ((kj >= blk) | (n > 0))
    lane = lax.broadcasted_iota(jnp.int32, (2 * blk, LANES), 1)
    lane_q = lax.broadcasted_iota(jnp.int32, (blk, LANES), 1)
    zero_k = jnp.zeros((HEAD_DIM, 2 * blk), BF16)
    ones_bd = jnp.concatenate([(lane < HEAD_DIM).astype(BF16), (lane >= HEAD_DIM).astype(BF16)], axis=0)

    for g in range(KV_HEADS):
        kt_g = jnp.concatenate([ktp_ref[g * HEAD_DIM:(g + 1) * HEAD_DIM, :],
                                ktc_ref[g * HEAD_DIM:(g + 1) * HEAD_DIM, :]], axis=1)
        rhs = jnp.concatenate([jnp.concatenate([kt_g, zero_k], axis=0),
                               jnp.concatenate([zero_k, kt_g], axis=0)], axis=1)
        vt = jnp.concatenate([vp_ref[:, (g // 2) * LANES:(g // 2 + 1) * LANES],
                              vc_ref[:, (g // 2) * LANES:(g // 2 + 1) * LANES]], axis=0)
        vr = pltpu.roll(vt, HEAD_DIM, axis=1)
        v_lo = jnp.where(lane < HEAD_DIM, vt if g % 2 == 0 else vr, 0.0)
        v_hi = jnp.where(lane >= HEAD_DIM, vr if g % 2 == 0 else vt, 0.0)
        v_bd = jnp.concatenate([jnp.concatenate([v_lo, v_hi], axis=0).astype(BF16), ones_bd], axis=1)
        for pair in range(GROUP // 2):
            jp = g * (GROUP // 2) + pair
            logits = _dot(q_ref[:, jp * LANES:(jp + 1) * LANES], rhs)
            ps, sk = [], []
            for i in range(2):
                hd = 2 * jp + i
                lg = jnp.where(valid, logits[:, i * 2 * blk:(i + 1) * 2 * blk] + bias_ref[hd], NEG)
                sink = sink_ref[hd]
                mx = jnp.maximum(jnp.max(lg, axis=1, keepdims=True), sink)
                ps.append(jnp.exp(lg - mx).astype(BF16))
                sk.append(jnp.exp(sink - mx))
            pv = _dot(jnp.concatenate(ps, axis=1), v_bd)
            den = pv[:, LANES:] + jnp.where(lane_q < HEAD_DIM, sk[0], sk[1])
            out_ref[:, jp * LANES:(jp + 1) * LANES] = (pv[:, :LANES] * (1.0 / den)).astype(out_ref.dtype)


def _swa_prompt(q, kt, v, bias, sinks, *, mixer, batch, seq):
    nb = seq // WINDOW
    t = batch * seq
    cur = lambda b, n: b * nb + n
    prev = lambda b, n: jnp.maximum(b * nb + n - 1, 0)
    return pl.pallas_call(
        _swa_prompt_body,
        grid=(batch, nb),
        in_specs=[pl.BlockSpec((WINDOW, D_MODEL), lambda b, n: (cur(b, n), 0)),
                  pl.BlockSpec((KV_DIM, WINDOW), lambda b, n: (0, prev(b, n))),
                  pl.BlockSpec((KV_DIM, WINDOW), lambda b, n: (0, cur(b, n))),
                  pl.BlockSpec((WINDOW, KV_DIM), lambda b, n: (prev(b, n), 0)),
                  pl.BlockSpec((WINDOW, KV_DIM), lambda b, n: (cur(b, n), 0)),
                  _resident(bias.shape),
                  pl.BlockSpec(memory_space=pltpu.SMEM)],
        out_specs=pl.BlockSpec((WINDOW, D_MODEL), lambda b, n: (cur(b, n), 0)),
        out_shape=jax.ShapeDtypeStruct((t, D_MODEL), BF16),
        compiler_params=_params(2), name="swa_prompt",
    )(q, kt, kt, v, v, bias, sinks[mixer])


SAMPLE_BB = 8


def _swa_sample_body(q_ref, kc_ref, vc_ref, kn_ref, vn_ref, bias_ref, sink_ref, out_ref, ko_ref, vo_ref):
    w = WINDOW
    row = lax.broadcasted_iota(jnp.int32, (w, KV_DIM), 0)
    rg = lax.broadcasted_iota(jnp.int32, (Q_HEADS, KV_DIM), 0) // GROUP
    lg = lax.broadcasted_iota(jnp.int32, (Q_HEADS, KV_DIM), 1) // HEAD_DIM
    rsel = lax.broadcasted_iota(jnp.int32, (Q_HEADS, HEAD_DIM), 0) // GROUP
    bias = bias_ref[...]
    sink = sink_ref[...]
    for i in range(SAMPLE_BB):
        k_win = jnp.where(row == w - 1, kn_ref[i], pltpu.roll(kc_ref[i], w - 1, axis=0))
        v_win = jnp.where(row == w - 1, vn_ref[i], pltpu.roll(vc_ref[i], w - 1, axis=0))
        ko_ref[i] = k_win
        vo_ref[i] = v_win
        qb = q_ref[i].astype(F32)
        q_bd = jnp.where(rg == lg, jnp.concatenate([qb] * KV_HEADS, axis=1), 0.0).astype(BF16)
        logits = lax.dot_general(q_bd, k_win.astype(BF16), _NT, preferred_element_type=F32) + bias
        mx = jnp.maximum(jnp.max(logits, axis=1, keepdims=True), sink)
        p = jnp.exp(logits - mx)
        inv = 1.0 / (jnp.sum(p, axis=1, keepdims=True) + jnp.exp(sink - mx))
        o = _dot((p * inv).astype(BF16), v_win.astype(BF16))
        sel = o[:, 0:HEAD_DIM]
        for g in range(1, KV_HEADS):
            sel = jnp.where(rsel == g, o[:, g * HEAD_DIM:(g + 1) * HEAD_DIM], sel)
        out_ref[i] = sel.astype(out_ref.dtype)


def _swa_sample(q, kc_all, vc_all, kn, vn, bias, sinks, k_new_prev, v_new_prev, *, mixer):
    b = q.shape[0]
    bb = SAMPLE_BB
    blk = lambda shape: pl.BlockSpec((bb,) + shape, lambda i: (i, 0, 0))
    cache = pl.BlockSpec((None, bb, WINDOW, KV_DIM), lambda i: (mixer, i, 0, 0))
    in_specs = [blk((Q_HEADS, HEAD_DIM)), cache, cache, blk((1, KV_DIM)), blk((1, KV_DIM)), _resident(bias.shape),
                _layer(sinks, mixer)]
    args = [q, kc_all, vc_all, kn, vn, bias, sinks]
    body, aliases = _swa_sample_body, {}
    if k_new_prev is not None:
        in_specs += [pl.BlockSpec(memory_space=pl.ANY)] * 2
        body = _skip_ref(_skip_ref(body, len(args)), len(args))
        aliases = {len(args): 1, len(args) + 1: 2}
        args += [k_new_prev, v_new_prev]
    return pl.pallas_call(
        body,
        grid=(b // bb,),
        in_specs=in_specs,
        out_specs=[blk((Q_HEADS, HEAD_DIM)), cache, cache],
        out_shape=[jax.ShapeDtypeStruct((b, Q_HEADS, HEAD_DIM), BF16), jax.ShapeDtypeStruct(kc_all.shape, F32),
                   jax.ShapeDtypeStruct(vc_all.shape, F32)],
        input_output_aliases=aliases,
        compiler_params=_params(1), name="swa_sample",
    )(*args)


def kernel(x_prompt, x_sample, state_mlstm_C, state_mlstm_n, state_mlstm_m, cache_swa_k, cache_swa_v, p_prompt, p_sample, rel_bias, norm_pre_mix, norm_post_mix, norm_pre_ffn, norm_post_ffn, mlstm_w_in, mlstm_b_i, mlstm_b_f, mlstm_norm, mlstm_w_out, swa_w_qkv, swa_sinks, swa_w_o, ffn_w_up, ffn_w_down, ple_w_proj, ple_norm, ple_w_gate):
    bp, sp, _ = x_prompt.shape
    bs = x_sample.shape[0]
    win = cache_swa_k.shape[2]
    assert win == WINDOW and x_sample.shape[1] == 1
    bf = lambda a: a.astype(BF16)
    vecs = lambda a: a.astype(F32)[:, None, :]
    t_last = lambda a: jnp.swapaxes(a, 1, 2)

    post_w = dict(npm=vecs(norm_post_mix), npf=vecs(norm_pre_ffn), wup=bf(ffn_w_up), wdn=bf(ffn_w_down),
                  npff=vecs(norm_post_ffn), wpj=bf(ple_w_proj), pln=vecs(ple_norm), wgate=bf(ple_w_gate))
    n_pre = vecs(norm_pre_mix)
    m_w, m_wkt, m_wgt = bf(mlstm_w_in), bf(t_last(mlstm_w_in[:, :, K0:V0])), bf(t_last(mlstm_w_in[:, :, G0:]))
    m_bias = jnp.concatenate([mlstm_b_i, mlstm_b_f], axis=1).astype(F32)
    m_wout, m_norm = bf(mlstm_w_out), vecs(mlstm_norm)
    s_w, s_wkt, s_wo = bf(swa_w_qkv), bf(t_last(swa_w_qkv[:, :, D_MODEL:D_MODEL + KV_DIM])), bf(swa_w_o)
    sinks = swa_sinks.astype(F32)

    qi = jnp.arange(WINDOW)[:, None]
    kj = jnp.arange(2 * WINDOW)[None, :]
    bias_p = _bias_table(_rel_bucket(qi + WINDOW - kj).astype(jnp.int32), rel_bias.astype(F32))
    dist_s = (win - 1) - jnp.arange(win)
    bias_s = _bias_table(jnp.broadcast_to(_rel_bucket(dist_s)[None, :], (8, win)).astype(jnp.int32),
                         rel_bias.astype(F32))[:, 0, :]

    hp = x_prompt.reshape(bp * sp, D_MODEL)
    hs = x_sample.reshape(bs, D_MODEL)
    pp = p_prompt.reshape(DEPTH, bp * sp, PLE_DIM)
    ps = p_sample.reshape(DEPTH, bs, PLE_DIM)
    c_old = state_mlstm_C.astype(F32).reshape(-1, bs, HK, DV)
    kc_old = cache_swa_k.astype(F32).reshape(-1, bs, win, KV_DIM)
    vc_old = cache_swa_v.astype(F32).reshape(-1, bs, win, KV_DIM)
    c_new = k_new = v_new = None
    outs = dict(Cp=[], np=[], mp=[], kp=[], vp=[], ns=[], ms=[])

    for i in range(DEPTH):
        j = i // N_MIXERS
        if i % N_MIXERS == 0:
            wo = m_wout
            q, kt, v, o, gt = _mlstm_in(hp, n_pre, m_w, m_wkt, m_wgt, m_bias[j].reshape(-1, 1),
                                        layer=i, mixer=j, prompt=True, act_dtype=BF16)
            a_p, c_fin, n_fin, m_fin = _mlstm_scan(q, kt, v, o, gt, m_norm, mixer=j, batch=bp, seq=sp)
            outs["Cp"].append(c_fin)
            outs["np"].append(n_fin[..., 0])
            outs["mp"].append(m_fin[..., 0])
            q, k, v, o, g = _mlstm_in(hs, n_pre, m_w, m_wkt, m_wgt, m_bias[j].reshape(1, -1),
                                      layer=i, mixer=j, prompt=False, act_dtype=F32)
            heads_first = lambda a: a.reshape(bs, HEADS, DQK).transpose(1, 0, 2)
            a_s, c_new, n_new, m_new = _mlstm_step(
                c_old, c_new, state_mlstm_n[j].astype(F32).transpose(1, 0, 2), state_mlstm_m[j].astype(F32),
                heads_first(q), heads_first(k), v, o, g, m_norm, mixer=j)
            outs["ns"].append(n_new.transpose(1, 0, 2))
            outs["ms"].append(m_new)
        else:
            wo = s_wo
            q, k, kt, v = _swa_in(hp, n_pre, s_w, s_wkt, layer=i, mixer=j)
            a_p = _swa_prompt(q, kt, v, bias_p, sinks, mixer=j, batch=bp, seq=sp)
            last = lambda a: a.reshape(bp, sp, KV_DIM)[:, sp - win:].reshape(bp, win, KV_HEADS, HEAD_DIM)
            outs["kp"].append(last(k))
            outs["vp"].append(last(v))
            q, k, _, v = _swa_in(hs, n_pre, s_w, s_wkt, layer=i, mixer=j)
            a_s, k_new, v_new = _swa_sample(
                q.reshape(bs, Q_HEADS, HEAD_DIM), kc_old, vc_old, k.reshape(bs, 1, KV_DIM), v.reshape(bs, 1, KV_DIM),
                bias_s, sinks[:, :, None], k_new, v_new, mixer=j)
            a_s = a_s.reshape(bs, D_MODEL)
        hp = _post(a_p, hp, pp, wo, post_w, layer=i, mixer=j)
        hs = _post(a_s, hs, ps, wo, post_w, layer=i, mixer=j)

    st = lambda key, like: jnp.stack(outs[key]).astype(like.dtype)
    kv_out = lambda a, like: a.reshape(-1, bs, win, KV_HEADS, HEAD_DIM).astype(like.dtype)
    return (hp.reshape(x_prompt.shape), hs.reshape(x_sample.shape),
            st("Cp", state_mlstm_C), st("np", state_mlstm_n), st("mp", state_mlstm_m),
            st("kp", cache_swa_k), st("vp", cache_swa_v),
            c_new.reshape(state_mlstm_C.shape).astype(state_mlstm_C.dtype), st("ns", state_mlstm_n),
            st("ms", state_mlstm_m), kv_out(k_new, cache_swa_k), kv_out(v_new, cache_swa_v))
```

```python
import functools
import math

import jax
import jax.numpy as jnp
from jax import lax
from jax.experimental import pallas as pl
from jax.experimental.pallas import tpu as pltpu

F32 = jnp.float32
BF16 = jnp.bfloat16

D_MODEL = 1024
DEPTH = 4
N_MIXERS = 2
HEADS = 8
DQK = 64
DV = 128
CHUNK = 128
Q_HEADS = 16
KV_HEADS = 4
GROUP = Q_HEADS // KV_HEADS
HEAD_DIM = 64
KV_DIM = KV_HEADS * HEAD_DIM
WINDOW = 128
REL_BUCKETS = 32
REL_MAX_DIST = 128
PLE_DIM = 256
D_FF = 4 * D_MODEL
EPS = 1e-6
NEG = -1e30

HK, HV = HEADS * DQK, HEADS * DV
Q0, K0, V0, O0, G0 = 0, HK, 2 * HK, 2 * HK + HV, 2 * HK + HV + D_MODEL
MLSTM_IN = G0 + 2 * HEADS

V7X_VMEM_BYTES = 64 * 1024 * 1024
VMEM_LIMIT = V7X_VMEM_BYTES - 8 * 1024 * 1024
LANES = 128
ROW_TILE = 512

_NT = (((1,), (1,)), ((), ()))


def _row_tile(t):
    return ROW_TILE if t % ROW_TILE == 0 else t


def _params(n_grid):
    return pltpu.CompilerParams(dimension_semantics=("arbitrary",) * n_grid, vmem_limit_bytes=VMEM_LIMIT)


def _resident(shape):
    nd = len(shape)
    return pl.BlockSpec(shape, lambda *_: (0,) * nd, pipeline_mode=pl.Buffered(1))


def _layer(arr, layer):
    tail = arr.shape[1:]
    return pl.BlockSpec((None,) + tail, lambda *_: (layer,) + (0,) * len(tail), pipeline_mode=pl.Buffered(1))


def _skip_ref(body, pos):
    def wrapped(*refs):
        return body(*refs[:pos], *refs[pos + 1:])
    return wrapped


def _rms(x, w=None):
    y = x * lax.rsqrt(jnp.mean(x * x, axis=-1, keepdims=True) + EPS)
    return y if w is None else y * w


def _log_sigmoid(x):
    return -(jnp.maximum(-x, 0.0) + jnp.log1p(jnp.exp(-jnp.abs(x))))


def _sigmoid(x):
    return 1.0 / (1.0 + jnp.exp(-x))


def _dot(a, b):
    return jnp.dot(a, b, preferred_element_type=F32)


def _mlstm_in_body(h_ref, nw_ref, w_ref, wkgt_ref, bg_ref, q_ref, k_ref, v_ref, o_ref, g_ref, *, prompt):
    u = _rms(h_ref[...], nw_ref[...]).astype(BF16)
    q_ref[...] = (_dot(u, w_ref[:, Q0:K0]) * (DQK ** -0.5)).astype(q_ref.dtype)
    v_ref[...] = _dot(u, w_ref[:, V0:O0]).astype(v_ref.dtype)
    o_ref[...] = _dot(u, w_ref[:, O0:G0])
    if prompt:
        kgt = lax.dot_general(wkgt_ref[...], u, _NT, preferred_element_type=F32)
        k_ref[...] = kgt[:HK].astype(k_ref.dtype)
        gt = kgt[HK:] + bg_ref[...]
        ig, lf = gt[:HEADS], _log_sigmoid(gt[HEADS:])
        tm = lf.shape[1]
        r = lax.broadcasted_iota(jnp.int32, (tm, tm), 0)
        t = lax.broadcasted_iota(jnp.int32, (tm, tm), 1)
        same_chunk_upper = ((r <= t) & (r // CHUNK == t // CHUNK)).astype(F32)
        b = jnp.dot(lf, same_chunk_upper, precision=lax.Precision.HIGHEST, preferred_element_type=F32)
        g_ref[...] = jnp.concatenate([ig, lf, b], axis=0)
    else:
        k_ref[...] = _dot(u, w_ref[:, K0:V0]).astype(k_ref.dtype)
        g = _dot(u, w_ref[:, G0:MLSTM_IN]) + bg_ref[...]
        lane = lax.broadcasted_iota(jnp.int32, g.shape, 1)
        g_ref[...] = jnp.where(lane < HEADS, g, _log_sigmoid(g))


def _mlstm_in(h, nw, w, wkgt, bg, *, layer, mixer, prompt, act_dtype):
    t = h.shape[0]
    tm = _row_tile(t)
    row = lambda n: pl.BlockSpec((tm, n), lambda i: (i, 0))
    col = lambda n: pl.BlockSpec((n, tm), lambda i: (0, i))
    kshape, kspec = ((HK, t), col(HK)) if prompt else ((t, HK), row(HK))
    gshape, gspec = ((3 * HEADS, t), col(3 * HEADS)) if prompt else ((t, 2 * HEADS), row(2 * HEADS))
    return pl.pallas_call(
        functools.partial(_mlstm_in_body, prompt=prompt),
        grid=(t // tm,),
        in_specs=[row(D_MODEL), _layer(nw, layer), _layer(w, mixer), _layer(wkgt, mixer), _resident(bg.shape)],
        out_specs=[row(HK), kspec, row(HV), row(D_MODEL), gspec],
        out_shape=[jax.ShapeDtypeStruct((t, HK), act_dtype), jax.ShapeDtypeStruct(kshape, act_dtype),
                   jax.ShapeDtypeStruct((t, HV), act_dtype), jax.ShapeDtypeStruct((t, D_MODEL), F32),
                   jax.ShapeDtypeStruct(gshape, F32)],
        compiler_params=_params(1), name="mlstm_in",
    )(h, nw, w, wkgt, bg)


def _swa_in_body(h_ref, nw_ref, w_ref, wkt_ref, q_ref, k_ref, kt_ref, v_ref):
    u = _rms(h_ref[...], nw_ref[...]).astype(BF16)
    q_ref[...] = (_dot(u, w_ref[:, :D_MODEL]) * (HEAD_DIM ** -0.5)).astype(q_ref.dtype)
    k_ref[...] = _dot(u, w_ref[:, D_MODEL:D_MODEL + KV_DIM])
    kt_ref[...] = lax.dot_general(wkt_ref[...], u, _NT, preferred_element_type=F32).astype(kt_ref.dtype)
    v_ref[...] = _dot(u, w_ref[:, D_MODEL + KV_DIM:])


def _swa_in(h, nw, w, wkt, *, layer, mixer):
    t = h.shape[0]
    tm = _row_tile(t)
    row = lambda n: pl.BlockSpec((tm, n), lambda i: (i, 0))
    return pl.pallas_call(
        _swa_in_body,
        grid=(t // tm,),
        in_specs=[row(D_MODEL), _layer(nw, layer), _layer(w, mixer), _layer(wkt, mixer)],
        out_specs=[row(D_MODEL), row(KV_DIM), pl.BlockSpec((KV_DIM, tm), lambda i: (0, i)), row(KV_DIM)],
        out_shape=[jax.ShapeDtypeStruct((t, D_MODEL), BF16), jax.ShapeDtypeStruct((t, KV_DIM), F32),
                   jax.ShapeDtypeStruct((KV_DIM, t), BF16), jax.ShapeDtypeStruct((t, KV_DIM), F32)],
        compiler_params=_params(1), name="swa_in",
    )(h, nw, w, wkt)


FF_CHUNK = 1024


def _post_rows(a, h, p, wo_ref, npm_ref, npf_ref, wup_ref, wdn_ref, npff_ref, wpj_ref, pln_ref, wgate_ref):
    y = _dot(a.astype(BF16), wo_ref[...])
    h1 = h + _rms(y, npm_ref[...])
    u = _rms(h1, npf_ref[...]).astype(BF16)
    acc = jnp.zeros(h1.shape, F32)
    for c in range(D_FF // FF_CHUNK):
        hid = jnp.maximum(_dot(u, wup_ref[:, c * FF_CHUNK:(c + 1) * FF_CHUNK]), 0.0)
        acc = acc + _dot((hid * hid).astype(BF16), wdn_ref[c * FF_CHUNK:(c + 1) * FF_CHUNK, :])
    h2 = h1 + _rms(acc, npff_ref[...])
    e = _rms(_dot(p.astype(BF16), wpj_ref[...]), pln_ref[...])
    g = _sigmoid(_dot(_rms(h2).astype(BF16), wgate_ref[...]))
    return h2 + e * g


def _post_body(a_ref, h_ref, p_ref, *rest):
    w_refs, out_ref = rest[:-1], rest[-1]
    out_ref[...] = _post_rows(a_ref[...], h_ref[...], p_ref[...], *w_refs)


_POST_KEYS = ("npm", "npf", "wup", "wdn", "npff", "wpj", "pln", "wgate")


def _post_weights(wo, w, layer, mixer):
    return ([_layer(wo, mixer)] + [_layer(w[k], layer) for k in _POST_KEYS], [wo] + [w[k] for k in _POST_KEYS])


def _post(a, h, p, wo, w, *, layer, mixer):
    t = h.shape[0]
    tm = _row_tile(t)
    row = lambda n: pl.BlockSpec((tm, n), lambda i: (i, 0))
    w_specs, w_args = _post_weights(wo, w, layer, mixer)
    return pl.pallas_call(
        _post_body,
        grid=(t // tm,),
        in_specs=[row(D_MODEL), row(D_MODEL), pl.BlockSpec((None, tm, PLE_DIM), lambda i: (layer, i, 0))] + w_specs,
        out_specs=row(D_MODEL),
        out_shape=jax.ShapeDtypeStruct((t, D_MODEL), F32),
        compiler_params=_params(1), name="post",
    )(a, h, p, *w_args)


def _pipelined_specs(tm, nt, layer):
    mix_i = lambda s: jnp.minimum(s, nt - 1)
    post_i = lambda s: jnp.maximum(s - 1, 0)
    mix_row = lambda n: pl.BlockSpec((tm, n), lambda s: (mix_i(s), 0))
    mix_col = lambda n: pl.BlockSpec((n, tm), lambda s: (0, mix_i(s)))
    post_row = lambda n: pl.BlockSpec((tm, n), lambda s: (post_i(s), 0))
    p_spec = pl.BlockSpec((None, tm, PLE_DIM), lambda s: (layer, post_i(s), 0))
    return mix_i, mix_row, mix_col, post_row, p_spec


STATE_W = 2 * DV


def _scan_chunk(q_ref, kt_ref, v_ref, o_ref, gt_ref, nw_ref, hg_ref, cn_scr, m_scr, r0):
    L = CHUNK
    rows = slice(r0, r0 + L)
    ri = lax.broadcasted_iota(jnp.int32, (L, L), 0)
    ci = lax.broadcasted_iota(jnp.int32, (L, L), 1)
    tri = ci <= ri
    gt = gt_ref[:, rows]
    ig_rows, lf_rows, b_rows = gt[:HEADS], gt[HEADS:2 * HEADS], gt[2 * HEADS:]
    a_rows = ig_rows - b_rows
    m_prev = m_scr[...]
    b_last = jnp.broadcast_to(b_rows[:, L - 1:L], (HEADS, L))
    m_new = b_last + jnp.maximum(m_prev, jnp.max(a_rows, axis=1, keepdims=True))
    decay_rows = jnp.exp(b_last + m_prev - m_new)
    wk_rows = jnp.exp(b_last - b_rows + ig_rows - m_new)
    m_scr[...] = m_new

    ones_rhs = jnp.ones((L, DV), BF16)
    zeros_half = jnp.zeros((DQK, L), BF16)
    for h in range(HEADS):
        j, par = h // 2, h % 2
        q_pair = q_ref[rows, j * 128:(j + 1) * 128]
        kt_h = kt_ref[h * DQK:(h + 1) * DQK, rows]
        kt_m = jnp.concatenate([kt_h, zeros_half] if par == 0 else [zeros_half, kt_h], axis=0)
        v_ext = jnp.concatenate([v_ref[rows, h * DV:(h + 1) * DV], ones_rhs], axis=1)
        cn = cn_scr[h]
        a_row, m_row = a_rows[h:h + 1], m_prev[h:h + 1]

        big_m = jnp.maximum(jnp.max(jnp.where(tri, a_row, -jnp.inf), axis=1, keepdims=True), m_row)
        b_col = jnp.sum(jnp.where(tri, lf_rows[h:h + 1], 0.0), axis=1, keepdims=True)
        w_inter = jnp.exp(m_row - big_m)
        s = _dot(q_pair, kt_m) * jnp.exp(jnp.where(tri, a_row - big_m, NEG))
        sv = _dot(s.astype(BF16), v_ext)
        qc = _dot(q_pair, cn.astype(BF16))
        num = w_inter * qc[:, :DV] + sv[:, :DV]
        den = w_inter * qc[:, DV:] + sv[:, DV:]
        hh = num * (1.0 / jnp.maximum(jnp.abs(den), jnp.exp(-(b_col + big_m))))

        kw_t = kt_m.astype(F32) * wk_rows[h:h + 1]
        dec = decay_rows[h:h + 1]
        cn_scr[h] = jnp.concatenate([dec, dec], axis=1) * cn + _dot(kw_t.astype(BF16), v_ext)

        hn = _rms(hh, nw_ref[:, h * DV:(h + 1) * DV])
        hg_ref[rows, h * DV:(h + 1) * DV] = (hn * _sigmoid(o_ref[rows, h * DV:(h + 1) * DV])).astype(hg_ref.dtype)


def _mlstm_post_body(q_ref, kt_ref, v_ref, o_ref, gt_ref, nw_ref, h_ref, p_ref, *rest, tiles_per_seq):
    w_refs = rest[:-7]
    out_ref, c_out_ref, n_out_ref, m_out_ref, hg_scr, cn_scr, m_scr = rest[-7:]
    s = pl.program_id(0)
    nt = pl.num_programs(0) - 1
    tile = jnp.minimum(s, nt - 1)

    @pl.when(s == 0)
    def _():
        hg_scr[...] = jnp.zeros(hg_scr.shape, hg_scr.dtype)

    @pl.when(tile % tiles_per_seq == 0)
    def _():
        cn_scr[...] = jnp.zeros(cn_scr.shape, F32)
        m_scr[...] = jnp.zeros(m_scr.shape, F32)

    out_ref[...] = _post_rows(hg_scr[(s + 1) % 2], h_ref[...], p_ref[...], *w_refs)
    hg_slot = hg_scr.at[s % 2]
    for r0 in range(0, q_ref.shape[0], CHUNK):
        _scan_chunk(q_ref, kt_ref, v_ref, o_ref, gt_ref, nw_ref, hg_slot, cn_scr, m_scr, r0)

    @pl.when((tile % tiles_per_seq == tiles_per_seq - 1) & (s < nt))
    def _():
        for h in range(HEADS):
            par = h % 2
            st = cn_scr[h, par * DQK:(par + 1) * DQK, :]
            c_out_ref[0, h] = st[:, :DV]
            n_out_ref[0, h] = st[:, DV:]
        m_out_ref[0] = m_scr[...]


def _mlstm_post(q, kt, v, o, gt, nw, h, p, wo, w, *, layer, mixer, batch, seq):
    t = batch * seq
    tm = ROW_TILE
    nt = t // tm
    tiles_per_seq = seq // tm
    mix_i, mix_row, mix_col, post_row, p_spec = _pipelined_specs(tm, nt, layer)
    seq_i = lambda s: mix_i(s) // tiles_per_seq
    w_specs, w_args = _post_weights(wo, w, layer, mixer)
    return pl.pallas_call(
        functools.partial(_mlstm_post_body, tiles_per_seq=tiles_per_seq),
        grid=(nt + 1,),
        in_specs=[mix_row(HK), mix_col(HK), mix_row(HV), mix_row(D_MODEL), mix_col(3 * HEADS), _layer(nw, mixer),
                  post_row(D_MODEL), p_spec] + w_specs,
        out_specs=[post_row(D_MODEL),
                   pl.BlockSpec((1, HEADS, DQK, DV), lambda s: (seq_i(s), 0, 0, 0)),
                   pl.BlockSpec((1, HEADS, DQK, STATE_W - DV), lambda s: (seq_i(s), 0, 0, 0)),
                   pl.BlockSpec((1, HEADS, LANES), lambda s: (seq_i(s), 0, 0))],
        out_shape=[jax.ShapeDtypeStruct((t, D_MODEL), F32),
                   jax.ShapeDtypeStruct((batch, HEADS, DQK, DV), F32),
                   jax.ShapeDtypeStruct((batch, HEADS, DQK, STATE_W - DV), F32),
                   jax.ShapeDtypeStruct((batch, HEADS, LANES), F32)],
        scratch_shapes=[pltpu.VMEM((2, tm, HV), BF16), pltpu.VMEM((HEADS, 2 * DQK, STATE_W), F32),
                        pltpu.VMEM((HEADS, LANES), F32)],
        compiler_params=_params(1), name="mlstm_post",
    )(q, kt, v, o, gt, nw, h, p, *w_args)


def _mlstm_step_body(c_ref, n_ref, m_ref, q_ref, k_ref, v_ref, o_ref, g_ref, nw_ref,
                     hg_ref, c_out_ref, n_out_ref, m_out_ref):
    @pl.when(pl.program_id(1) == 0)
    def _():
        _mlstm_step_head(c_ref, n_ref, m_ref, q_ref, k_ref, v_ref, o_ref, g_ref, nw_ref,
                         hg_ref, c_out_ref, n_out_ref, m_out_ref)

    @pl.when(pl.program_id(1) > 0)
    def _():
        c_out_ref[...] = jnp.zeros(c_out_ref.shape, F32)


def _mlstm_step_head(c_ref, n_ref, m_ref, q_ref, k_ref, v_ref, o_ref, g_ref, nw_ref,
                     hg_ref, c_out_ref, n_out_ref, m_out_ref):
    h = pl.program_id(0)
    q = q_ref[0]
    k = k_ref[0]
    v = v_ref[...]
    n = n_ref[0]
    g = g_ref[...]
    lane_g = lax.broadcasted_iota(jnp.int32, g.shape, 1)
    ig = jnp.sum(jnp.where(lane_g == h, g, 0.0), axis=1, keepdims=True)
    lf = jnp.sum(jnp.where(lane_g == h + HEADS, g, 0.0), axis=1, keepdims=True)
    m_all = m_ref[...]
    lane_m = lax.broadcasted_iota(jnp.int32, m_all.shape, 1)
    m_prev = jnp.sum(jnp.where(lane_m == h, m_all, 0.0), axis=1, keepdims=True)

    inter = lf + m_prev
    m_t = jnp.maximum(inter, ig)
    w_inter = jnp.exp(inter - m_t)
    s = jnp.sum(q * k, axis=1, keepdims=True) * jnp.exp(ig - m_t)
    decay = jnp.exp(lf + m_prev - m_t)
    kw = k * jnp.exp(ig - m_t)

    qc = jnp.zeros(v.shape, F32)
    for d in range(DQK):
        c_d = c_ref[:, d, :]
        qc = qc + q[:, d:d + 1] * c_d
        c_out_ref[:, d, :] = decay * c_d + kw[:, d:d + 1] * v
    num = w_inter * qc + s * v
    den = w_inter * jnp.sum(q * n, axis=1, keepdims=True) + s
    hh = num * (1.0 / jnp.maximum(jnp.abs(den), jnp.exp(-m_t)))
    n_out_ref[0] = decay * n + kw

    @pl.when(h == 0)
    def _():
        m_out_ref[...] = jnp.zeros(m_out_ref.shape, F32)

    m_out_ref[...] = jnp.where(lane_m == h, m_t, m_out_ref[...])
    hg_ref[...] = (_rms(hh, nw_ref[...]) * _sigmoid(o_ref[...])).astype(hg_ref.dtype)


def _mlstm_step(c_all, c_new_prev, n, m, q, k, v, o, g, nw, *, mixer):
    n_slabs, b = c_all.shape[:2]
    head3 = pl.BlockSpec((1, b, DQK), lambda h, _: (h, 0, 0))
    lanes = pl.BlockSpec((b, DV), lambda h, _: (0, h))
    cspec = pl.BlockSpec((None, b, DQK, DV), lambda h, _: (mixer, 0, h, 0))
    c_out_spec = pl.BlockSpec((None, b, DQK, DV), lambda h, ps: ((mixer + ps) % n_slabs, 0, h, 0))
    whole = lambda shape: pl.BlockSpec(shape, lambda h, _: (0,) * len(shape))
    in_specs = [cspec, head3, whole((b, HEADS)), head3, head3, lanes, lanes, whole((b, 2 * HEADS)),
                pl.BlockSpec((None, 1, DV), lambda h, _: (mixer, 0, h))]
    args = [c_all, n, m, q, k, v, o, g, nw]
    body, aliases, passes = _mlstm_step_body, {}, n_slabs
    if c_new_prev is not None:
        in_specs.append(pl.BlockSpec(memory_space=pl.ANY))
        body, aliases, passes = _skip_ref(body, len(args)), {len(args): 1}, 1
        args.append(c_new_prev)
    return pl.pallas_call(
        body,
        grid=(HEADS, passes),
        in_specs=in_specs,
        out_specs=[lanes, c_out_spec, head3, whole((b, HEADS))],
        out_shape=[jax.ShapeDtypeStruct((b, HV), BF16), jax.ShapeDtypeStruct(c_all.shape, F32),
                   jax.ShapeDtypeStruct(n.shape, F32), jax.ShapeDtypeStruct((b, HEADS), F32)],
        input_output_aliases=aliases,
        compiler_params=_params(2), name="mlstm_step",
    )(*args)


def _bias_body(bucket_ref, rel_ref, out_ref):
    bucket = bucket_ref[...]
    for h in range(Q_HEADS):
        acc = jnp.zeros(bucket.shape, F32)
        for b in range(REL_BUCKETS):
            acc = jnp.where(bucket == b, rel_ref[b, h], acc)
        out_ref[h] = acc


def _bias_table(bucket, rel_bias):
    shape = bucket.shape
    return pl.pallas_call(
        _bias_body,
        in_specs=[pl.BlockSpec(memory_space=pltpu.VMEM), pl.BlockSpec(memory_space=pltpu.SMEM)],
        out_specs=pl.BlockSpec(memory_space=pltpu.VMEM),
        out_shape=jax.ShapeDtypeStruct((Q_HEADS,) + shape, F32),
        name="rel_bias_table",
    )(bucket, rel_bias)


def _rel_bucket(dist):
    max_exact = REL_BUCKETS // 2
    d = jnp.maximum(dist, 0)
    df = jnp.maximum(d, 1).astype(F32)
    large = max_exact + (jnp.log(df / max_exact) / math.log(REL_MAX_DIST / max_exact)
                         * (REL_BUCKETS - max_exact)).astype(jnp.int32)
    large = jnp.minimum(large, REL_BUCKETS - 1)
    return jnp.where(d < max_exact, d, large)


def _swa_block(q, kt_prev, kt_cur, v_prev, v_cur, bias_ref, sink_ref, in_range, out_ref, rows):
    blk = WINDOW
    qi = lax.broadcasted_iota(jnp.int32, (blk, blk), 0)
    kj = lax.broadcasted_iota(jnp.int32, (blk, blk), 1)
    from_prev = kj > qi
    lane = lax.broadcasted_iota(jnp.int32, (2 * blk, LANES), 1)
    lane_q = lax.broadcasted_iota(jnp.int32, (blk, LANES), 1)
    zero_k = jnp.zeros((HEAD_DIM, 2 * blk), BF16)
    ones_bd = jnp.concatenate([(lane < HEAD_DIM).astype(BF16), (lane >= HEAD_DIM).astype(BF16)], axis=0)

    for g in range(KV_HEADS):
        hd_rows = slice(g * HEAD_DIM, (g + 1) * HEAD_DIM)
        kt_g = jnp.concatenate([kt_prev[hd_rows, :], kt_cur[hd_rows, :]], axis=1)
        rhs = jnp.concatenate([jnp.concatenate([kt_g, zero_k], axis=0),
                               jnp.concatenate([zero_k, kt_g], axis=0)], axis=1)
        v_lanes = slice((g // 2) * LANES, (g // 2 + 1) * LANES)
        vt = jnp.concatenate([v_prev[:, v_lanes], v_cur[:, v_lanes]], axis=0)
        vr = pltpu.roll(vt, HEAD_DIM, axis=1)
        v_lo = jnp.where(lane < HEAD_DIM, vt if g % 2 == 0 else vr, 0.0)
        v_hi = jnp.where(lane >= HEAD_DIM, vr if g % 2 == 0 else vt, 0.0)
        v_bd = jnp.concatenate([jnp.concatenate([v_lo, v_hi], axis=0).astype(BF16), ones_bd], axis=1)
        for pair in range(GROUP // 2):
            jp = g * (GROUP // 2) + pair
            logits = _dot(q[:, jp * LANES:(jp + 1) * LANES], rhs)
            ps, sk = [], []
            for i in range(2):
                hd = 2 * jp + i
                lg = jnp.where(from_prev, logits[:, 2 * i * blk:(2 * i + 1) * blk],
                               logits[:, (2 * i + 1) * blk:(2 * i + 2) * blk]) + bias_ref[hd]
                if in_range is not None:
                    lg = jnp.where(in_range, lg, NEG)
                sink = sink_ref[hd]
                mx = jnp.maximum(jnp.max(lg, axis=1, keepdims=True), sink)
                p = jnp.exp(lg - mx)
                ps += [jnp.where(from_prev, p, 0.0).astype(BF16), jnp.where(from_prev, 0.0, p).astype(BF16)]
                sk.append(jnp.exp(sink - mx))
            pv = _dot(jnp.concatenate(ps, axis=1), v_bd)
            den = pv[:, LANES:] + jnp.where(lane_q < HEAD_DIM, sk[0], sk[1])
            out_ref[rows, jp * LANES:(jp + 1) * LANES] = (pv[:, :LANES] * (1.0 / den)).astype(out_ref.dtype)


def _swa_post_body(q_ref, kt_ref, ktp_ref, v_ref, vp_ref, bias_ref, sink_ref, h_ref, p_ref, *rest, tiles_per_seq):
    w_refs, out_ref, attn_scr = rest[:-2], rest[-2], rest[-1]
    s = pl.program_id(0)
    nt = pl.num_programs(0) - 1
    tile = jnp.minimum(s, nt - 1)

    @pl.when(s == 0)
    def _():
        attn_scr[...] = jnp.zeros(attn_scr.shape, attn_scr.dtype)

    out_ref[...] = _post_rows(attn_scr[(s + 1) % 2], h_ref[...], p_ref[...], *w_refs)
    attn_slot = attn_scr.at[s % 2]
    blk = WINDOW
    qi = lax.broadcasted_iota(jnp.int32, (blk, blk), 0)
    kj = lax.broadcasted_iota(jnp.int32, (blk, blk), 1)
    has_prev = (kj <= qi) | (tile % tiles_per_seq != 0)
    for b in range(q_ref.shape[0] // blk):
        rows, prev_rows = slice(b * blk, (b + 1) * blk), slice((b - 1) * blk, b * blk)
        _swa_block(q_ref[rows, :],
                   ktp_ref[...] if b == 0 else kt_ref[:, prev_rows], kt_ref[:, rows],
                   vp_ref[...] if b == 0 else v_ref[prev_rows, :], v_ref[rows, :],
                   bias_ref, sink_ref, has_prev if b == 0 else None, attn_slot, rows)


def _swa_post(q, kt, v, bias, sinks, h, p, wo, w, *, layer, mixer, batch, seq):
    t = batch * seq
    tm = ROW_TILE
    nt = t // tm
    blocks = tm // WINDOW
    mix_i, mix_row, mix_col, post_row, p_spec = _pipelined_specs(tm, nt, layer)
    prev_blk = lambda s: jnp.maximum(mix_i(s) * blocks - 1, 0)
    w_specs, w_args = _post_weights(wo, w, layer, mixer)
    return pl.pallas_call(
        functools.partial(_swa_post_body, tiles_per_seq=seq // tm),
        grid=(nt + 1,),
        in_specs=[mix_row(D_MODEL), mix_col(KV_DIM), pl.BlockSpec((KV_DIM, WINDOW), lambda s: (0, prev_blk(s))),
                  mix_row(KV_DIM), pl.BlockSpec((WINDOW, KV_DIM), lambda s: (prev_blk(s), 0)),
                  _resident(bias.shape), pl.BlockSpec(memory_space=pltpu.SMEM), post_row(D_MODEL), p_spec] + w_specs,
        out_specs=post_row(D_MODEL),
        out_shape=jax.ShapeDtypeStruct((t, D_MODEL), F32),
        scratch_shapes=[pltpu.VMEM((2, tm, D_MODEL), BF16)],
        compiler_params=_params(1), name="swa_post",
    )(q, kt, kt, v, v, bias, sinks[mixer], h, p, *w_args)


SAMPLE_BB = 8


def _swa_sample_body(q_ref, kc_ref, vc_ref, kn_ref, vn_ref, bias_ref, sink_ref, out_ref, ko_ref, vo_ref):
    w = WINDOW
    row = lax.broadcasted_iota(jnp.int32, (w, KV_DIM), 0)
    rg = lax.broadcasted_iota(jnp.int32, (Q_HEADS, KV_DIM), 0) // GROUP
    lg = lax.broadcasted_iota(jnp.int32, (Q_HEADS, KV_DIM), 1) // HEAD_DIM
    rsel = lax.broadcasted_iota(jnp.int32, (Q_HEADS, HEAD_DIM), 0) // GROUP
    bias = bias_ref[...]
    sink = sink_ref[...]
    for i in range(SAMPLE_BB):
        k_win = jnp.where(row == w - 1, kn_ref[i], pltpu.roll(kc_ref[i], w - 1, axis=0))
        v_win = jnp.where(row == w - 1, vn_ref[i], pltpu.roll(vc_ref[i], w - 1, axis=0))
        ko_ref[i] = k_win
        vo_ref[i] = v_win
        qb = q_ref[i].astype(F32)
        q_bd = jnp.where(rg == lg, jnp.concatenate([qb] * KV_HEADS, axis=1), 0.0).astype(BF16)
        logits = lax.dot_general(q_bd, k_win.astype(BF16), _NT, preferred_element_type=F32) + bias
        mx = jnp.maximum(jnp.max(logits, axis=1, keepdims=True), sink)
        p = jnp.exp(logits - mx)
        inv = 1.0 / (jnp.sum(p, axis=1, keepdims=True) + jnp.exp(sink - mx))
        o = _dot((p * inv).astype(BF16), v_win.astype(BF16))
        sel = o[:, 0:HEAD_DIM]
        for g in range(1, KV_HEADS):
            sel = jnp.where(rsel == g, o[:, g * HEAD_DIM:(g + 1) * HEAD_DIM], sel)
        out_ref[i] = sel.astype(out_ref.dtype)


def _swa_sample(q, kc_all, vc_all, kn, vn, bias, sinks, *, mixer):
    b = q.shape[0]
    bb = SAMPLE_BB
    blk = lambda shape: pl.BlockSpec((bb,) + shape, lambda i: (i, 0, 0))
    cache = pl.BlockSpec((None, bb, WINDOW, KV_DIM), lambda i: (mixer, i, 0, 0))
    return pl.pallas_call(
        _swa_sample_body,
        grid=(b // bb,),
        in_specs=[blk((Q_HEADS, HEAD_DIM)), cache, cache, blk((1, KV_DIM)), blk((1, KV_DIM)), _resident(bias.shape),
                  _layer(sinks, mixer)],
        out_specs=[blk((Q_HEADS, HEAD_DIM)), cache, cache],
        out_shape=[jax.ShapeDtypeStruct((b, Q_HEADS, HEAD_DIM), BF16), jax.ShapeDtypeStruct(kc_all.shape, F32),
                   jax.ShapeDtypeStruct(vc_all.shape, F32)],
        input_output_aliases={1: 1, 2: 2},
        compiler_params=_params(1), name="swa_sample",
    )(q, kc_all, vc_all, kn, vn, bias, sinks)


def kernel(x_prompt, x_sample, state_mlstm_C, state_mlstm_n, state_mlstm_m, cache_swa_k, cache_swa_v, p_prompt, p_sample, rel_bias, norm_pre_mix, norm_post_mix, norm_pre_ffn, norm_post_ffn, mlstm_w_in, mlstm_b_i, mlstm_b_f, mlstm_norm, mlstm_w_out, swa_w_qkv, swa_sinks, swa_w_o, ffn_w_up, ffn_w_down, ple_w_proj, ple_norm, ple_w_gate):
    bp, sp, _ = x_prompt.shape
    bs = x_sample.shape[0]
    win = cache_swa_k.shape[2]
    assert win == WINDOW and x_sample.shape[1] == 1
    bf = lambda a: a.astype(BF16)
    vecs = lambda a: a.astype(F32)[:, None, :]
    t_last = lambda a: jnp.swapaxes(a, 1, 2)

    post_w = dict(npm=vecs(norm_post_mix), npf=vecs(norm_pre_ffn), wup=bf(ffn_w_up), wdn=bf(ffn_w_down),
                  npff=vecs(norm_post_ffn), wpj=bf(ple_w_proj), pln=vecs(ple_norm), wgate=bf(ple_w_gate))
    n_pre = vecs(norm_pre_mix)
    m_w = bf(mlstm_w_in)
    m_wkgt = bf(t_last(jnp.concatenate([mlstm_w_in[:, :, K0:V0], mlstm_w_in[:, :, G0:]], axis=2)))
    m_bias = jnp.concatenate([mlstm_b_i, mlstm_b_f], axis=1).astype(F32)
    m_wout, m_norm = bf(mlstm_w_out), vecs(mlstm_norm)
    s_w, s_wkt, s_wo = bf(swa_w_qkv), bf(t_last(swa_w_qkv[:, :, D_MODEL:D_MODEL + KV_DIM])), bf(swa_w_o)
    sinks = swa_sinks.astype(F32)

    qi = jnp.arange(WINDOW)[:, None]
    kj = jnp.arange(WINDOW)[None, :]
    dist_p = jnp.where(kj > qi, qi + WINDOW - kj, qi - kj)
    bias_p = _bias_table(_rel_bucket(dist_p).astype(jnp.int32), rel_bias.astype(F32))
    dist_s = (win - 1) - jnp.arange(win)
    bias_s = _bias_table(jnp.broadcast_to(_rel_bucket(dist_s)[None, :], (8, win)).astype(jnp.int32),
                         rel_bias.astype(F32))[:, 0, :]

    hp = x_prompt.reshape(bp * sp, D_MODEL)
    hs = x_sample.reshape(bs, D_MODEL)
    pp = p_prompt.reshape(DEPTH, bp * sp, PLE_DIM)
    ps = p_sample.reshape(DEPTH, bs, PLE_DIM)
    c_old = state_mlstm_C.astype(F32).reshape(-1, bs, HK, DV)
    k_win = cache_swa_k.astype(F32).reshape(-1, bs, win, KV_DIM)
    v_win = cache_swa_v.astype(F32).reshape(-1, bs, win, KV_DIM)
    c_new = None
    outs = dict(Cp=[], np=[], mp=[], kp=[], vp=[], ns=[], ms=[])

    for i in range(DEPTH):
        j = i // N_MIXERS
        if i % N_MIXERS == 0:
            wo = m_wout
            q, kt, v, o, gt = _mlstm_in(hp, n_pre, m_w, m_wkgt, m_bias[j].reshape(-1, 1),
                                        layer=i, mixer=j, prompt=True, act_dtype=BF16)
            hp, c_fin, n_fin, m_fin = _mlstm_post(q, kt, v, o, gt, m_norm, hp, pp, wo, post_w,
                                                  layer=i, mixer=j, batch=bp, seq=sp)
            outs["Cp"].append(c_fin)
            outs["np"].append(n_fin[..., 0])
            outs["mp"].append(m_fin[..., 0])
            q, k, v, o, g = _mlstm_in(hs, n_pre, m_w, m_wkgt, m_bias[j].reshape(1, -1),
                                      layer=i, mixer=j, prompt=False, act_dtype=F32)
            heads_first = lambda a: a.reshape(bs, HEADS, DQK).transpose(1, 0, 2)
            a_s, c_new, n_new, m_new = _mlstm_step(
                c_old, c_new, state_mlstm_n[j].astype(F32).transpose(1, 0, 2), state_mlstm_m[j].astype(F32),
                heads_first(q), heads_first(k), v, o, g, m_norm, mixer=j)
            outs["ns"].append(n_new.transpose(1, 0, 2))
            outs["ms"].append(m_new)
        else:
            wo = s_wo
            q, k, kt, v = _swa_in(hp, n_pre, s_w, s_wkt, layer=i, mixer=j)
            hp = _swa_post(q, kt, v, bias_p, sinks, hp, pp, wo, post_w, layer=i, mixer=j, batch=bp, seq=sp)
            last = lambda a: a.reshape(bp, sp, KV_DIM)[:, sp - win:].reshape(bp, win, KV_HEADS, HEAD_DIM)
            outs["kp"].append(last(k))
            outs["vp"].append(last(v))
            q, k, _, v = _swa_in(hs, n_pre, s_w, s_wkt, layer=i, mixer=j)
            a_s, k_win, v_win = _swa_sample(
                q.reshape(bs, Q_HEADS, HEAD_DIM), k_win, v_win, k.reshape(bs, 1, KV_DIM), v.reshape(bs, 1, KV_DIM),
                bias_s, sinks[:, :, None], mixer=j)
            a_s = a_s.reshape(bs, D_MODEL)
        hs = _post(a_s, hs, ps, wo, post_w, layer=i, mixer=j)

    st = lambda key, like: jnp.stack(outs[key]).astype(like.dtype)
    kv_out = lambda a, like: a.reshape(-1, bs, win, KV_HEADS, HEAD_DIM).astype(like.dtype)
    return (hp.reshape(x_prompt.shape), hs.reshape(x_sample.shape),
            st("Cp", state_mlstm_C), st("np", state_mlstm_n), st("mp", state_mlstm_m),
            st("kp", cache_swa_k), st("vp", cache_swa_v),
            c_new.reshape(state_mlstm_C.shape).astype(state_mlstm_C.dtype), st("ns", state_mlstm_n),
            st("ms", state_mlstm_m), kv_out(k_win, cache_swa_k), kv_out(v_win, cache_swa_v))
```

```python
import functools
import math

import jax
import jax.numpy as jnp
from jax import lax
from jax.experimental import pallas as pl
from jax.experimental.pallas import tpu as pltpu

F32 = jnp.float32
BF16 = jnp.bfloat16

D_MODEL = 1024
DEPTH = 4
N_MIXERS = 2
HEADS = 8
DQK = 64
DV = 128
CHUNK = 128
Q_HEADS = 16
KV_HEADS = 4
GROUP = Q_HEADS // KV_HEADS
HEAD_DIM = 64
KV_DIM = KV_HEADS * HEAD_DIM
WINDOW = 128
REL_BUCKETS = 32
REL_MAX_DIST = 128
PLE_DIM = 256
D_FF = 4 * D_MODEL
EPS = 1e-6
NEG = -1e30

HK, HV = HEADS * DQK, HEADS * DV
Q0, K0, V0, O0, G0 = 0, HK, 2 * HK, 2 * HK + HV, 2 * HK + HV + D_MODEL
MLSTM_IN = G0 + 2 * HEADS

V7X_VMEM_BYTES = 64 * 1024 * 1024
VMEM_LIMIT = V7X_VMEM_BYTES - 8 * 1024 * 1024
LANES = 128
ROW_TILE = 512

_NT = (((1,), (1,)), ((), ()))


def _row_tile(t):
    return ROW_TILE if t % ROW_TILE == 0 else t


def _params(n_grid):
    return pltpu.CompilerParams(dimension_semantics=("arbitrary",) * n_grid, vmem_limit_bytes=VMEM_LIMIT)


def _resident(shape):
    nd = len(shape)
    return pl.BlockSpec(shape, lambda *_: (0,) * nd, pipeline_mode=pl.Buffered(1))


def _layer(arr, layer):
    tail = arr.shape[1:]
    return pl.BlockSpec((None,) + tail, lambda *_: (layer,) + (0,) * len(tail), pipeline_mode=pl.Buffered(1))


def _skip_ref(body, pos):
    def wrapped(*refs):
        return body(*refs[:pos], *refs[pos + 1:])
    return wrapped


def _rms(x, w=None):
    y = x * lax.rsqrt(jnp.mean(x * x, axis=-1, keepdims=True) + EPS)
    return y if w is None else y * w


def _log_sigmoid(x):
    return -(jnp.maximum(-x, 0.0) + jnp.log1p(jnp.exp(-jnp.abs(x))))


def _sigmoid(x):
    return 1.0 / (1.0 + jnp.exp(-x))


def _dot(a, b):
    return jnp.dot(a, b, preferred_element_type=F32)


def _mlstm_in_body(h_ref, nw_ref, w_ref, bg_ref, q_ref, k_ref, v_ref, o_ref, g_ref, *scratch, prompt):
    if prompt:
        wkgt_scr, = scratch

        @pl.when(pl.program_id(0) == 0)
        def _():
            wkgt_scr[0:HK, :] = w_ref[:, K0:V0].astype(F32).T.astype(BF16)
            wg = jnp.concatenate([w_ref[:, G0:MLSTM_IN].astype(F32),
                                  jnp.zeros((D_MODEL, LANES - 2 * HEADS), F32)], axis=1)
            wkgt_scr[HK:, :] = wg.T.astype(BF16)

    u = _rms(h_ref[...], nw_ref[...]).astype(BF16)
    q_ref[...] = (_dot(u, w_ref[:, Q0:K0]) * (DQK ** -0.5)).astype(q_ref.dtype)
    v_ref[...] = _dot(u, w_ref[:, V0:O0]).astype(v_ref.dtype)
    o_ref[...] = _dot(u, w_ref[:, O0:G0])
    if prompt:
        kgt = lax.dot_general(wkgt_scr[...], u, _NT, preferred_element_type=F32)
        k_ref[...] = kgt[:HK].astype(k_ref.dtype)
        gt = kgt[HK:HK + 2 * HEADS] + bg_ref[...]
        ig, lf = gt[:HEADS], _log_sigmoid(gt[HEADS:])
        tm = lf.shape[1]
        r = lax.broadcasted_iota(jnp.int32, (tm, tm), 0)
        t = lax.broadcasted_iota(jnp.int32, (tm, tm), 1)
        same_chunk_upper = ((r <= t) & (r // CHUNK == t // CHUNK)).astype(F32)
        b = jnp.dot(lf, same_chunk_upper, precision=lax.Precision.HIGHEST, preferred_element_type=F32)
        g_ref[...] = jnp.concatenate([ig, lf, b], axis=0)
    else:
        k_ref[...] = _dot(u, w_ref[:, K0:V0]).astype(k_ref.dtype)
        g = _dot(u, w_ref[:, G0:MLSTM_IN]) + bg_ref[...]
        lane = lax.broadcasted_iota(jnp.int32, g.shape, 1)
        g_ref[...] = jnp.where(lane < HEADS, g, _log_sigmoid(g))


def _mlstm_in(h, nw, w, bg, *, layer, mixer, prompt, act_dtype):
    t = h.shape[0]
    tm = _row_tile(t)
    row = lambda n: pl.BlockSpec((tm, n), lambda i: (i, 0))
    col = lambda n: pl.BlockSpec((n, tm), lambda i: (0, i))
    kshape, kspec = ((HK, t), col(HK)) if prompt else ((t, HK), row(HK))
    gshape, gspec = ((3 * HEADS, t), col(3 * HEADS)) if prompt else ((t, 2 * HEADS), row(2 * HEADS))
    return pl.pallas_call(
        functools.partial(_mlstm_in_body, prompt=prompt),
        grid=(t // tm,),
        in_specs=[row(D_MODEL), _layer(nw, layer), _layer(w, mixer), _resident(bg.shape)],
        out_specs=[row(HK), kspec, row(HV), row(D_MODEL), gspec],
        out_shape=[jax.ShapeDtypeStruct((t, HK), act_dtype), jax.ShapeDtypeStruct(kshape, act_dtype),
                   jax.ShapeDtypeStruct((t, HV), act_dtype), jax.ShapeDtypeStruct((t, D_MODEL), F32),
                   jax.ShapeDtypeStruct(gshape, F32)],
        scratch_shapes=[pltpu.VMEM((HK + LANES, D_MODEL), BF16)] if prompt else [],
        compiler_params=_params(1), name="mlstm_in",
    )(h, nw, w, bg)


def _swa_in_body(h_ref, nw_ref, w_ref, q_ref, k_ref, kt_ref, v_ref):
    u = _rms(h_ref[...], nw_ref[...]).astype(BF16)
    q_ref[...] = (_dot(u, w_ref[:, :D_MODEL]) * (HEAD_DIM ** -0.5)).astype(q_ref.dtype)
    k = _dot(u, w_ref[:, D_MODEL:D_MODEL + KV_DIM])
    k_ref[...] = k
    kt_ref[...] = k.T.astype(kt_ref.dtype)
    v_ref[...] = _dot(u, w_ref[:, D_MODEL + KV_DIM:])


def _swa_in(h, nw, w, *, layer, mixer):
    t = h.shape[0]
    tm = _row_tile(t)
    row = lambda n: pl.BlockSpec((tm, n), lambda i: (i, 0))
    return pl.pallas_call(
        _swa_in_body,
        grid=(t // tm,),
        in_specs=[row(D_MODEL), _layer(nw, layer), _layer(w, mixer)],
        out_specs=[row(D_MODEL), row(KV_DIM), pl.BlockSpec((KV_DIM, tm), lambda i: (0, i)), row(KV_DIM)],
        out_shape=[jax.ShapeDtypeStruct((t, D_MODEL), BF16), jax.ShapeDtypeStruct((t, KV_DIM), F32),
                   jax.ShapeDtypeStruct((KV_DIM, t), BF16), jax.ShapeDtypeStruct((t, KV_DIM), F32)],
        compiler_params=_params(1), name="swa_in",
    )(h, nw, w)


FF_CHUNK = 1024


def _post_rows(a, h, p, wo_ref, npm_ref, npf_ref, wup_ref, wdn_ref, npff_ref, wpj_ref, pln_ref, wgate_ref):
    y = _dot(a.astype(BF16), wo_ref[...])
    h1 = h + _rms(y, npm_ref[...])
    u = _rms(h1, npf_ref[...]).astype(BF16)
    acc = jnp.zeros(h1.shape, F32)
    for c in range(D_FF // FF_CHUNK):
        hid = jnp.maximum(_dot(u, wup_ref[:, c * FF_CHUNK:(c + 1) * FF_CHUNK]), 0.0)
        acc = acc + _dot((hid * hid).astype(BF16), wdn_ref[c * FF_CHUNK:(c + 1) * FF_CHUNK, :])
    h2 = h1 + _rms(acc, npff_ref[...])
    e = _rms(_dot(p.astype(BF16), wpj_ref[...]), pln_ref[...])
    g = _sigmoid(_dot(_rms(h2).astype(BF16), wgate_ref[...]))
    return h2 + e * g


def _post_body(a_ref, h_ref, p_ref, *rest):
    w_refs, out_ref = rest[:-1], rest[-1]
    out_ref[...] = _post_rows(a_ref[...], h_ref[...], p_ref[...], *w_refs)


_POST_KEYS = ("npm", "npf", "wup", "wdn", "npff", "wpj", "pln", "wgate")


def _post_weights(wo, w, layer, mixer):
    return ([_layer(wo, mixer)] + [_layer(w[k], layer) for k in _POST_KEYS], [wo] + [w[k] for k in _POST_KEYS])


def _post(a, h, p, wo, w, *, layer, mixer):
    t = h.shape[0]
    tm = _row_tile(t)
    row = lambda n: pl.BlockSpec((tm, n), lambda i: (i, 0))
    w_specs, w_args = _post_weights(wo, w, layer, mixer)
    return pl.pallas_call(
        _post_body,
        grid=(t // tm,),
        in_specs=[row(D_MODEL), row(D_MODEL), pl.BlockSpec((None, tm, PLE_DIM), lambda i: (layer, i, 0))] + w_specs,
        out_specs=row(D_MODEL),
        out_shape=jax.ShapeDtypeStruct((t, D_MODEL), F32),
        compiler_params=_params(1), name="post",
    )(a, h, p, *w_args)


def _pipelined_specs(tm, nt, layer):
    mix_i = lambda s: jnp.minimum(s, nt - 1)
    post_i = lambda s: jnp.maximum(s - 1, 0)
    mix_row = lambda n: pl.BlockSpec((tm, n), lambda s: (mix_i(s), 0))
    mix_col = lambda n: pl.BlockSpec((n, tm), lambda s: (0, mix_i(s)))
    post_row = lambda n: pl.BlockSpec((tm, n), lambda s: (post_i(s), 0))
    p_spec = pl.BlockSpec((None, tm, PLE_DIM), lambda s: (layer, post_i(s), 0))
    return mix_i, mix_row, mix_col, post_row, p_spec


STATE_W = 2 * DV


def _scan_chunk(q_ref, kt_ref, v_ref, o_ref, gt_ref, nw_ref, hg_ref, cn_scr, m_scr, r0):
    L = CHUNK
    rows = slice(r0, r0 + L)
    ri = lax.broadcasted_iota(jnp.int32, (L, L), 0)
    ci = lax.broadcasted_iota(jnp.int32, (L, L), 1)
    tri = ci <= ri
    gt = gt_ref[:, rows]
    ig_rows, lf_rows, b_rows = gt[:HEADS], gt[HEADS:2 * HEADS], gt[2 * HEADS:]
    a_rows = ig_rows - b_rows
    m_prev = m_scr[...]
    b_last = jnp.broadcast_to(b_rows[:, L - 1:L], (HEADS, L))
    m_new = b_last + jnp.maximum(m_prev, jnp.max(a_rows, axis=1, keepdims=True))
    decay_rows = jnp.exp(b_last + m_prev - m_new)
    wk_rows = jnp.exp(b_last - b_rows + ig_rows - m_new)
    m_scr[...] = m_new

    ones_rhs = jnp.ones((L, DV), BF16)
    zeros_half = jnp.zeros((DQK, L), BF16)
    for h in range(HEADS):
        j, par = h // 2, h % 2
        q_pair = q_ref[rows, j * 128:(j + 1) * 128]
        kt_h = kt_ref[h * DQK:(h + 1) * DQK, rows]
        kt_m = jnp.concatenate([kt_h, zeros_half] if par == 0 else [zeros_half, kt_h], axis=0)
        v_ext = jnp.concatenate([v_ref[rows, h * DV:(h + 1) * DV], ones_rhs], axis=1)
        cn = cn_scr[h]
        a_row, m_row = a_rows[h:h + 1], m_prev[h:h + 1]

        big_m = jnp.maximum(jnp.max(jnp.where(tri, a_row, -jnp.inf), axis=1, keepdims=True), m_row)
        b_col = jnp.sum(jnp.where(tri, lf_rows[h:h + 1], 0.0), axis=1, keepdims=True)
        w_inter = jnp.exp(m_row - big_m)
        s = _dot(q_pair, kt_m) * jnp.exp(jnp.where(tri, a_row - big_m, NEG))
        sv = _dot(s.astype(BF16), v_ext)
        qc = _dot(q_pair, cn.astype(BF16))
        num = w_inter * qc[:, :DV] + sv[:, :DV]
        den = w_inter * qc[:, DV:] + sv[:, DV:]
        hh = num * (1.0 / jnp.maximum(jnp.abs(den), jnp.exp(-(b_col + big_m))))

        kw_t = kt_m.astype(F32) * wk_rows[h:h + 1]
        dec = decay_rows[h:h + 1]
        cn_scr[h] = jnp.concatenate([dec, dec], axis=1) * cn + _dot(kw_t.astype(BF16), v_ext)

        hn = _rms(hh, nw_ref[:, h * DV:(h + 1) * DV])
        hg_ref[rows, h * DV:(h + 1) * DV] = (hn * _sigmoid(o_ref[rows, h * DV:(h + 1) * DV])).astype(hg_ref.dtype)


def _mlstm_post_body(q_ref, kt_ref, v_ref, o_ref, gt_ref, nw_ref, h_ref, p_ref, *rest, tiles_per_seq):
    w_refs = rest[:-7]
    out_ref, c_out_ref, n_out_ref, m_out_ref, hg_scr, cn_scr, m_scr = rest[-7:]
    s = pl.program_id(0)
    nt = pl.num_programs(0) - 1
    tile = jnp.minimum(s, nt - 1)

    @pl.when(s == 0)
    def _():
        hg_scr[...] = jnp.zeros(hg_scr.shape, hg_scr.dtype)

    @pl.when(tile % tiles_per_seq == 0)
    def _():
        cn_scr[...] = jnp.zeros(cn_scr.shape, F32)
        m_scr[...] = jnp.zeros(m_scr.shape, F32)

    out_ref[...] = _post_rows(hg_scr[(s + 1) % 2], h_ref[...], p_ref[...], *w_refs)
    hg_slot = hg_scr.at[s % 2]
    for r0 in range(0, q_ref.shape[0], CHUNK):
        _scan_chunk(q_ref, kt_ref, v_ref, o_ref, gt_ref, nw_ref, hg_slot, cn_scr, m_scr, r0)

    @pl.when((tile % tiles_per_seq == tiles_per_seq - 1) & (s < nt))
    def _():
        for h in range(HEADS):
            par = h % 2
            st = cn_scr[h, par * DQK:(par + 1) * DQK, :]
            c_out_ref[0, h] = st[:, :DV]
            n_out_ref[0, h] = st[:, DV:]
        m_out_ref[0] = m_scr[...]


def _mlstm_post(q, kt, v, o, gt, nw, h, p, wo, w, *, layer, mixer, batch, seq):
    t = batch * seq
    tm = ROW_TILE
    nt = t // tm
    tiles_per_seq = seq // tm
    mix_i, mix_row, mix_col, post_row, p_spec = _pipelined_specs(tm, nt, layer)
    seq_i = lambda s: mix_i(s) // tiles_per_seq
    w_specs, w_args = _post_weights(wo, w, layer, mixer)
    return pl.pallas_call(
        functools.partial(_mlstm_post_body, tiles_per_seq=tiles_per_seq),
        grid=(nt + 1,),
        in_specs=[mix_row(HK), mix_col(HK), mix_row(HV), mix_row(D_MODEL), mix_col(3 * HEADS), _layer(nw, mixer),
                  post_row(D_MODEL), p_spec] + w_specs,
        out_specs=[post_row(D_MODEL),
                   pl.BlockSpec((1, HEADS, DQK, DV), lambda s: (seq_i(s), 0, 0, 0)),
                   pl.BlockSpec((1, HEADS, DQK, STATE_W - DV), lambda s: (seq_i(s), 0, 0, 0)),
                   pl.BlockSpec((1, HEADS, LANES), lambda s: (seq_i(s), 0, 0))],
        out_shape=[jax.ShapeDtypeStruct((t, D_MODEL), F32),
                   jax.ShapeDtypeStruct((batch, HEADS, DQK, DV), F32),
                   jax.ShapeDtypeStruct((batch, HEADS, DQK, STATE_W - DV), F32),
                   jax.ShapeDtypeStruct((batch, HEADS, LANES), F32)],
        scratch_shapes=[pltpu.VMEM((2, tm, HV), BF16), pltpu.VMEM((HEADS, 2 * DQK, STATE_W), F32),
                        pltpu.VMEM((HEADS, LANES), F32)],
        compiler_params=_params(1), name="mlstm_post",
    )(q, kt, v, o, gt, nw, h, p, *w_args)


STEP_BB = 8
_TN = (((0,), (0,)), ((), ()))


def _mlstm_step_body(c_ref, n_ref, m_ref, q_ref, k_ref, v_ref, o_ref, g_ref, nw_ref,
                     hg_ref, c_out_ref, n_out_ref, m_out_ref):
    @pl.when(pl.program_id(1) == 0)
    def _():
        _mlstm_step_rows(c_ref, n_ref, m_ref, q_ref, k_ref, v_ref, o_ref, g_ref, nw_ref,
                         hg_ref, c_out_ref, n_out_ref, m_out_ref)

    @pl.when(pl.program_id(1) > 0)
    def _():
        c_out_ref[...] = jnp.zeros(c_out_ref.shape, F32)


def _mlstm_step_rows(c_ref, n_ref, m_ref, q_ref, k_ref, v_ref, o_ref, g_ref, nw_ref,
                     hg_ref, c_out_ref, n_out_ref, m_out_ref):
    bb = STEP_BB
    g = g_ref[...]
    ig, lf, m_prev = g[:, :HEADS], g[:, HEADS:], m_ref[...]
    inter = lf + m_prev
    m_t = jnp.maximum(inter, ig)
    w_inter = jnp.exp(inter - m_t)
    in_w = jnp.exp(ig - m_t)
    decay = jnp.exp(lf + m_prev - m_t)
    floor = jnp.exp(-m_t)
    m_out_ref[...] = m_t
    row_k = lax.broadcasted_iota(jnp.int32, (bb, DQK), 0)
    row_v = lax.broadcasted_iota(jnp.int32, (bb, DV), 0)

    for h in range(HEADS):
        col = lambda a: a[:, h:h + 1]
        q = q_ref[:, h * DQK:(h + 1) * DQK]
        k = k_ref[:, h * DQK:(h + 1) * DQK]
        v = v_ref[:, h * DV:(h + 1) * DV]
        n = n_ref[:, h, :]
        s = jnp.sum(q * k, axis=1, keepdims=True) * col(in_w)
        kw = k * col(in_w)
        q16, v16 = q.astype(BF16), v.astype(BF16)
        qc = jnp.zeros((bb, DV), F32)
        for i in range(bb):
            c_i = c_ref[i, h * DQK:(h + 1) * DQK, :]
            qc = jnp.where(row_v == i, _dot(q16, c_i.astype(BF16)), qc)
            outer = lax.dot_general(jnp.where(row_k == i, kw, 0.0).astype(BF16), v16, _TN,
                                    preferred_element_type=F32)
            c_out_ref[i, h * DQK:(h + 1) * DQK, :] = decay[i:i + 1, h:h + 1] * c_i + outer
        num = col(w_inter) * qc + s * v
        den = col(w_inter) * jnp.sum(q * n, axis=1, keepdims=True) + s
        hh = num * (1.0 / jnp.maximum(jnp.abs(den), col(floor)))
        n_out_ref[:, h, :] = col(decay) * n + kw
        hg_ref[:, h * DV:(h + 1) * DV] = (_rms(hh, nw_ref[:, h * DV:(h + 1) * DV])
                                          * _sigmoid(o_ref[:, h * DV:(h + 1) * DV]))


def _mlstm_step(c_all, c_new_prev, n, m, q, k, v, o, g, nw, *, mixer):
    n_slabs, b = c_all.shape[:2]
    bb = STEP_BB
    rows = lambda width: pl.BlockSpec((bb, width), lambda i, _: (i, 0))
    cspec = pl.BlockSpec((None, bb, HK, DV), lambda i, _: (mixer, i, 0, 0))
    c_out_spec = pl.BlockSpec((None, bb, HK, DV), lambda i, ps: ((mixer + ps) % n_slabs, i, 0, 0))
    nspec = pl.BlockSpec((bb, HEADS, DQK), lambda i, _: (i, 0, 0))
    in_specs = [cspec, nspec, rows(HEADS), rows(HK), rows(HK), rows(HV), rows(D_MODEL), rows(2 * HEADS),
                pl.BlockSpec((None, 1, HV), lambda i, _: (mixer, 0, 0))]
    args = [c_all, n, m, q, k, v, o, g, nw]
    body, aliases, passes = _mlstm_step_body, {}, n_slabs
    if c_new_prev is not None:
        in_specs.append(pl.BlockSpec(memory_space=pl.ANY))
        body, aliases, passes = _skip_ref(body, len(args)), {len(args): 1}, 1
        args.append(c_new_prev)
    return pl.pallas_call(
        body,
        grid=(b // bb, passes),
        in_specs=in_specs,
        out_specs=[rows(HV), c_out_spec, nspec, rows(HEADS)],
        out_shape=[jax.ShapeDtypeStruct((b, HV), F32), jax.ShapeDtypeStruct(c_all.shape, F32),
                   jax.ShapeDtypeStruct(n.shape, F32), jax.ShapeDtypeStruct((b, HEADS), F32)],
        input_output_aliases=aliases,
        compiler_params=_params(2), name="mlstm_step",
    )(*args)


def _bias_body(bucket_ref, rel_ref, out_ref):
    bucket = bucket_ref[...]
    for h in range(Q_HEADS):
        acc = jnp.zeros(bucket.shape, F32)
        for b in range(REL_BUCKETS):
            acc = jnp.where(bucket == b, rel_ref[b, h], acc)
        out_ref[h] = acc


def _bias_table(bucket, rel_bias):
    shape = bucket.shape
    return pl.pallas_call(
        _bias_body,
        in_specs=[pl.BlockSpec(memory_space=pltpu.VMEM), pl.BlockSpec(memory_space=pltpu.SMEM)],
        out_specs=pl.BlockSpec(memory_space=pltpu.VMEM),
        out_shape=jax.ShapeDtypeStruct((Q_HEADS,) + shape, F32),
        name="rel_bias_table",
    )(bucket, rel_bias)


def _rel_bucket(dist):
    max_exact = REL_BUCKETS // 2
    d = jnp.maximum(dist, 0)
    df = jnp.maximum(d, 1).astype(F32)
    large = max_exact + (jnp.log(df / max_exact) / math.log(REL_MAX_DIST / max_exact)
                         * (REL_BUCKETS - max_exact)).astype(jnp.int32)
    large = jnp.minimum(large, REL_BUCKETS - 1)
    return jnp.where(d < max_exact, d, large)


def _swa_block(q, kt_prev, kt_cur, v_prev, v_cur, bias_ref, sink_ref, in_range, out_ref, rows):
    blk = WINDOW
    qi = lax.broadcasted_iota(jnp.int32, (blk, blk), 0)
    kj = lax.broadcasted_iota(jnp.int32, (blk, blk), 1)
    from_prev = kj > qi
    lane = lax.broadcasted_iota(jnp.int32, (2 * blk, LANES), 1)
    lane_q = lax.broadcasted_iota(jnp.int32, (blk, LANES), 1)
    zero_k = jnp.zeros((HEAD_DIM, 2 * blk), BF16)
    ones_bd = jnp.concatenate([(lane < HEAD_DIM).astype(BF16), (lane >= HEAD_DIM).astype(BF16)], axis=0)

    for g in range(KV_HEADS):
        hd_rows = slice(g * HEAD_DIM, (g + 1) * HEAD_DIM)
        kt_g = jnp.concatenate([kt_prev[hd_rows, :], kt_cur[hd_rows, :]], axis=1)
        rhs = jnp.concatenate([jnp.concatenate([kt_g, zero_k], axis=0),
                               jnp.concatenate([zero_k, kt_g], axis=0)], axis=1)
        v_lanes = slice((g // 2) * LANES, (g // 2 + 1) * LANES)
        vt = jnp.concatenate([v_prev[:, v_lanes], v_cur[:, v_lanes]], axis=0)
        vr = pltpu.roll(vt, HEAD_DIM, axis=1)
        v_lo = jnp.where(lane < HEAD_DIM, vt if g % 2 == 0 else vr, 0.0)
        v_hi = jnp.where(lane >= HEAD_DIM, vr if g % 2 == 0 else vt, 0.0)
        v_bd = jnp.concatenate([jnp.concatenate([v_lo, v_hi], axis=0).astype(BF16), ones_bd], axis=1)
        for pair in range(GROUP // 2):
            jp = g * (GROUP // 2) + pair
            logits = _dot(q[:, jp * LANES:(jp + 1) * LANES], rhs)
            ps, sk = [], []
            for i in range(2):
                hd = 2 * jp + i
                lg = jnp.where(from_prev, logits[:, 2 * i * blk:(2 * i + 1) * blk],
                               logits[:, (2 * i + 1) * blk:(2 * i + 2) * blk]) + bias_ref[hd]
                if in_range is not None:
                    lg = jnp.where(in_range, lg, NEG)
                sink = sink_ref[hd]
                mx = jnp.maximum(jnp.max(lg, axis=1, keepdims=True), sink)
                p = jnp.exp(lg - mx)
                ps += [jnp.where(from_prev, p, 0.0).astype(BF16), jnp.where(from_prev, 0.0, p).astype(BF16)]
                sk.append(jnp.exp(sink - mx))
            pv = _dot(jnp.concatenate(ps, axis=1), v_bd)
            den = pv[:, LANES:] + jnp.where(lane_q < HEAD_DIM, sk[0], sk[1])
            out_ref[rows, jp * LANES:(jp + 1) * LANES] = (pv[:, :LANES] * (1.0 / den)).astype(out_ref.dtype)


def _swa_post_body(q_ref, kt_ref, ktp_ref, v_ref, vp_ref, bias_ref, sink_ref, h_ref, p_ref, *rest, tiles_per_seq):
    w_refs, out_ref, attn_scr = rest[:-2], rest[-2], rest[-1]
    s = pl.program_id(0)
    nt = pl.num_programs(0) - 1
    tile = jnp.minimum(s, nt - 1)

    @pl.when(s == 0)
    def _():
        attn_scr[...] = jnp.zeros(attn_scr.shape, attn_scr.dtype)

    out_ref[...] = _post_rows(attn_scr[(s + 1) % 2], h_ref[...], p_ref[...], *w_refs)
    attn_slot = attn_scr.at[s % 2]
    blk = WINDOW
    qi = lax.broadcasted_iota(jnp.int32, (blk, blk), 0)
    kj = lax.broadcasted_iota(jnp.int32, (blk, blk), 1)
    has_prev = (kj <= qi) | (tile % tiles_per_seq != 0)
    for b in range(q_ref.shape[0] // blk):
        rows, prev_rows = slice(b * blk, (b + 1) * blk), slice((b - 1) * blk, b * blk)
        _swa_block(q_ref[rows, :],
                   ktp_ref[...] if b == 0 else kt_ref[:, prev_rows], kt_ref[:, rows],
                   vp_ref[...] if b == 0 else v_ref[prev_rows, :], v_ref[rows, :],
                   bias_ref, sink_ref, has_prev if b == 0 else None, attn_slot, rows)


def _swa_post(q, kt, v, bias, sinks, h, p, wo, w, *, layer, mixer, batch, seq):
    t = batch * seq
    tm = ROW_TILE
    nt = t // tm
    blocks = tm // WINDOW
    mix_i, mix_row, mix_col, post_row, p_spec = _pipelined_specs(tm, nt, layer)
    prev_blk = lambda s: jnp.maximum(mix_i(s) * blocks - 1, 0)
    w_specs, w_args = _post_weights(wo, w, layer, mixer)
    return pl.pallas_call(
        functools.partial(_swa_post_body, tiles_per_seq=seq // tm),
        grid=(nt + 1,),
        in_specs=[mix_row(D_MODEL), mix_col(KV_DIM), pl.BlockSpec((KV_DIM, WINDOW), lambda s: (0, prev_blk(s))),
                  mix_row(KV_DIM), pl.BlockSpec((WINDOW, KV_DIM), lambda s: (prev_blk(s), 0)),
                  _resident(bias.shape), pl.BlockSpec(memory_space=pltpu.SMEM), post_row(D_MODEL), p_spec] + w_specs,
        out_specs=post_row(D_MODEL),
        out_shape=jax.ShapeDtypeStruct((t, D_MODEL), F32),
        scratch_shapes=[pltpu.VMEM((2, tm, D_MODEL), BF16)],
        compiler_params=_params(1), name="swa_post",
    )(q, kt, kt, v, v, bias, sinks[mixer], h, p, *w_args)


SAMPLE_BB = 8


def _swa_sample_body(q_ref, kc_ref, vc_ref, kn_ref, vn_ref, bias_ref, sink_ref, out_ref, ko_ref, vo_ref):
    w = WINDOW
    row = lax.broadcasted_iota(jnp.int32, (w, KV_DIM), 0)
    rg = lax.broadcasted_iota(jnp.int32, (Q_HEADS, KV_DIM), 0) // GROUP
    lg = lax.broadcasted_iota(jnp.int32, (Q_HEADS, KV_DIM), 1) // HEAD_DIM
    rsel = lax.broadcasted_iota(jnp.int32, (Q_HEADS, HEAD_DIM), 0) // GROUP
    logits, v_wins = [], []
    for i in range(SAMPLE_BB):
        k_win = jnp.where(row == w - 1, kn_ref[i], pltpu.roll(kc_ref[i], w - 1, axis=0))
        v_win = jnp.where(row == w - 1, vn_ref[i], pltpu.roll(vc_ref[i], w - 1, axis=0))
        ko_ref[i] = k_win
        vo_ref[i] = v_win
        qb = q_ref[i].astype(F32)
        q_bd = jnp.where(rg == lg, jnp.concatenate([qb] * KV_HEADS, axis=1), 0.0).astype(BF16)
        logits.append(lax.dot_general(q_bd, k_win.astype(BF16), _NT, preferred_element_type=F32))
        v_wins.append(v_win.astype(BF16))
    lg_all = jnp.concatenate(logits, axis=0) + bias_ref[...]
    sink = sink_ref[...]
    mx = jnp.maximum(jnp.max(lg_all, axis=1, keepdims=True), sink)
    p = jnp.exp(lg_all - mx)
    probs = (p * (1.0 / (jnp.sum(p, axis=1, keepdims=True) + jnp.exp(sink - mx)))).astype(BF16)
    for i in range(SAMPLE_BB):
        o = _dot(probs[i * Q_HEADS:(i + 1) * Q_HEADS], v_wins[i])
        sel = o[:, 0:HEAD_DIM]
        for g in range(1, KV_HEADS):
            sel = jnp.where(rsel == g, o[:, g * HEAD_DIM:(g + 1) * HEAD_DIM], sel)
        out_ref[i] = sel.astype(out_ref.dtype)


def _swa_sample(q, kc_all, vc_all, kn, vn, bias, sinks, *, mixer):
    b = q.shape[0]
    bb = SAMPLE_BB
    blk = lambda shape: pl.BlockSpec((bb,) + shape, lambda i: (i, 0, 0))
    cache = pl.BlockSpec((None, bb, WINDOW, KV_DIM), lambda i: (mixer, i, 0, 0))
    return pl.pallas_call(
        _swa_sample_body,
        grid=(b // bb,),
        in_specs=[blk((Q_HEADS, HEAD_DIM)), cache, cache, blk((1, KV_DIM)), blk((1, KV_DIM)), _resident(bias.shape),
                  _layer(sinks, mixer)],
        out_specs=[blk((Q_HEADS, HEAD_DIM)), cache, cache],
        out_shape=[jax.ShapeDtypeStruct((b, Q_HEADS, HEAD_DIM), BF16), jax.ShapeDtypeStruct(kc_all.shape, F32),
                   jax.ShapeDtypeStruct(vc_all.shape, F32)],
        input_output_aliases={1: 1, 2: 2},
        compiler_params=_params(1), name="swa_sample",
    )(q, kc_all, vc_all, kn, vn, bias, sinks)


def kernel(x_prompt, x_sample, state_mlstm_C, state_mlstm_n, state_mlstm_m, cache_swa_k, cache_swa_v, p_prompt, p_sample, rel_bias, norm_pre_mix, norm_post_mix, norm_pre_ffn, norm_post_ffn, mlstm_w_in, mlstm_b_i, mlstm_b_f, mlstm_norm, mlstm_w_out, swa_w_qkv, swa_sinks, swa_w_o, ffn_w_up, ffn_w_down, ple_w_proj, ple_norm, ple_w_gate):
    bp, sp, _ = x_prompt.shape
    bs = x_sample.shape[0]
    win = cache_swa_k.shape[2]
    assert win == WINDOW and x_sample.shape[1] == 1
    bf = lambda a: a.astype(BF16)
    vecs = lambda a: a.astype(F32)[:, None, :]

    post_w = dict(npm=vecs(norm_post_mix), npf=vecs(norm_pre_ffn), wup=bf(ffn_w_up), wdn=bf(ffn_w_down),
                  npff=vecs(norm_post_ffn), wpj=bf(ple_w_proj), pln=vecs(ple_norm), wgate=bf(ple_w_gate))
    n_pre = vecs(norm_pre_mix)
    m_w = bf(mlstm_w_in)
    m_bias = jnp.concatenate([mlstm_b_i, mlstm_b_f], axis=1).astype(F32)
    m_wout, m_norm = bf(mlstm_w_out), vecs(mlstm_norm)
    s_w, s_wo = bf(swa_w_qkv), bf(swa_w_o)
    sinks = swa_sinks.astype(F32)

    qi = jnp.arange(WINDOW)[:, None]
    kj = jnp.arange(WINDOW)[None, :]
    dist_p = jnp.where(kj > qi, qi + WINDOW - kj, qi - kj)
    bias_p = _bias_table(_rel_bucket(dist_p).astype(jnp.int32), rel_bias.astype(F32))
    dist_s = (win - 1) - jnp.arange(win)
    bias_s = _bias_table(jnp.broadcast_to(_rel_bucket(dist_s)[None, :], (8, win)).astype(jnp.int32),
                         rel_bias.astype(F32))[:, 0, :]
    bias_s = jnp.tile(bias_s, (SAMPLE_BB, 1))
    sinks_s = jnp.tile(sinks[:, :, None], (1, SAMPLE_BB, 1))

    hp = x_prompt.reshape(bp * sp, D_MODEL)
    hs = x_sample.reshape(bs, D_MODEL)
    pp = p_prompt.reshape(DEPTH, bp * sp, PLE_DIM)
    ps = p_sample.reshape(DEPTH, bs, PLE_DIM)
    c_old = state_mlstm_C.astype(F32).reshape(-1, bs, HK, DV)
    k_win = cache_swa_k.astype(F32).reshape(-1, bs, win, KV_DIM)
    v_win = cache_swa_v.astype(F32).reshape(-1, bs, win, KV_DIM)
    c_new = None
    outs = dict(Cp=[], np=[], mp=[], kp=[], vp=[], ns=[], ms=[])

    for i in range(DEPTH):
        j = i // N_MIXERS
        if i % N_MIXERS == 0:
            wo = m_wout
            q, kt, v, o, gt = _mlstm_in(hp, n_pre, m_w, m_bias[j].reshape(-1, 1),
                                        layer=i, mixer=j, prompt=True, act_dtype=BF16)
            hp, c_fin, n_fin, m_fin = _mlstm_post(q, kt, v, o, gt, m_norm, hp, pp, wo, post_w,
                                                  layer=i, mixer=j, batch=bp, seq=sp)
            outs["Cp"].append(c_fin)
            outs["np"].append(n_fin[..., 0])
            outs["mp"].append(m_fin[..., 0])
            q, k, v, o, g = _mlstm_in(hs, n_pre, m_w, m_bias[j].reshape(1, -1),
                                      layer=i, mixer=j, prompt=False, act_dtype=F32)
            a_s, c_new, n_new, m_new = _mlstm_step(
                c_old, c_new, state_mlstm_n[j].astype(F32), state_mlstm_m[j].astype(F32),
                q, k, v, o, g, m_norm, mixer=j)
            outs["ns"].append(n_new)
            outs["ms"].append(m_new)
        else:
            wo = s_wo
            q, k, kt, v = _swa_in(hp, n_pre, s_w, layer=i, mixer=j)
            hp = _swa_post(q, kt, v, bias_p, sinks, hp, pp, wo, post_w, layer=i, mixer=j, batch=bp, seq=sp)
            last = lambda a: a.reshape(bp, sp, KV_DIM)[:, sp - win:].reshape(bp, win, KV_HEADS, HEAD_DIM)
            outs["kp"].append(last(k))
            outs["vp"].append(last(v))
            q, k, _, v = _swa_in(hs, n_pre, s_w, layer=i, mixer=j)
            a_s, k_win, v_win = _swa_sample(
                q.reshape(bs, Q_HEADS, HEAD_DIM), k_win, v_win, k.reshape(bs, 1, KV_DIM), v.reshape(bs, 1, KV_DIM),
                bias_s, sinks_s, mixer=j)
            a_s = a_s.reshape(bs, D_MODEL)
        hs = _post(a_s, hs, ps, wo, post_w, layer=i, mixer=j)

    st = lambda key, like: jnp.stack(outs[key]).astype(like.dtype)
    kv_out = lambda a, like: a.reshape(-1, bs, win, KV_HEADS, HEAD_DIM).astype(like.dtype)
    return (hp.reshape(x_prompt.shape), hs.reshape(x_sample.shape),
            st("Cp", state_mlstm_C), st("np", state_mlstm_n), st("mp", state_mlstm_m),
            st("kp", cache_swa_k), st("vp", cache_swa_v),
            c_new.reshape(state_mlstm_C.shape).astype(state_mlstm_C.dtype), st("ns", state_mlstm_n),
            st("ms", state_mlstm_m), kv_out(k_win, cache_swa_k), kv_out(v_win, cache_swa_v))
```

```python
import functools
import math

import jax
import jax.numpy as jnp
from jax import lax
from jax.experimental import pallas as pl
from jax.experimental.pallas import tpu as pltpu

F32 = jnp.float32
BF16 = jnp.bfloat16

D_MODEL = 1024
DEPTH = 4
N_MIXERS = 2
HEADS = 8
DQK = 64
DV = 128
CHUNK = 128
Q_HEADS = 16
KV_HEADS = 4
GROUP = Q_HEADS // KV_HEADS
HEAD_DIM = 64
KV_DIM = KV_HEADS * HEAD_DIM
WINDOW = 128
REL_BUCKETS = 32
REL_MAX_DIST = 128
PLE_DIM = 256
D_FF = 4 * D_MODEL
EPS = 1e-6
NEG = -1e30

HK, HV = HEADS * DQK, HEADS * DV
Q0, K0, V0, O0, G0 = 0, HK, 2 * HK, 2 * HK + HV, 2 * HK + HV + D_MODEL
MLSTM_IN = G0 + 2 * HEADS

V7X_VMEM_BYTES = 64 * 1024 * 1024
VMEM_LIMIT = V7X_VMEM_BYTES - 8 * 1024 * 1024
LANES = 128
ROW_TILE = 512

_NT = (((1,), (1,)), ((), ()))


def _row_tile(t):
    return ROW_TILE if t % ROW_TILE == 0 else t


def _params(n_grid):
    return pltpu.CompilerParams(dimension_semantics=("arbitrary",) * n_grid, vmem_limit_bytes=VMEM_LIMIT)


def _resident(shape):
    nd = len(shape)
    return pl.BlockSpec(shape, lambda *_: (0,) * nd, pipeline_mode=pl.Buffered(1))


def _layer(arr, layer):
    tail = arr.shape[1:]
    return pl.BlockSpec((None,) + tail, lambda *_: (layer,) + (0,) * len(tail), pipeline_mode=pl.Buffered(1))


def _skip_ref(body, pos):
    def wrapped(*refs):
        return body(*refs[:pos], *refs[pos + 1:])
    return wrapped


def _rms(x, w=None):
    y = x * lax.rsqrt(jnp.mean(x * x, axis=-1, keepdims=True) + EPS)
    return y if w is None else y * w


def _log_sigmoid(x):
    return -(jnp.maximum(-x, 0.0) + jnp.log1p(jnp.exp(-jnp.abs(x))))


def _sigmoid(x):
    return 1.0 / (1.0 + jnp.exp(-x))


def _dot(a, b):
    return jnp.dot(a, b, preferred_element_type=F32)


def _mlstm_in_body(h_ref, nw_ref, w_ref, bg_ref, q_ref, k_ref, v_ref, o_ref, g_ref, *scratch, prompt):
    if prompt:
        wkgt_scr, = scratch

        @pl.when(pl.program_id(0) == 0)
        def _():
            wkgt_scr[0:HK, :] = w_ref[:, K0:V0].astype(F32).T.astype(BF16)
            wg = jnp.concatenate([w_ref[:, G0:MLSTM_IN].astype(F32),
                                  jnp.zeros((D_MODEL, LANES - 2 * HEADS), F32)], axis=1)
            wkgt_scr[HK:, :] = wg.T.astype(BF16)

    u = _rms(h_ref[...], nw_ref[...]).astype(BF16)
    q_ref[...] = (_dot(u, w_ref[:, Q0:K0]) * (DQK ** -0.5)).astype(q_ref.dtype)
    v_ref[...] = _dot(u, w_ref[:, V0:O0]).astype(v_ref.dtype)
    o_ref[...] = _dot(u, w_ref[:, O0:G0])
    if prompt:
        kgt = lax.dot_general(wkgt_scr[...], u, _NT, preferred_element_type=F32)
        k_ref[...] = kgt[:HK].astype(k_ref.dtype)
        gt = kgt[HK:HK + 2 * HEADS] + bg_ref[...]
        ig, lf = gt[:HEADS], _log_sigmoid(gt[HEADS:])
        tm = lf.shape[1]
        r = lax.broadcasted_iota(jnp.int32, (tm, tm), 0)
        t = lax.broadcasted_iota(jnp.int32, (tm, tm), 1)
        same_chunk_upper = ((r <= t) & (r // CHUNK == t // CHUNK)).astype(F32)
        b = jnp.dot(lf, same_chunk_upper, precision=lax.Precision.HIGHEST, preferred_element_type=F32)
        g_ref[...] = jnp.concatenate([ig, lf, b], axis=0)
    else:
        k_ref[...] = _dot(u, w_ref[:, K0:V0]).astype(k_ref.dtype)
        g = _dot(u, w_ref[:, G0:MLSTM_IN]) + bg_ref[...]
        lane = lax.broadcasted_iota(jnp.int32, g.shape, 1)
        g_ref[...] = jnp.where(lane < HEADS, g, _log_sigmoid(g))


def _mlstm_in(h, nw, w, bg, *, layer, mixer, prompt, act_dtype):
    t = h.shape[0]
    tm = _row_tile(t)
    row = lambda n: pl.BlockSpec((tm, n), lambda i: (i, 0))
    col = lambda n: pl.BlockSpec((n, tm), lambda i: (0, i))
    kshape, kspec = ((HK, t), col(HK)) if prompt else ((t, HK), row(HK))
    gshape, gspec = ((3 * HEADS, t), col(3 * HEADS)) if prompt else ((t, 2 * HEADS), row(2 * HEADS))
    return pl.pallas_call(
        functools.partial(_mlstm_in_body, prompt=prompt),
        grid=(t // tm,),
        in_specs=[row(D_MODEL), _layer(nw, layer), _layer(w, mixer), _resident(bg.shape)],
        out_specs=[row(HK), kspec, row(HV), row(D_MODEL), gspec],
        out_shape=[jax.ShapeDtypeStruct((t, HK), act_dtype), jax.ShapeDtypeStruct(kshape, act_dtype),
                   jax.ShapeDtypeStruct((t, HV), act_dtype), jax.ShapeDtypeStruct((t, D_MODEL), F32),
                   jax.ShapeDtypeStruct(gshape, F32)],
        scratch_shapes=[pltpu.VMEM((HK + LANES, D_MODEL), BF16)] if prompt else [],
        compiler_params=_params(1), name="mlstm_in",
    )(h, nw, w, bg)


def _swa_in_body(h_ref, nw_ref, w_ref, q_ref, k_ref, kt_ref, v_ref):
    u = _rms(h_ref[...], nw_ref[...]).astype(BF16)
    q_ref[...] = (_dot(u, w_ref[:, :D_MODEL]) * (HEAD_DIM ** -0.5)).astype(q_ref.dtype)
    k = _dot(u, w_ref[:, D_MODEL:D_MODEL + KV_DIM])
    k_ref[...] = k
    kt_ref[...] = k.T.astype(kt_ref.dtype)
    v_ref[...] = _dot(u, w_ref[:, D_MODEL + KV_DIM:])


def _swa_in(h, nw, w, *, layer, mixer):
    t = h.shape[0]
    tm = _row_tile(t)
    row = lambda n: pl.BlockSpec((tm, n), lambda i: (i, 0))
    return pl.pallas_call(
        _swa_in_body,
        grid=(t // tm,),
        in_specs=[row(D_MODEL), _layer(nw, layer), _layer(w, mixer)],
        out_specs=[row(D_MODEL), row(KV_DIM), pl.BlockSpec((KV_DIM, tm), lambda i: (0, i)), row(KV_DIM)],
        out_shape=[jax.ShapeDtypeStruct((t, D_MODEL), BF16), jax.ShapeDtypeStruct((t, KV_DIM), F32),
                   jax.ShapeDtypeStruct((KV_DIM, t), BF16), jax.ShapeDtypeStruct((t, KV_DIM), F32)],
        compiler_params=_params(1), name="swa_in",
    )(h, nw, w)


FF_CHUNK = 1024


def _post_rows(a, h, p, wo_ref, npm_ref, npf_ref, wup_ref, wdn_ref, npff_ref, wpj_ref, pln_ref, wgate_ref):
    y = _dot(a.astype(BF16), wo_ref[...])
    h1 = h + _rms(y, npm_ref[...])
    u = _rms(h1, npf_ref[...]).astype(BF16)
    acc = jnp.zeros(h1.shape, F32)
    for c in range(D_FF // FF_CHUNK):
        hid = jnp.maximum(_dot(u, wup_ref[:, c * FF_CHUNK:(c + 1) * FF_CHUNK]), 0.0)
        acc = acc + _dot((hid * hid).astype(BF16), wdn_ref[c * FF_CHUNK:(c + 1) * FF_CHUNK, :])
    h2 = h1 + _rms(acc, npff_ref[...])
    e = _rms(_dot(p.astype(BF16), wpj_ref[...]), pln_ref[...])
    g = _sigmoid(_dot(_rms(h2).astype(BF16), wgate_ref[...]))
    return h2 + e * g


def _post_body(a_ref, h_ref, p_ref, *rest):
    w_refs, out_ref = rest[:-1], rest[-1]
    out_ref[...] = _post_rows(a_ref[...], h_ref[...], p_ref[...], *w_refs)


_POST_KEYS = ("npm", "npf", "wup", "wdn", "npff", "wpj", "pln", "wgate")


def _post_weights(wo, w, layer, mixer):
    return ([_layer(wo, mixer)] + [_layer(w[k], layer) for k in _POST_KEYS], [wo] + [w[k] for k in _POST_KEYS])


def _post(a, h, p, wo, w, *, layer, mixer):
    t = h.shape[0]
    tm = _row_tile(t)
    row = lambda n: pl.BlockSpec((tm, n), lambda i: (i, 0))
    w_specs, w_args = _post_weights(wo, w, layer, mixer)
    return pl.pallas_call(
        _post_body,
        grid=(t // tm,),
        in_specs=[row(D_MODEL), row(D_MODEL), pl.BlockSpec((None, tm, PLE_DIM), lambda i: (layer, i, 0))] + w_specs,
        out_specs=row(D_MODEL),
        out_shape=jax.ShapeDtypeStruct((t, D_MODEL), F32),
        compiler_params=_params(1), name="post",
    )(a, h, p, *w_args)


def _pipelined_specs(tm, nt, layer):
    mix_i = lambda s: jnp.minimum(s, nt - 1)
    post_i = lambda s: jnp.maximum(s - 1, 0)
    mix_row = lambda n: pl.BlockSpec((tm, n), lambda s: (mix_i(s), 0))
    mix_col = lambda n: pl.BlockSpec((n, tm), lambda s: (0, mix_i(s)))
    post_row = lambda n: pl.BlockSpec((tm, n), lambda s: (post_i(s), 0))
    p_spec = pl.BlockSpec((None, tm, PLE_DIM), lambda s: (layer, post_i(s), 0))
    return mix_i, mix_row, mix_col, post_row, p_spec


STATE_W = 2 * DV


def _scan_chunk(q_ref, kt_ref, v_ref, o_ref, gt_ref, nw_ref, hg_ref, cn_scr, m_scr, r0):
    L = CHUNK
    rows = slice(r0, r0 + L)
    ri = lax.broadcasted_iota(jnp.int32, (L, L), 0)
    ci = lax.broadcasted_iota(jnp.int32, (L, L), 1)
    tri = ci <= ri
    gt = gt_ref[:, rows]
    ig_rows, lf_rows, b_rows = gt[:HEADS], gt[HEADS:2 * HEADS], gt[2 * HEADS:]
    a_rows = ig_rows - b_rows
    m_prev = m_scr[...]
    b_last = jnp.broadcast_to(b_rows[:, L - 1:L], (HEADS, L))
    m_new = b_last + jnp.maximum(m_prev, jnp.max(a_rows, axis=1, keepdims=True))
    decay_rows = jnp.exp(b_last + m_prev - m_new)
    wk_rows = jnp.exp(b_last - b_rows + ig_rows - m_new)
    m_scr[...] = m_new

    ones_rhs = jnp.ones((L, DV), BF16)
    zeros_kt = jnp.zeros((DQK, L), BF16)
    zeros_cn = jnp.zeros((DQK, STATE_W), BF16)
    for j in range(HEADS // 2):
        q_pair = q_ref[rows, j * 128:(j + 1) * 128]
        kt_pair = [kt_ref[h * DQK:(h + 1) * DQK, rows] for h in (2 * j, 2 * j + 1)]
        s_pair = _dot(q_pair, jnp.concatenate([jnp.concatenate([kt_pair[0], zeros_kt], axis=0),
                                               jnp.concatenate([zeros_kt, kt_pair[1]], axis=0)], axis=1))
        for par in range(2):
            h = 2 * j + par
            kt_h = kt_pair[par]
            v_ext = jnp.concatenate([v_ref[rows, h * DV:(h + 1) * DV], ones_rhs], axis=1)
            cn = cn_scr[h]
            cn16 = cn.astype(BF16)
            a_row, m_row = a_rows[h:h + 1], m_prev[h:h + 1]

            big_m = jnp.maximum(jnp.max(jnp.where(tri, a_row, -jnp.inf), axis=1, keepdims=True), m_row)
            b_col = jnp.sum(jnp.where(tri, lf_rows[h:h + 1], 0.0), axis=1, keepdims=True)
            w_inter = jnp.exp(m_row - big_m)
            s = s_pair[:, par * L:(par + 1) * L] * jnp.exp(jnp.where(tri, a_row - big_m, NEG))
            sv = _dot(s.astype(BF16), v_ext)
            qc = _dot(q_pair, jnp.concatenate([cn16, zeros_cn] if par == 0 else [zeros_cn, cn16], axis=0))
            num = w_inter * qc[:, :DV] + sv[:, :DV]
            den = w_inter * qc[:, DV:] + sv[:, DV:]
            hh = num * (1.0 / jnp.maximum(jnp.abs(den), jnp.exp(-(b_col + big_m))))

            kw_t = kt_h.astype(F32) * wk_rows[h:h + 1]
            dec = decay_rows[h:h + 1]
            cn_scr[h] = jnp.concatenate([dec, dec], axis=1) * cn + _dot(kw_t.astype(BF16), v_ext)

            hn = _rms(hh, nw_ref[:, h * DV:(h + 1) * DV])
            hg_ref[rows, h * DV:(h + 1) * DV] = (hn * _sigmoid(o_ref[rows, h * DV:(h + 1) * DV])
                                                 ).astype(hg_ref.dtype)


def _mlstm_post_body(q_ref, kt_ref, v_ref, o_ref, gt_ref, nw_ref, h_ref, p_ref, *rest, tiles_per_seq):
    w_refs = rest[:-7]
    out_ref, c_out_ref, n_out_ref, m_out_ref, hg_scr, cn_scr, m_scr = rest[-7:]
    s = pl.program_id(0)
    nt = pl.num_programs(0) - 1
    tile = jnp.minimum(s, nt - 1)

    @pl.when(s == 0)
    def _():
        hg_scr[...] = jnp.zeros(hg_scr.shape, hg_scr.dtype)

    @pl.when(tile % tiles_per_seq == 0)
    def _():
        cn_scr[...] = jnp.zeros(cn_scr.shape, F32)
        m_scr[...] = jnp.zeros(m_scr.shape, F32)

    out_ref[...] = _post_rows(hg_scr[(s + 1) % 2], h_ref[...], p_ref[...], *w_refs)
    hg_slot = hg_scr.at[s % 2]
    for r0 in range(0, q_ref.shape[0], CHUNK):
        _scan_chunk(q_ref, kt_ref, v_ref, o_ref, gt_ref, nw_ref, hg_slot, cn_scr, m_scr, r0)

    @pl.when((tile % tiles_per_seq == tiles_per_seq - 1) & (s < nt))
    def _():
        for h in range(HEADS):
            c_out_ref[0, h] = cn_scr[h, :, 0:DV]
            n_out_ref[0, h] = cn_scr[h, :, DV:STATE_W]
        m_out_ref[0] = m_scr[...]


def _mlstm_post(q, kt, v, o, gt, nw, h, p, wo, w, *, layer, mixer, batch, seq):
    t = batch * seq
    tm = ROW_TILE
    nt = t // tm
    tiles_per_seq = seq // tm
    mix_i, mix_row, mix_col, post_row, p_spec = _pipelined_specs(tm, nt, layer)
    seq_i = lambda s: mix_i(s) // tiles_per_seq
    w_specs, w_args = _post_weights(wo, w, layer, mixer)
    return pl.pallas_call(
        functools.partial(_mlstm_post_body, tiles_per_seq=tiles_per_seq),
        grid=(nt + 1,),
        in_specs=[mix_row(HK), mix_col(HK), mix_row(HV), mix_row(D_MODEL), mix_col(3 * HEADS), _layer(nw, mixer),
                  post_row(D_MODEL), p_spec] + w_specs,
        out_specs=[post_row(D_MODEL),
                   pl.BlockSpec((1, HEADS, DQK, DV), lambda s: (seq_i(s), 0, 0, 0)),
                   pl.BlockSpec((1, HEADS, DQK, STATE_W - DV), lambda s: (seq_i(s), 0, 0, 0)),
                   pl.BlockSpec((1, HEADS, LANES), lambda s: (seq_i(s), 0, 0))],
        out_shape=[jax.ShapeDtypeStruct((t, D_MODEL), F32),
                   jax.ShapeDtypeStruct((batch, HEADS, DQK, DV), F32),
                   jax.ShapeDtypeStruct((batch, HEADS, DQK, STATE_W - DV), F32),
                   jax.ShapeDtypeStruct((batch, HEADS, LANES), F32)],
        scratch_shapes=[pltpu.VMEM((2, tm, HV), BF16), pltpu.VMEM((HEADS, DQK, STATE_W), F32),
                        pltpu.VMEM((HEADS, LANES), F32)],
        compiler_params=_params(1), name="mlstm_post",
    )(q, kt, v, o, gt, nw, h, p, *w_args)


STEP_BB = 8
_TN = (((0,), (0,)), ((), ()))


def _mlstm_step_body(c_ref, n_ref, m_ref, q_ref, k_ref, v_ref, o_ref, g_ref, nw_ref,
                     hg_ref, c_out_ref, n_out_ref, m_out_ref, *, own):
    bb = STEP_BB
    for slab in range(c_out_ref.shape[0]):
        if slab != own:
            c_out_ref[slab] = jnp.zeros(c_out_ref.shape[1:], F32)
    g = g_ref[...]
    ig, lf, m_prev = g[:, :HEADS], g[:, HEADS:], m_ref[...]
    inter = lf + m_prev
    m_t = jnp.maximum(inter, ig)
    w_inter = jnp.exp(inter - m_t)
    in_w = jnp.exp(ig - m_t)
    decay = jnp.exp(lf + m_prev - m_t)
    floor = jnp.exp(-m_t)
    m_out_ref[...] = m_t
    row_k = lax.broadcasted_iota(jnp.int32, (bb, DQK), 0)
    row_v = lax.broadcasted_iota(jnp.int32, (bb, DV), 0)

    for h in range(HEADS):
        col = lambda a: a[:, h:h + 1]
        q = q_ref[:, h * DQK:(h + 1) * DQK]
        k = k_ref[:, h * DQK:(h + 1) * DQK]
        v = v_ref[:, h * DV:(h + 1) * DV]
        n = n_ref[:, h, :]
        s = jnp.sum(q * k, axis=1, keepdims=True) * col(in_w)
        kw = k * col(in_w)
        q16, v16 = q.astype(BF16), v.astype(BF16)
        qc = jnp.zeros((bb, DV), F32)
        for i in range(bb):
            c_i = c_ref[i, h * DQK:(h + 1) * DQK, :]
            qc = jnp.where(row_v == i, _dot(q16, c_i.astype(BF16)), qc)
            outer = lax.dot_general(jnp.where(row_k == i, kw, 0.0).astype(BF16), v16, _TN,
                                    preferred_element_type=F32)
            c_out_ref[own, i, h * DQK:(h + 1) * DQK, :] = decay[i:i + 1, h:h + 1] * c_i + outer
        num = col(w_inter) * qc + s * v
        den = col(w_inter) * jnp.sum(q * n, axis=1, keepdims=True) + s
        hh = num * (1.0 / jnp.maximum(jnp.abs(den), col(floor)))
        n_out_ref[:, h, :] = col(decay) * n + kw
        hg_ref[:, h * DV:(h + 1) * DV] = (_rms(hh, nw_ref[:, h * DV:(h + 1) * DV])
                                          * _sigmoid(o_ref[:, h * DV:(h + 1) * DV]))


def _mlstm_step(c_all, c_new_prev, n, m, q, k, v, o, g, nw, *, mixer):
    n_slabs, b = c_all.shape[:2]
    bb = STEP_BB
    rows = lambda width: pl.BlockSpec((bb, width), lambda i: (i, 0))
    cspec = pl.BlockSpec((None, bb, HK, DV), lambda i: (mixer, i, 0, 0))
    nspec = pl.BlockSpec((bb, HEADS, DQK), lambda i: (i, 0, 0))
    in_specs = [cspec, nspec, rows(HEADS), rows(HK), rows(HK), rows(HV), rows(D_MODEL), rows(2 * HEADS),
                pl.BlockSpec((None, 1, HV), lambda i: (mixer, 0, 0))]
    args = [c_all, n, m, q, k, v, o, g, nw]
    if c_new_prev is None:
        c_out_spec = pl.BlockSpec((n_slabs, bb, HK, DV), lambda i: (0, i, 0, 0))
        body, aliases = functools.partial(_mlstm_step_body, own=mixer), {}
    else:
        c_out_spec = pl.BlockSpec((1, bb, HK, DV), lambda i: (mixer, i, 0, 0))
        in_specs.append(pl.BlockSpec(memory_space=pl.ANY))
        body, aliases = _skip_ref(functools.partial(_mlstm_step_body, own=0), len(args)), {len(args): 1}
        args.append(c_new_prev)
    return pl.pallas_call(
        body,
        grid=(b // bb,),
        in_specs=in_specs,
        out_specs=[rows(HV), c_out_spec, nspec, rows(HEADS)],
        out_shape=[jax.ShapeDtypeStruct((b, HV), F32), jax.ShapeDtypeStruct(c_all.shape, F32),
                   jax.ShapeDtypeStruct(n.shape, F32), jax.ShapeDtypeStruct((b, HEADS), F32)],
        input_output_aliases=aliases,
        compiler_params=_params(1), name="mlstm_step",
    )(*args)


def _bias_body(bucket_ref, rel_ref, out_ref):
    bucket = bucket_ref[...]
    for h in range(Q_HEADS):
        acc = jnp.zeros(bucket.shape, F32)
        for b in range(REL_BUCKETS):
            acc = jnp.where(bucket == b, rel_ref[b, h], acc)
        out_ref[h] = acc


def _bias_table(bucket, rel_bias):
    shape = bucket.shape
    return pl.pallas_call(
        _bias_body,
        in_specs=[pl.BlockSpec(memory_space=pltpu.VMEM), pl.BlockSpec(memory_space=pltpu.SMEM)],
        out_specs=pl.BlockSpec(memory_space=pltpu.VMEM),
        out_shape=jax.ShapeDtypeStruct((Q_HEADS,) + shape, F32),
        name="rel_bias_table",
    )(bucket, rel_bias)


def _rel_bucket(dist):
    max_exact = REL_BUCKETS // 2
    d = jnp.maximum(dist, 0)
    df = jnp.maximum(d, 1).astype(F32)
    large = max_exact + (jnp.log(df / max_exact) / math.log(REL_MAX_DIST / max_exact)
                         * (REL_BUCKETS - max_exact)).astype(jnp.int32)
    large = jnp.minimum(large, REL_BUCKETS - 1)
    return jnp.where(d < max_exact, d, large)


def _swa_block(q, kt_prev, kt_cur, v_prev, v_cur, bias_ref, sink_ref, in_range, out_ref, rows):
    blk = WINDOW
    qi = lax.broadcasted_iota(jnp.int32, (blk, blk), 0)
    kj = lax.broadcasted_iota(jnp.int32, (blk, blk), 1)
    from_prev = kj > qi
    lane = lax.broadcasted_iota(jnp.int32, (2 * blk, LANES), 1)
    lane_q = lax.broadcasted_iota(jnp.int32, (blk, LANES), 1)
    zero_k = jnp.zeros((HEAD_DIM, 2 * blk), BF16)
    ones_bd = jnp.concatenate([(lane < HEAD_DIM).astype(BF16), (lane >= HEAD_DIM).astype(BF16)], axis=0)

    for g in range(KV_HEADS):
        hd_rows = slice(g * HEAD_DIM, (g + 1) * HEAD_DIM)
        kt_g = jnp.concatenate([kt_prev[hd_rows, :], kt_cur[hd_rows, :]], axis=1)
        rhs = jnp.concatenate([jnp.concatenate([kt_g, zero_k], axis=0),
                               jnp.concatenate([zero_k, kt_g], axis=0)], axis=1)
        v_lanes = slice((g // 2) * LANES, (g // 2 + 1) * LANES)
        vt = jnp.concatenate([v_prev[:, v_lanes], v_cur[:, v_lanes]], axis=0)
        vr = pltpu.roll(vt, HEAD_DIM, axis=1)
        v_lo = jnp.where(lane < HEAD_DIM, vt if g % 2 == 0 else vr, 0.0)
        v_hi = jnp.where(lane >= HEAD_DIM, vr if g % 2 == 0 else vt, 0.0)
        v_bd = jnp.concatenate([jnp.concatenate([v_lo, v_hi], axis=0).astype(BF16), ones_bd], axis=1)
        for pair in range(GROUP // 2):
            jp = g * (GROUP // 2) + pair
            logits = _dot(q[:, jp * LANES:(jp + 1) * LANES], rhs)
            ps, sk = [], []
            for i in range(2):
                hd = 2 * jp + i
                lg = jnp.where(from_prev, logits[:, 2 * i * blk:(2 * i + 1) * blk],
                               logits[:, (2 * i + 1) * blk:(2 * i + 2) * blk]) + bias_ref[hd]
                if in_range is not None:
                    lg = jnp.where(in_range, lg, NEG)
                sink = sink_ref[hd]
                mx = jnp.maximum(jnp.max(lg, axis=1, keepdims=True), sink)
                p = jnp.exp(lg - mx)
                ps += [jnp.where(from_prev, p, 0.0).astype(BF16), jnp.where(from_prev, 0.0, p).astype(BF16)]
                sk.append(jnp.exp(sink - mx))
            pv = _dot(jnp.concatenate(ps, axis=1), v_bd)
            den = pv[:, LANES:] + jnp.where(lane_q < HEAD_DIM, sk[0], sk[1])
            out_ref[rows, jp * LANES:(jp + 1) * LANES] = (pv[:, :LANES] * (1.0 / den)).astype(out_ref.dtype)


def _swa_post_body(q_ref, kt_ref, ktp_ref, v_ref, vp_ref, bias_ref, sink_ref, h_ref, p_ref, *rest, tiles_per_seq):
    w_refs, out_ref, attn_scr = rest[:-2], rest[-2], rest[-1]
    s = pl.program_id(0)
    nt = pl.num_programs(0) - 1
    tile = jnp.minimum(s, nt - 1)

    @pl.when(s == 0)
    def _():
        attn_scr[...] = jnp.zeros(attn_scr.shape, attn_scr.dtype)

    out_ref[...] = _post_rows(attn_scr[(s + 1) % 2], h_ref[...], p_ref[...], *w_refs)
    attn_slot = attn_scr.at[s % 2]
    blk = WINDOW
    qi = lax.broadcasted_iota(jnp.int32, (blk, blk), 0)
    kj = lax.broadcasted_iota(jnp.int32, (blk, blk), 1)
    has_prev = (kj <= qi) | (tile % tiles_per_seq != 0)
    for b in range(q_ref.shape[0] // blk):
        rows, prev_rows = slice(b * blk, (b + 1) * blk), slice((b - 1) * blk, b * blk)
        _swa_block(q_ref[rows, :],
                   ktp_ref[...] if b == 0 else kt_ref[:, prev_rows], kt_ref[:, rows],
                   vp_ref[...] if b == 0 else v_ref[prev_rows, :], v_ref[rows, :],
                   bias_ref, sink_ref, has_prev if b == 0 else None, attn_slot, rows)


def _swa_post(q, kt, v, bias, sinks, h, p, wo, w, *, layer, mixer, batch, seq):
    t = batch * seq
    tm = ROW_TILE
    nt = t // tm
    blocks = tm // WINDOW
    mix_i, mix_row, mix_col, post_row, p_spec = _pipelined_specs(tm, nt, layer)
    prev_blk = lambda s: jnp.maximum(mix_i(s) * blocks - 1, 0)
    w_specs, w_args = _post_weights(wo, w, layer, mixer)
    return pl.pallas_call(
        functools.partial(_swa_post_body, tiles_per_seq=seq // tm),
        grid=(nt + 1,),
        in_specs=[mix_row(D_MODEL), mix_col(KV_DIM), pl.BlockSpec((KV_DIM, WINDOW), lambda s: (0, prev_blk(s))),
                  mix_row(KV_DIM), pl.BlockSpec((WINDOW, KV_DIM), lambda s: (prev_blk(s), 0)),
                  _resident(bias.shape), pl.BlockSpec(memory_space=pltpu.SMEM), post_row(D_MODEL), p_spec] + w_specs,
        out_specs=post_row(D_MODEL),
        out_shape=jax.ShapeDtypeStruct((t, D_MODEL), F32),
        scratch_shapes=[pltpu.VMEM((2, tm, D_MODEL), BF16)],
        compiler_params=_params(1), name="swa_post",
    )(q, kt, kt, v, v, bias, sinks[mixer], h, p, *w_args)


SAMPLE_BB = 8


def _swa_sample_body(q_ref, kc_ref, vc_ref, kn_ref, vn_ref, bias_ref, sink_ref, out_ref, ko_ref, vo_ref):
    w = WINDOW
    row = lax.broadcasted_iota(jnp.int32, (w, KV_DIM), 0)
    rg = lax.broadcasted_iota(jnp.int32, (Q_HEADS, KV_DIM), 0) // GROUP
    lg = lax.broadcasted_iota(jnp.int32, (Q_HEADS, KV_DIM), 1) // HEAD_DIM
    rsel = lax.broadcasted_iota(jnp.int32, (Q_HEADS, HEAD_DIM), 0) // GROUP
    logits, v_wins = [], []
    for i in range(SAMPLE_BB):
        k_win = jnp.where(row == w - 1, kn_ref[i], pltpu.roll(kc_ref[i], w - 1, axis=0))
        v_win = jnp.where(row == w - 1, vn_ref[i], pltpu.roll(vc_ref[i], w - 1, axis=0))
        ko_ref[i] = k_win
        vo_ref[i] = v_win
        qb = q_ref[i].astype(F32)
        q_bd = jnp.where(rg == lg, jnp.concatenate([qb] * KV_HEADS, axis=1), 0.0).astype(BF16)
        logits.append(lax.dot_general(q_bd, k_win.astype(BF16), _NT, preferred_element_type=F32))
        v_wins.append(v_win.astype(BF16))
    lg_all = jnp.concatenate(logits, axis=0) + bias_ref[...]
    sink = sink_ref[...]
    mx = jnp.maximum(jnp.max(lg_all, axis=1, keepdims=True), sink)
    p = jnp.exp(lg_all - mx)
    probs = (p * (1.0 / (jnp.sum(p, axis=1, keepdims=True) + jnp.exp(sink - mx)))).astype(BF16)
    for i in range(SAMPLE_BB):
        o = _dot(probs[i * Q_HEADS:(i + 1) * Q_HEADS], v_wins[i])
        sel = o[:, 0:HEAD_DIM]
        for g in range(1, KV_HEADS):
            sel = jnp.where(rsel == g, o[:, g * HEAD_DIM:(g + 1) * HEAD_DIM], sel)
        out_ref[i] = sel.astype(out_ref.dtype)


def _swa_sample(q, kc_all, vc_all, kn, vn, bias, sinks, *, mixer):
    b = q.shape[0]
    bb = SAMPLE_BB
    blk = lambda shape: pl.BlockSpec((bb,) + shape, lambda i: (i, 0, 0))
    cache = pl.BlockSpec((None, bb, WINDOW, KV_DIM), lambda i: (mixer, i, 0, 0))
    return pl.pallas_call(
        _swa_sample_body,
        grid=(b // bb,),
        in_specs=[blk((Q_HEADS, HEAD_DIM)), cache, cache, blk((1, KV_DIM)), blk((1, KV_DIM)), _resident(bias.shape),
                  _layer(sinks, mixer)],
        out_specs=[blk((Q_HEADS, HEAD_DIM)), cache, cache],
        out_shape=[jax.ShapeDtypeStruct((b, Q_HEADS, HEAD_DIM), BF16), jax.ShapeDtypeStruct(kc_all.shape, F32),
                   jax.ShapeDtypeStruct(vc_all.shape, F32)],
        input_output_aliases={1: 1, 2: 2},
        compiler_params=_params(1), name="swa_sample",
    )(q, kc_all, vc_all, kn, vn, bias, sinks)


def kernel(x_prompt, x_sample, state_mlstm_C, state_mlstm_n, state_mlstm_m, cache_swa_k, cache_swa_v, p_prompt, p_sample, rel_bias, norm_pre_mix, norm_post_mix, norm_pre_ffn, norm_post_ffn, mlstm_w_in, mlstm_b_i, mlstm_b_f, mlstm_norm, mlstm_w_out, swa_w_qkv, swa_sinks, swa_w_o, ffn_w_up, ffn_w_down, ple_w_proj, ple_norm, ple_w_gate):
    bp, sp, _ = x_prompt.shape
    bs = x_sample.shape[0]
    win = cache_swa_k.shape[2]
    assert win == WINDOW and x_sample.shape[1] == 1
    bf = lambda a: a.astype(BF16)
    vecs = lambda a: a.astype(F32)[:, None, :]

    post_w = dict(npm=vecs(norm_post_mix), npf=vecs(norm_pre_ffn), wup=bf(ffn_w_up), wdn=bf(ffn_w_down),
                  npff=vecs(norm_post_ffn), wpj=bf(ple_w_proj), pln=vecs(ple_norm), wgate=bf(ple_w_gate))
    n_pre = vecs(norm_pre_mix)
    m_w = bf(mlstm_w_in)
    m_bias = jnp.concatenate([mlstm_b_i, mlstm_b_f], axis=1).astype(F32)
    m_wout, m_norm = bf(mlstm_w_out), vecs(mlstm_norm)
    s_w, s_wo = bf(swa_w_qkv), bf(swa_w_o)
    sinks = swa_sinks.astype(F32)

    qi = jnp.arange(WINDOW)[:, None]
    kj = jnp.arange(WINDOW)[None, :]
    dist_p = jnp.where(kj > qi, qi + WINDOW - kj, qi - kj)
    bias_p = _bias_table(_rel_bucket(dist_p).astype(jnp.int32), rel_bias.astype(F32))
    dist_s = (win - 1) - jnp.arange(win)
    bias_s = _bias_table(jnp.broadcast_to(_rel_bucket(dist_s)[None, :], (8, win)).astype(jnp.int32),
                         rel_bias.astype(F32))[:, 0, :]
    bias_s = jnp.tile(bias_s, (SAMPLE_BB, 1))
    sinks_s = jnp.tile(sinks[:, :, None], (1, SAMPLE_BB, 1))

    hp = x_prompt.reshape(bp * sp, D_MODEL)
    hs = x_sample.reshape(bs, D_MODEL)
    pp = p_prompt.reshape(DEPTH, bp * sp, PLE_DIM)
    ps = p_sample.reshape(DEPTH, bs, PLE_DIM)
    c_old = state_mlstm_C.astype(F32).reshape(-1, bs, HK, DV)
    k_win = cache_swa_k.astype(F32).reshape(-1, bs, win, KV_DIM)
    v_win = cache_swa_v.astype(F32).reshape(-1, bs, win, KV_DIM)
    c_new = None
    outs = dict(Cp=[], np=[], mp=[], kp=[], vp=[], ns=[], ms=[])

    for i in range(DEPTH):
        j = i // N_MIXERS
        if i % N_MIXERS == 0:
            wo = m_wout
            q, kt, v, o, gt = _mlstm_in(hp, n_pre, m_w, m_bias[j].reshape(-1, 1),
                                        layer=i, mixer=j, prompt=True, act_dtype=BF16)
            hp, c_fin, n_fin, m_fin = _mlstm_post(q, kt, v, o, gt, m_norm, hp, pp, wo, post_w,
                                                  layer=i, mixer=j, batch=bp, seq=sp)
            outs["Cp"].append(c_fin)
            outs["np"].append(n_fin[..., 0])
            outs["mp"].append(m_fin[..., 0])
            q, k, v, o, g = _mlstm_in(hs, n_pre, m_w, m_bias[j].reshape(1, -1),
                                      layer=i, mixer=j, prompt=False, act_dtype=F32)
            a_s, c_new, n_new, m_new = _mlstm_step(
                c_old, c_new, state_mlstm_n[j].astype(F32), state_mlstm_m[j].astype(F32),
                q, k, v, o, g, m_norm, mixer=j)
            outs["ns"].append(n_new)
            outs["ms"].append(m_new)
        else:
            wo = s_wo
            q, k, kt, v = _swa_in(hp, n_pre, s_w, layer=i, mixer=j)
            hp = _swa_post(q, kt, v, bias_p, sinks, hp, pp, wo, post_w, layer=i, mixer=j, batch=bp, seq=sp)
            last = lambda a: a.reshape(bp, sp, KV_DIM)[:, sp - win:].reshape(bp, win, KV_HEADS, HEAD_DIM)
            outs["kp"].append(last(k))
            outs["vp"].append(last(v))
            q, k, _, v = _swa_in(hs, n_pre, s_w, layer=i, mixer=j)
            a_s, k_win, v_win = _swa_sample(
                q.reshape(bs, Q_HEADS, HEAD_DIM), k_win, v_win, k.reshape(bs, 1, KV_DIM), v.reshape(bs, 1, KV_DIM),
                bias_s, sinks_s, mixer=j)
            a_s = a_s.reshape(bs, D_MODEL)
        hs = _post(a_s, hs, ps, wo, post_w, layer=i, mixer=j)

    st = lambda key, like: jnp.stack(outs[key]).astype(like.dtype)
    kv_out = lambda a, like: a.reshape(-1, bs, win, KV_HEADS, HEAD_DIM).astype(like.dtype)
    return (hp.reshape(x_prompt.shape), hs.reshape(x_sample.shape),
            st("Cp", state_mlstm_C), st("np", state_mlstm_n), st("mp", state_mlstm_m),
            st("kp", cache_swa_k), st("vp", cache_swa_v),
            c_new.reshape(state_mlstm_C.shape).astype(state_mlstm_C.dtype), st("ns", state_mlstm_n),
            st("ms", state_mlstm_m), kv_out(k_win, cache_swa_k), kv_out(v_win, cache_swa_v))
```

```python
import functools
import math

import jax
import jax.numpy as jnp
from jax import lax
from jax.experimental import pallas as pl
from jax.experimental.pallas import tpu as pltpu

F32 = jnp.float32
BF16 = jnp.bfloat16

D_MODEL = 1024
DEPTH = 4
N_MIXERS = 2
HEADS = 8
DQK = 64
DV = 128
CHUNK = 128
Q_HEADS = 16
KV_HEADS = 4
GROUP = Q_HEADS // KV_HEADS
HEAD_DIM = 64
KV_DIM = KV_HEADS * HEAD_DIM
WINDOW = 128
REL_BUCKETS = 32
REL_MAX_DIST = 128
PLE_DIM = 256
D_FF = 4 * D_MODEL
EPS = 1e-6
NEG = -1e30

HK, HV = HEADS * DQK, HEADS * DV
Q0, K0, V0, O0, G0 = 0, HK, 2 * HK, 2 * HK + HV, 2 * HK + HV + D_MODEL
MLSTM_IN = G0 + 2 * HEADS

V7X_VMEM_BYTES = 64 * 1024 * 1024
VMEM_LIMIT = V7X_VMEM_BYTES - 8 * 1024 * 1024
LANES = 128
ROW_TILE = 512

_NT = (((1,), (1,)), ((), ()))


def _row_tile(t):
    return ROW_TILE if t % ROW_TILE == 0 else t


def _params(n_grid):
    return pltpu.CompilerParams(dimension_semantics=("arbitrary",) * n_grid, vmem_limit_bytes=VMEM_LIMIT)


def _resident(shape):
    nd = len(shape)
    return pl.BlockSpec(shape, lambda *_: (0,) * nd, pipeline_mode=pl.Buffered(1))


def _layer(arr, layer):
    tail = arr.shape[1:]
    return pl.BlockSpec((None,) + tail, lambda *_: (layer,) + (0,) * len(tail), pipeline_mode=pl.Buffered(1))


def _skip_ref(body, pos):
    def wrapped(*refs):
        return body(*refs[:pos], *refs[pos + 1:])
    return wrapped


def _rms(x, w=None):
    y = x * lax.rsqrt(jnp.mean(x * x, axis=-1, keepdims=True) + EPS)
    return y if w is None else y * w


def _log_sigmoid(x):
    return -(jnp.maximum(-x, 0.0) + jnp.log1p(jnp.exp(-jnp.abs(x))))


def _sigmoid(x):
    return 1.0 / (1.0 + jnp.exp(-x))


def _dot(a, b):
    return jnp.dot(a, b, preferred_element_type=F32)


def _mlstm_in_body(h_ref, nw_ref, w_ref, bg_ref, hnw_ref, q_ref, k_ref, v_ref, o_ref, g_ref, *scratch, prompt):
    if prompt:
        wkgt_scr, = scratch

        @pl.when(pl.program_id(0) == 0)
        def _():
            wkgt_scr[0:HK, :] = w_ref[:, K0:V0].astype(F32).T.astype(BF16)
            wg = jnp.concatenate([w_ref[:, G0:MLSTM_IN].astype(F32),
                                  jnp.zeros((D_MODEL, LANES - 2 * HEADS), F32)], axis=1)
            wkgt_scr[HK:, :] = wg.T.astype(BF16)

    u = _rms(h_ref[...], nw_ref[...]).astype(BF16)
    q_ref[...] = (_dot(u, w_ref[:, Q0:K0]) * (DQK ** -0.5)).astype(q_ref.dtype)
    v_ref[...] = _dot(u, w_ref[:, V0:O0]).astype(v_ref.dtype)
    o_ref[...] = _sigmoid(_dot(u, w_ref[:, O0:G0])) * hnw_ref[...]
    if prompt:
        kgt =lax.dot_general(wkgt_scr[...], u, _NT, preferred_element_type=F32)
        k_ref[...] = kgt[:HK].astype(k_ref.dtype)
        gt = kgt[HK:HK + 2 * HEADS] + bg_ref[...]
        ig, lf = gt[:HEADS], _log_sigmoid(gt[HEADS:])
        tm = lf.shape[1]
        r = lax.broadcasted_iota(jnp.int32, (tm, tm), 0)
        t = lax.broadcasted_iota(jnp.int32, (tm, tm), 1)
        same_chunk_upper = ((r <= t) & (r // CHUNK == t // CHUNK)).astype(F32)
        b = jnp.dot(lf, same_chunk_upper, precision=lax.Precision.HIGHEST, preferred_element_type=F32)
        g_ref[...] = jnp.concatenate([ig, lf, b], axis=0)
    else:
        k_ref[...] = _dot(u, w_ref[:, K0:V0]).astype(k_ref.dtype)
        g = _dot(u, w_ref[:, G0:MLSTM_IN]) + bg_ref[...]
        lane = lax.broadcasted_iota(jnp.int32, g.shape, 1)
        g_ref[...] = jnp.where(lane < HEADS, g, _log_sigmoid(g))


def _mlstm_in(h, nw, w, bg, hnw, *, layer, mixer, prompt, act_dtype):
    t = h.shape[0]
    tm = _row_tile(t)
    row = lambda n: pl.BlockSpec((tm, n), lambda i: (i, 0))
    col = lambda n: pl.BlockSpec((n, tm), lambda i: (0, i))
    kshape, kspec = ((HK, t), col(HK)) if prompt else ((t, HK), row(HK))
    gshape, gspec = ((3 * HEADS, t), col(3 * HEADS)) if prompt else ((t, 2 * HEADS), row(2 * HEADS))
    return pl.pallas_call(
        functools.partial(_mlstm_in_body, prompt=prompt),
        grid=(t // tm,),
        in_specs=[row(D_MODEL), _layer(nw, layer), _layer(w, mixer), _resident(bg.shape), _layer(hnw, mixer)],
        out_specs=[row(HK), kspec, row(HV), row(D_MODEL), gspec],
        out_shape=[jax.ShapeDtypeStruct((t, HK), act_dtype), jax.ShapeDtypeStruct(kshape, act_dtype),
                   jax.ShapeDtypeStruct((t, HV), act_dtype), jax.ShapeDtypeStruct((t, D_MODEL), F32),
                   jax.ShapeDtypeStruct(gshape, F32)],
        scratch_shapes=[pltpu.VMEM((HK + LANES, D_MODEL), BF16)] if prompt else [],
        compiler_params=_params(1), name="mlstm_in",
    )(h, nw, w, bg, hnw)


def _swa_in_body(h_ref, nw_ref, w_ref, q_ref, k_ref, kt_ref, v_ref):
    u = _rms(h_ref[...], nw_ref[...]).astype(BF16)
    q_ref[...] = (_dot(u, w_ref[:, :D_MODEL]) * (HEAD_DIM ** -0.5)).astype(q_ref.dtype)
    k = _dot(u, w_ref[:, D_MODEL:D_MODEL + KV_DIM])
    k_ref[...] = k
    kt_ref[...] = k.T.astype(kt_ref.dtype)
    v_ref[...] = _dot(u, w_ref[:, D_MODEL + KV_DIM:])


def _swa_in(h, nw, w, *, layer, mixer):
    t = h.shape[0]
    tm = _row_tile(t)
    row = lambda n: pl.BlockSpec((tm, n), lambda i: (i, 0))
    return pl.pallas_call(
        _swa_in_body,
        grid=(t // tm,),
        in_specs=[row(D_MODEL), _layer(nw, layer), _layer(w, mixer)],
        out_specs=[row(D_MODEL), row(KV_DIM), pl.BlockSpec((KV_DIM, tm), lambda i: (0, i)), row(KV_DIM)],
        out_shape=[jax.ShapeDtypeStruct((t, D_MODEL), BF16), jax.ShapeDtypeStruct((t, KV_DIM), F32),
                   jax.ShapeDtypeStruct((KV_DIM, t), BF16), jax.ShapeDtypeStruct((t, KV_DIM), F32)],
        compiler_params=_params(1), name="swa_in",
    )(h, nw, w)


FF_CHUNK = 1024


def _post_rows(a, h, p, wo_ref, npm_ref, npf_ref, wup_ref, wdn_ref, npff_ref, wpj_ref, pln_ref, wgate_ref):
    y = _dot(a.astype(BF16), wo_ref[...])
    h1 = h + _rms(y, npm_ref[...])
    u = _rms(h1, npf_ref[...]).astype(BF16)
    acc = jnp.zeros(h1.shape, F32)
    for c in range(D_FF // FF_CHUNK):
        hid = jnp.maximum(_dot(u, wup_ref[:, c * FF_CHUNK:(c + 1) * FF_CHUNK]), 0.0)
        acc = acc + _dot((hid * hid).astype(BF16), wdn_ref[c * FF_CHUNK:(c + 1) * FF_CHUNK, :])
    h2 = h1 + _rms(acc, npff_ref[...])
    e = _rms(_dot(p.astype(BF16), wpj_ref[...]), pln_ref[...])
    g = _sigmoid(_dot(_rms(h2).astype(BF16), wgate_ref[...]))
    return h2 + e * g


def _post_body(a_ref, h_ref, p_ref, *rest):
    w_refs, out_ref = rest[:-1], rest[-1]
    out_ref[...] = _post_rows(a_ref[...], h_ref[...], p_ref[...], *w_refs)


_POST_KEYS = ("npm", "npf", "wup", "wdn", "npff", "wpj", "pln", "wgate")


def _post_weights(wo, w, layer, mixer):
    return ([_layer(wo, mixer)] + [_layer(w[k], layer) for k in _POST_KEYS], [wo] + [w[k] for k in _POST_KEYS])


def _post(a, h, p, wo, w, *, layer, mixer):
    t = h.shape[0]
    tm = _row_tile(t)
    row = lambda n: pl.BlockSpec((tm, n), lambda i: (i, 0))
    w_specs, w_args = _post_weights(wo, w, layer, mixer)
    return pl.pallas_call(
        _post_body,
        grid=(t // tm,),
        in_specs=[row(D_MODEL), row(D_MODEL), pl.BlockSpec((None, tm, PLE_DIM), lambda i: (layer, i, 0))] + w_specs,
        out_specs=row(D_MODEL),
        out_shape=jax.ShapeDtypeStruct((t, D_MODEL), F32),
        compiler_params=_params(1), name="post",
    )(a, h, p, *w_args)


def _pipelined_specs(tm, nt, layer):
    mix_i = lambda s: jnp.minimum(s, nt - 1)
    post_i = lambda s: jnp.maximum(s - 1, 0)
    mix_row = lambda n: pl.BlockSpec((tm, n), lambda s: (mix_i(s), 0))
    mix_col = lambda n: pl.BlockSpec((n, tm), lambda s: (0, mix_i(s)))
    post_row = lambda n: pl.BlockSpec((tm, n), lambda s: (post_i(s), 0))
    p_spec = pl.BlockSpec((None, tm, PLE_DIM), lambda s: (layer, post_i(s), 0))
    return mix_i, mix_row, mix_col, post_row, p_spec


STATE_W = 2 * DV


def _scan_chunk(q_ref, kt_ref, v_ref, og_ref, gt_ref, hg_ref, cn_scr, m_scr, r0):
    L = CHUNK
    rows = slice(r0, r0 + L)
    ri = lax.broadcasted_iota(jnp.int32, (L, L), 0)
    ci = lax.broadcasted_iota(jnp.int32, (L, L), 1)
    tri = ci <= ri
    gt = gt_ref[:, rows]
    ig_rows, lf_rows, b_rows = gt[:HEADS], gt[HEADS:2 * HEADS], gt[2 * HEADS:]
    a_rows = ig_rows - b_rows
    m_prev = m_scr[...]
    b_last = jnp.broadcast_to(b_rows[:, L - 1:L], (HEADS, L))
    m_new = b_last + jnp.maximum(m_prev, jnp.max(a_rows, axis=1, keepdims=True))
    decay_rows = jnp.exp(b_last + m_prev - m_new)
    wk_rows = jnp.exp(b_last - b_rows + ig_rows - m_new)
    m_scr[...] = m_new

    ones_rhs = jnp.ones((L, DV), BF16)
    zeros_kt = jnp.zeros((DQK, L), BF16)
    zeros_cn = jnp.zeros((DQK, STATE_W), BF16)
    for j in range(HEADS // 2):
        q_pair = q_ref[rows, j * 128:(j + 1) * 128]
        kt_pair = [kt_ref[h * DQK:(h + 1) * DQK, rows] for h in (2 * j, 2 * j + 1)]
        s_pair = _dot(q_pair, jnp.concatenate([jnp.concatenate([kt_pair[0], zeros_kt], axis=0),
                                               jnp.concatenate([zeros_kt, kt_pair[1]], axis=0)], axis=1))
        for par in range(2):
            h = 2 * j + par
            kt_h = kt_pair[par]
            v_ext = jnp.concatenate([v_ref[rows, h * DV:(h + 1) * DV], ones_rhs], axis=1)
            cn = cn_scr[h]
            cn16 = cn.astype(BF16)
            a_row, m_row = a_rows[h:h + 1], m_prev[h:h + 1]

            big_m = jnp.maximum(jnp.max(jnp.where(tri, a_row, -jnp.inf), axis=1, keepdims=True), m_row)
            b_col = jnp.sum(jnp.where(tri, lf_rows[h:h + 1], 0.0), axis=1, keepdims=True)
            w_inter = jnp.exp(m_row - big_m)
            s = s_pair[:, par * L:(par + 1) * L] * jnp.exp(jnp.where(tri, a_row - big_m, NEG))
            q_w = (q_pair.astype(F32) * w_inter).astype(BF16)
            state = jnp.concatenate([cn16, zeros_cn] if par == 0 else [zeros_cn, cn16], axis=0)
            nd = _dot(jnp.concatenate([s.astype(BF16), q_w], axis=1), jnp.concatenate([v_ext, state], axis=0))
            hh = nd[:, :DV] * (1.0 / jnp.maximum(jnp.abs(nd[:, DV:]), jnp.exp(-(b_col + big_m))))

            kw_t = kt_h.astype(F32) * wk_rows[h:h + 1]
            dec = decay_rows[h:h + 1]
            cn_scr[h] = jnp.concatenate([dec, dec], axis=1) * cn + _dot(kw_t.astype(BF16), v_ext)

            hg_ref[rows, h * DV:(h + 1) * DV] = (_rms(hh) * og_ref[rows, h * DV:(h + 1) * DV]).astype(hg_ref.dtype)


def _mlstm_post_body(q_ref, kt_ref, v_ref, og_ref, gt_ref, h_ref, p_ref, *rest, tiles_per_seq):
    w_refs = rest[:-7]
    out_ref, c_out_ref, n_out_ref, m_out_ref, hg_scr, cn_scr, m_scr = rest[-7:]
    s = pl.program_id(0)
    nt = pl.num_programs(0) - 1
    tile = jnp.minimum(s, nt - 1)

    @pl.when(s == 0)
    def _():
        hg_scr[...] = jnp.zeros(hg_scr.shape, hg_scr.dtype)

    @pl.when(tile % tiles_per_seq == 0)
    def _():
        cn_scr[...] = jnp.zeros(cn_scr.shape, F32)
        m_scr[...] = jnp.zeros(m_scr.shape, F32)

    out_ref[...] = _post_rows(hg_scr[(s + 1) % 2], h_ref[...], p_ref[...], *w_refs)
    hg_slot = hg_scr.at[s % 2]
    for r0 in range(0, q_ref.shape[0], CHUNK):
        _scan_chunk(q_ref, kt_ref, v_ref, og_ref, gt_ref, hg_slot, cn_scr, m_scr, r0)

    @pl.when((tile % tiles_per_seq == tiles_per_seq - 1) & (s < nt))
    def _():
        for h in range(HEADS):
            c_out_ref[0, h] = cn_scr[h, :, 0:DV]
            n_out_ref[0, h] = cn_scr[h, :, DV:STATE_W]
        m_out_ref[0] = m_scr[...]


def _mlstm_post(q, kt, v, og, gt, h, p, wo, w, *, layer, mixer, batch, seq):
    t = batch * seq
    tm = ROW_TILE
    nt = t // tm
    tiles_per_seq = seq // tm
    mix_i, mix_row, mix_col, post_row, p_spec = _pipelined_specs(tm, nt, layer)
    seq_i = lambda s: mix_i(s) // tiles_per_seq
    w_specs, w_args = _post_weights(wo, w, layer, mixer)
    return pl.pallas_call(
        functools.partial(_mlstm_post_body, tiles_per_seq=tiles_per_seq),
        grid=(nt + 1,),
        in_specs=[mix_row(HK), mix_col(HK), mix_row(HV), mix_row(D_MODEL), mix_col(3 * HEADS),
                  post_row(D_MODEL), p_spec] + w_specs,
        out_specs=[post_row(D_MODEL),
                   pl.BlockSpec((1, HEADS, DQK, DV), lambda s: (seq_i(s), 0, 0, 0)),
                   pl.BlockSpec((1, HEADS, DQK, STATE_W - DV), lambda s: (seq_i(s), 0, 0, 0)),
                   pl.BlockSpec((1, HEADS, LANES), lambda s: (seq_i(s), 0, 0))],
        out_shape=[jax.ShapeDtypeStruct((t, D_MODEL), F32),
                   jax.ShapeDtypeStruct((batch, HEADS, DQK, DV), F32),
                   jax.ShapeDtypeStruct((batch, HEADS, DQK, STATE_W - DV), F32),
                   jax.ShapeDtypeStruct((batch, HEADS, LANES), F32)],
        scratch_shapes=[pltpu.VMEM((2, tm, HV), BF16), pltpu.VMEM((HEADS, DQK, STATE_W), F32),
                        pltpu.VMEM((HEADS, LANES), F32)],
        compiler_params=_params(1), name="mlstm_post",
    )(q, kt, v, og, gt, h, p, *w_args)


STEP_BB = 8
_TN = (((0,), (0,)), ((), ()))


def _mlstm_step_body(c_ref, n_ref, m_ref, q_ref, k_ref, v_ref, og_ref, g_ref,
                     hg_ref, c_out_ref, n_out_ref, m_out_ref, *, own):
    bb = STEP_BB
    for slab in range(c_out_ref.shape[0]):
        if slab != own:
            c_out_ref[slab] = jnp.zeros(c_out_ref.shape[1:], F32)
    g = g_ref[...]
    ig, lf, m_prev = g[:, :HEADS], g[:, HEADS:], m_ref[...]
    inter = lf + m_prev
    m_t = jnp.maximum(inter, ig)
    w_inter = jnp.exp(inter - m_t)
    in_w = jnp.exp(ig - m_t)
    decay = jnp.exp(lf + m_prev - m_t)
    floor = jnp.exp(-m_t)
    m_out_ref[...] = m_t
    row_k = lax.broadcasted_iota(jnp.int32, (bb, DQK), 0)
    row_v = lax.broadcasted_iota(jnp.int32, (bb, DV), 0)

    for h in range(HEADS):
        col = lambda a: a[:, h:h + 1]
        q = q_ref[:, h * DQK:(h + 1) * DQK]
        k = k_ref[:, h * DQK:(h + 1) * DQK]
        v = v_ref[:, h * DV:(h + 1) * DV]
        n = n_ref[:, h, :]
        s = jnp.sum(q * k, axis=1, keepdims=True) * col(in_w)
        kw = k * col(in_w)
        q16, v16 = q.astype(BF16), v.astype(BF16)
        qc = jnp.zeros((bb, DV), F32)
        for i in range(bb):
            c_i = c_ref[i, h * DQK:(h + 1) * DQK, :]
            qc = jnp.where(row_v == i, _dot(q16, c_i.astype(BF16)), qc)
            outer = lax.dot_general(jnp.where(row_k == i, kw, 0.0).astype(BF16), v16, _TN,
                                    preferred_element_type=F32)
            c_out_ref[own, i, h * DQK:(h + 1) * DQK, :] = decay[i:i + 1, h:h + 1] * c_i + outer
        num = col(w_inter) * qc + s * v
        den = col(w_inter) * jnp.sum(q * n, axis=1, keepdims=True) + s
        hh = num * (1.0 / jnp.maximum(jnp.abs(den), col(floor)))
        n_out_ref[:, h, :] = col(decay) * n + kw
        hg_ref[:, h * DV:(h + 1) * DV] = _rms(hh) * og_ref[:, h * DV:(h + 1) * DV]


def _mlstm_step(c_all, c_new_prev, n, m, q, k, v, og, g, *, mixer):
    n_slabs, b = c_all.shape[:2]
    bb = STEP_BB
    rows = lambda width: pl.BlockSpec((bb, width), lambda i: (i, 0))
    cspec = pl.BlockSpec((None, bb, HK, DV), lambda i: (mixer, i, 0, 0))
    nspec = pl.BlockSpec((bb, HEADS, DQK), lambda i: (i, 0, 0))
    in_specs = [cspec, nspec, rows(HEADS), rows(HK), rows(HK), rows(HV), rows(D_MODEL), rows(2 * HEADS)]
    args = [c_all, n, m, q, k, v, og, g]
    if c_new_prev is None:
        c_out_spec = pl.BlockSpec((n_slabs, bb, HK, DV), lambda i: (0, i, 0, 0))
        body, aliases = functools.partial(_mlstm_step_body, own=mixer), {}
    else:
        c_out_spec = pl.BlockSpec((1, bb, HK, DV), lambda i: (mixer, i, 0, 0))
        in_specs.append(pl.BlockSpec(memory_space=pl.ANY))
        body, aliases = _skip_ref(functools.partial(_mlstm_step_body, own=0), len(args)), {len(args): 1}
        args.append(c_new_prev)
    return pl.pallas_call(
        body,
        grid=(b // bb,),
        in_specs=in_specs,
        out_specs=[rows(HV), c_out_spec, nspec, rows(HEADS)],
        out_shape=[jax.ShapeDtypeStruct((b, HV), F32), jax.ShapeDtypeStruct(c_all.shape, F32),
                   jax.ShapeDtypeStruct(n.shape, F32), jax.ShapeDtypeStruct((b, HEADS), F32)],
        input_output_aliases=aliases,
        compiler_params=_params(1), name="mlstm_step",
    )(*args)


def _bias_body(bucket_ref, rel_ref, out_ref):
    bucket = bucket_ref[...]
    for h in range(Q_HEADS):
        acc = jnp.zeros(bucket.shape, F32)
        for b in range(REL_BUCKETS):
            acc = jnp.where(bucket == b, rel_ref[b, h], acc)
        out_ref[h] = acc


def _bias_table(bucket, rel_bias):
    shape = bucket.shape
    return pl.pallas_call(
        _bias_body,
        in_specs=[pl.BlockSpec(memory_space=pltpu.VMEM), pl.BlockSpec(memory_space=pltpu.SMEM)],
        out_specs=pl.BlockSpec(memory_space=pltpu.VMEM),
        out_shape=jax.ShapeDtypeStruct((Q_HEADS,) + shape, F32),
        name="rel_bias_table",
    )(bucket, rel_bias)


def _rel_bucket(dist):
    max_exact = REL_BUCKETS // 2
    d = jnp.maximum(dist, 0)
    df = jnp.maximum(d, 1).astype(F32)
    large = max_exact + (jnp.log(df / max_exact) / math.log(REL_MAX_DIST / max_exact)
                         * (REL_BUCKETS - max_exact)).astype(jnp.int32)
    large = jnp.minimum(large, REL_BUCKETS - 1)
    return jnp.where(d < max_exact, d, large)


def _swa_block(q, kt_prev, kt_cur, v_prev, v_cur, bias_ref, sink_ref, in_range, out_ref, rows):
    blk = WINDOW
    qi = lax.broadcasted_iota(jnp.int32, (blk, blk), 0)
    kj = lax.broadcasted_iota(jnp.int32, (blk, blk), 1)
    from_prev = kj > qi
    lane = lax.broadcasted_iota(jnp.int32, (2 * blk, LANES), 1)
    lane_q = lax.broadcasted_iota(jnp.int32, (blk, LANES), 1)
    zero_k = jnp.zeros((HEAD_DIM, 2 * blk), BF16)
    ones_bd = jnp.concatenate([(lane < HEAD_DIM).astype(BF16), (lane >= HEAD_DIM).astype(BF16)], axis=0)

    for g in range(KV_HEADS):
        hd_rows = slice(g * HEAD_DIM, (g + 1) * HEAD_DIM)
        kt_g = jnp.concatenate([kt_prev[hd_rows, :], kt_cur[hd_rows, :]], axis=1)
        rhs = jnp.concatenate([jnp.concatenate([kt_g, zero_k], axis=0),
                               jnp.concatenate([zero_k, kt_g], axis=0)], axis=1)
        v_lanes = slice((g // 2) * LANES, (g // 2 + 1) * LANES)
        vt = jnp.concatenate([v_prev[:, v_lanes], v_cur[:, v_lanes]], axis=0)
        vr = pltpu.roll(vt, HEAD_DIM, axis=1)
        v_lo = jnp.where(lane < HEAD_DIM, vt if g % 2 == 0 else vr, 0.0)
        v_hi = jnp.where(lane >= HEAD_DIM, vr if g % 2 == 0 else vt, 0.0)
        v_bd = jnp.concatenate([jnp.concatenate([v_lo, v_hi], axis=0).astype(BF16), ones_bd], axis=1)
        for pair in range(GROUP // 2):
            jp = g * (GROUP // 2) + pair
            logits = _dot(q[:, jp * LANES:(jp + 1) * LANES], rhs)
            ps, sk = [], []
            for i in range(2):
                hd = 2 * jp + i
                lg = jnp.where(from_prev, logits[:, 2 * i * blk:(2 * i + 1) * blk],
                               logits[:, (2 * i + 1) * blk:(2 * i + 2) * blk]) + bias_ref[hd]
                if in_range is not None:
                    lg = jnp.where(in_range, lg, NEG)
                sink = sink_ref[hd]
                mx = jnp.maximum(jnp.max(lg, axis=1, keepdims=True), sink)
                p = jnp.exp(lg - mx)
                ps += [jnp.where(from_prev, p, 0.0).astype(BF16), jnp.where(from_prev, 0.0, p).astype(BF16)]
                sk.append(jnp.exp(sink - mx))
            pv = _dot(jnp.concatenate(ps, axis=1), v_bd)
            den = pv[:, LANES:] + jnp.where(lane_q < HEAD_DIM, sk[0], sk[1])
            out_ref[rows, jp * LANES:(jp + 1) * LANES] = (pv[:, :LANES] * (1.0 / den)).astype(out_ref.dtype)


def _swa_post_body(q_ref, kt_ref, ktp_ref, v_ref, vp_ref, bias_ref, sink_ref, h_ref, p_ref, *rest, tiles_per_seq):
    w_refs, out_ref, attn_scr = rest[:-2], rest[-2], rest[-1]
    s = pl.program_id(0)
    nt = pl.num_programs(0) - 1
    tile = jnp.minimum(s, nt - 1)

    @pl.when(s == 0)
    def _():
        attn_scr[...] = jnp.zeros(attn_scr.shape, attn_scr.dtype)

    out_ref[...] = _post_rows(attn_scr[(s + 1) % 2], h_ref[...], p_ref[...], *w_refs)
    attn_slot = attn_scr.at[s % 2]
    blk = WINDOW
    qi = lax.broadcasted_iota(jnp.int32, (blk, blk), 0)
    kj = lax.broadcasted_iota(jnp.int32, (blk, blk), 1)
    has_prev = (kj <= qi) | (tile % tiles_per_seq != 0)
    for b in range(q_ref.shape[0] // blk):
        rows, prev_rows = slice(b * blk, (b + 1) * blk), slice((b - 1) * blk, b * blk)
        _swa_block(q_ref[rows, :],
                   ktp_ref[...] if b == 0 else kt_ref[:, prev_rows], kt_ref[:, rows],
                   vp_ref[...] if b == 0 else v_ref[prev_rows, :], v_ref[rows, :],
                   bias_ref, sink_ref, has_prev if b == 0 else None, attn_slot, rows)


def _swa_post(q, kt, v, bias, sinks, h, p, wo, w, *, layer, mixer, batch, seq):
    t = batch * seq
    tm = ROW_TILE
    nt = t // tm
    blocks = tm // WINDOW
    mix_i, mix_row, mix_col, post_row, p_spec = _pipelined_specs(tm, nt, layer)
    prev_blk = lambda s: jnp.maximum(mix_i(s) * blocks - 1, 0)
    w_specs, w_args = _post_weights(wo, w, layer, mixer)
    return pl.pallas_call(
        functools.partial(_swa_post_body, tiles_per_seq=seq // tm),
        grid=(nt + 1,),
        in_specs=[mix_row(D_MODEL), mix_col(KV_DIM), pl.BlockSpec((KV_DIM, WINDOW), lambda s: (0, prev_blk(s))),
                  mix_row(KV_DIM), pl.BlockSpec((WINDOW, KV_DIM), lambda s: (prev_blk(s), 0)),
                  _resident(bias.shape), pl.BlockSpec(memory_space=pltpu.SMEM), post_row(D_MODEL), p_spec] + w_specs,
        out_specs=post_row(D_MODEL),
        out_shape=jax.ShapeDtypeStruct((t, D_MODEL), F32),
        scratch_shapes=[pltpu.VMEM((2, tm, D_MODEL), BF16)],
        compiler_params=_params(1), name="swa_post",
    )(q, kt, kt, v, v, bias, sinks[mixer], h, p, *w_args)


SAMPLE_BB = 8


def _swa_sample_body(q_ref, kc_ref, vc_ref, kn_ref, vn_ref, bias_ref, sink_ref, out_ref, ko_ref, vo_ref):
    w = WINDOW
    row = lax.broadcasted_iota(jnp.int32, (w, KV_DIM), 0)
    rg = lax.broadcasted_iota(jnp.int32, (Q_HEADS, KV_DIM), 0) // GROUP
    lg = lax.broadcasted_iota(jnp.int32, (Q_HEADS, KV_DIM), 1) // HEAD_DIM
    rsel = lax.broadcasted_iota(jnp.int32, (Q_HEADS, HEAD_DIM), 0) // GROUP
    logits, v_wins = [], []
    for i in range(SAMPLE_BB):
        k_win = jnp.where(row == w - 1, kn_ref[i], pltpu.roll(kc_ref[i], w - 1, axis=0))
        v_win = jnp.where(row == w - 1, vn_ref[i], pltpu.roll(vc_ref[i], w - 1, axis=0))
        ko_ref[i] = k_win
        vo_ref[i] = v_win
        qb = q_ref[i].astype(F32)
        q_bd = jnp.where(rg == lg, jnp.concatenate([qb] * KV_HEADS, axis=1), 0.0).astype(BF16)
        logits.append(lax.dot_general(q_bd, k_win.astype(BF16), _NT, preferred_element_type=F32))
        v_wins.append(v_win.astype(BF16))
    lg_all = jnp.concatenate(logits, axis=0) + bias_ref[...]
    sink = sink_ref[...]
    mx = jnp.maximum(jnp.max(lg_all, axis=1, keepdims=True), sink)
    p = jnp.exp(lg_all - mx)
    probs = (p * (1.0 / (jnp.sum(p, axis=1, keepdims=True) + jnp.exp(sink - mx)))).astype(BF16)
    for i in range(SAMPLE_BB):
        o = _dot(probs[i * Q_HEADS:(i + 1) * Q_HEADS], v_wins[i])
        sel = o[:, 0:HEAD_DIM]
        for g in range(1, KV_HEADS):
            sel = jnp.where(rsel == g, o[:, g * HEAD_DIM:(g + 1) * HEAD_DIM], sel)
        out_ref[i] = sel.astype(out_ref.dtype)


def _swa_sample(q, kc_all, vc_all, kn, vn, bias, sinks, *, mixer):
    b = q.shape[0]
    bb = SAMPLE_BB
    blk = lambda shape: pl.BlockSpec((bb,) + shape, lambda i: (i, 0, 0))
    cache = pl.BlockSpec((None, bb, WINDOW, KV_DIM), lambda i: (mixer, i, 0, 0))
    return pl.pallas_call(
        _swa_sample_body,
        grid=(b // bb,),
        in_specs=[blk((Q_HEADS, HEAD_DIM)), cache, cache, blk((1, KV_DIM)), blk((1, KV_DIM)), _resident(bias.shape),
                  _layer(sinks, mixer)],
        out_specs=[blk((Q_HEADS, HEAD_DIM)), cache, cache],
        out_shape=[jax.ShapeDtypeStruct((b, Q_HEADS, HEAD_DIM), BF16), jax.ShapeDtypeStruct(kc_all.shape, F32),
                   jax.ShapeDtypeStruct(vc_all.shape, F32)],
        input_output_aliases={1: 1, 2: 2},
        compiler_params=_params(1), name="swa_sample",
    )(q, kc_all, vc_all, kn, vn, bias, sinks)


def kernel(x_prompt, x_sample, state_mlstm_C, state_mlstm_n, state_mlstm_m, cache_swa_k, cache_swa_v, p_prompt, p_sample, rel_bias, norm_pre_mix, norm_post_mix, norm_pre_ffn, norm_post_ffn, mlstm_w_in, mlstm_b_i, mlstm_b_f, mlstm_norm, mlstm_w_out, swa_w_qkv, swa_sinks, swa_w_o, ffn_w_up, ffn_w_down, ple_w_proj, ple_norm, ple_w_gate):
    bp, sp, _ = x_prompt.shape
    bs = x_sample.shape[0]
    win = cache_swa_k.shape[2]
    assert win == WINDOW and x_sample.shape[1] == 1
    bf = lambda a: a.astype(BF16)
    vecs = lambda a: a.astype(F32)[:, None, :]

    post_w = dict(npm=vecs(norm_post_mix), npf=vecs(norm_pre_ffn), wup=bf(ffn_w_up), wdn=bf(ffn_w_down),
                  npff=vecs(norm_post_ffn), wpj=bf(ple_w_proj), pln=vecs(ple_norm), wgate=bf(ple_w_gate))
    n_pre = vecs(norm_pre_mix)
    m_w = bf(mlstm_w_in)
    m_bias = jnp.concatenate([mlstm_b_i, mlstm_b_f], axis=1).astype(F32)
    m_wout, m_norm = bf(mlstm_w_out), vecs(mlstm_norm)
    s_w, s_wo = bf(swa_w_qkv), bf(swa_w_o)
    sinks = swa_sinks.astype(F32)

    qi = jnp.arange(WINDOW)[:, None]
    kj = jnp.arange(WINDOW)[None, :]
    dist_p = jnp.where(kj > qi, qi + WINDOW - kj, qi - kj)
    bias_p = _bias_table(_rel_bucket(dist_p).astype(jnp.int32), rel_bias.astype(F32))
    dist_s = (win - 1) - jnp.arange(win)
    bias_s = _bias_table(jnp.broadcast_to(_rel_bucket(dist_s)[None, :], (8, win)).astype(jnp.int32),
                         rel_bias.astype(F32))[:, 0, :]
    bias_s = jnp.tile(bias_s, (SAMPLE_BB, 1))
    sinks_s = jnp.tile(sinks[:, :, None], (1, SAMPLE_BB, 1))

    hp = x_prompt.reshape(bp * sp, D_MODEL)
    hs = x_sample.reshape(bs, D_MODEL)
    pp = p_prompt.reshape(DEPTH, bp * sp, PLE_DIM)
    ps = p_sample.reshape(DEPTH, bs, PLE_DIM)
    c_old = state_mlstm_C.astype(F32).reshape(-1, bs, HK, DV)
    k_win = cache_swa_k.astype(F32).reshape(-1, bs, win, KV_DIM)
    v_win = cache_swa_v.astype(F32).reshape(-1, bs, win, KV_DIM)
    c_new = None
    outs = dict(Cp=[], np=[], mp=[], kp=[], vp=[], ns=[], ms=[])

    for i in range(DEPTH):
        j = i // N_MIXERS
        if i % N_MIXERS == 0:
            wo = m_wout
            q, kt, v, og, gt = _mlstm_in(hp, n_pre, m_w, m_bias[j].reshape(-1, 1), m_norm,
                                         layer=i, mixer=j, prompt=True, act_dtype=BF16)
            hp, c_fin, n_fin, m_fin = _mlstm_post(q, kt, v, og, gt, hp, pp, wo, post_w,
                                                  layer=i, mixer=j, batch=bp, seq=sp)
            outs["Cp"].append(c_fin)
            outs["np"].append(n_fin[..., 0])
            outs["mp"].append(m_fin[..., 0])
            q, k, v, og, g = _mlstm_in(hs, n_pre, m_w, m_bias[j].reshape(1, -1), m_norm,
                                       layer=i, mixer=j, prompt=False, act_dtype=F32)
            a_s, c_new, n_new, m_new = _mlstm_step(
                c_old, c_new, state_mlstm_n[j].astype(F32), state_mlstm_m[j].astype(F32),
                q, k, v, og, g, mixer=j)
            outs["ns"].append(n_new)
            outs["ms"].append(m_new)
        else:
            wo = s_wo
            q, k, kt, v = _swa_in(hp, n_pre, s_w, layer=i, mixer=j)
            hp = _swa_post(q, kt, v, bias_p, sinks, hp, pp, wo, post_w, layer=i, mixer=j, batch=bp, seq=sp)
            last = lambda a: a.reshape(bp, sp, KV_DIM)[:, sp - win:].reshape(bp, win, KV_HEADS, HEAD_DIM)
            outs["kp"].append(last(k))
            outs["vp"].append(last(v))
            q, k, _, v = _swa_in(hs, n_pre, s_w, layer=i, mixer=j)
            a_s, k_win, v_win = _swa_sample(
                q.reshape(bs, Q_HEADS, HEAD_DIM), k_win, v_win, k.reshape(bs, 1, KV_DIM), v.reshape(bs, 1, KV_DIM),
                bias_s, sinks_s, mixer=j)
            a_s = a_s.reshape(bs, D_MODEL)
        hs = _post(a_s, hs, ps, wo, post_w, layer=i, mixer=j)

    st = lambda key, like: jnp.stack(outs[key]).astype(like.dtype)
    kv_out = lambda a, like: a.reshape(-1, bs, win, KV_HEADS, HEAD_DIM).astype(like.dtype)
    return (hp.reshape(x_prompt.shape), hs.reshape(x_sample.shape),
            st("Cp", state_mlstm_C), st("np", state_mlstm_n), st("mp", state_mlstm_m),
            st("kp", cache_swa_k), st("vp", cache_swa_v),
            c_new.reshape(state_mlstm_C.shape).astype(state_mlstm_C.dtype), st("ns", state_mlstm_n),
            st("ms", state_mlstm_m), kv_out(k_win, cache_swa_k), kv_out(v_win, cache_swa_v))
```

```python
import functools
import math

import jax
import jax.numpy as jnp
from jax import lax
from jax.experimental import pallas as pl
from jax.experimental.pallas import tpu as pltpu

F32 = jnp.float32
BF16 = jnp.bfloat16

D_MODEL = 1024
DEPTH = 4
N_MIXERS = 2
HEADS = 8
DQK = 64
DV = 128
CHUNK = 128
Q_HEADS = 16
KV_HEADS = 4
GROUP = Q_HEADS // KV_HEADS
HEAD_DIM = 64
KV_DIM = KV_HEADS * HEAD_DIM
WINDOW = 128
REL_BUCKETS = 32
REL_MAX_DIST = 128
PLE_DIM = 256
D_FF = 4 * D_MODEL
EPS = 1e-6
NEG = -1e30

HK, HV = HEADS * DQK, HEADS * DV
Q0, K0, V0, O0, G0 = 0, HK, 2 * HK, 2 * HK + HV, 2 * HK + HV + D_MODEL
MLSTM_IN = G0 + 2 * HEADS

V7X_VMEM_BYTES = 64 * 1024 * 1024
VMEM_LIMIT = V7X_VMEM_BYTES - 8 * 1024 * 1024
LANES = 128
ROW_TILE = 512

_NT = (((1,), (1,)), ((), ()))


def _row_tile(t):
    return ROW_TILE if t % ROW_TILE == 0 else t


def _params(n_grid):
    return pltpu.CompilerParams(dimension_semantics=("arbitrary",) * n_grid, vmem_limit_bytes=VMEM_LIMIT)


def _resident(shape):
    nd = len(shape)
    return pl.BlockSpec(shape, lambda *_: (0,) * nd, pipeline_mode=pl.Buffered(1))


def _layer(arr, layer):
    tail = arr.shape[1:]
    return pl.BlockSpec((None,) + tail, lambda *_: (layer,) + (0,) * len(tail), pipeline_mode=pl.Buffered(1))


def _skip_ref(body, pos):
    def wrapped(*refs):
        return body(*refs[:pos], *refs[pos + 1:])
    return wrapped


def _rms(x, w=None):
    y = x * lax.rsqrt(jnp.mean(x * x, axis=-1, keepdims=True) + EPS)
    return y if w is None else y * w


def _log_sigmoid(x):
    return -(jnp.maximum(-x, 0.0) + jnp.log1p(jnp.exp(-jnp.abs(x))))


def _sigmoid(x):
    return 1.0 / (1.0 + jnp.exp(-x))


def _dot(a, b):
    return jnp.dot(a, b, preferred_element_type=F32)


def _mlstm_in_body(h_ref, nw_ref, w_ref, bg_ref, hnw_ref, q_ref, k_ref, v_ref, o_ref, g_ref, *scratch, prompt):
    if prompt:
        wkgt_scr, = scratch

        @pl.when(pl.program_id(0) == 0)
        def _():
            wkgt_scr[0:HK, :] = w_ref[:, K0:V0].astype(F32).T.astype(BF16)
            wg = jnp.concatenate([w_ref[:, G0:MLSTM_IN].astype(F32),
                                  jnp.zeros((D_MODEL, LANES - 2 * HEADS), F32)], axis=1)
            wkgt_scr[HK:, :] = wg.T.astype(BF16)

    u = _rms(h_ref[...], nw_ref[...]).astype(BF16)
    q_ref[...] = (_dot(u, w_ref[:, Q0:K0]) * (DQK ** -0.5)).astype(q_ref.dtype)
    v_ref[...] = _dot(u, w_ref[:, V0:O0]).astype(v_ref.dtype)
    o_ref[...] = _sigmoid(_dot(u, w_ref[:, O0:G0])) * hnw_ref[...]
    if prompt:
        kgt =lax.dot_general(wkgt_scr[...], u, _NT, preferred_element_type=F32)
        k_ref[...] = kgt[:HK].astype(k_ref.dtype)
        gt = kgt[HK:HK + 2 * HEADS] + bg_ref[...]
        ig, lf = gt[:HEADS], _log_sigmoid(gt[HEADS:])
        tm = lf.shape[1]
        r = lax.broadcasted_iota(jnp.int32, (tm, tm), 0)
        t = lax.broadcasted_iota(jnp.int32, (tm, tm), 1)
        same_chunk_upper = ((r <= t) & (r // CHUNK == t // CHUNK)).astype(F32)
        b = jnp.dot(lf, same_chunk_upper, precision=lax.Precision.HIGHEST, preferred_element_type=F32)
        g_ref[...] = jnp.concatenate([ig, lf, b], axis=0)
    else:
        k_ref[...] = _dot(u, w_ref[:, K0:V0]).astype(k_ref.dtype)
        g = _dot(u, w_ref[:, G0:MLSTM_IN]) + bg_ref[...]
        lane = lax.broadcasted_iota(jnp.int32, g.shape, 1)
        g_ref[...] = jnp.where(lane < HEADS, g, _log_sigmoid(g))


def _mlstm_in(h, nw, w, bg, hnw, *, layer, mixer, prompt, act_dtype):
    t = h.shape[0]
    tm = _row_tile(t)
    row = lambda n: pl.BlockSpec((tm, n), lambda i: (i, 0))
    col = lambda n: pl.BlockSpec((n, tm), lambda i: (0, i))
    kshape, kspec = ((HK, t), col(HK)) if prompt else ((t, HK), row(HK))
    gshape, gspec = ((3 * HEADS, t), col(3 * HEADS)) if prompt else ((t, 2 * HEADS), row(2 * HEADS))
    return pl.pallas_call(
        functools.partial(_mlstm_in_body, prompt=prompt),
        grid=(t // tm,),
        in_specs=[row(D_MODEL), _layer(nw, layer), _layer(w, mixer), _resident(bg.shape), _layer(hnw, mixer)],
        out_specs=[row(HK), kspec, row(HV), row(D_MODEL), gspec],
        out_shape=[jax.ShapeDtypeStruct((t, HK), act_dtype), jax.ShapeDtypeStruct(kshape, act_dtype),
                   jax.ShapeDtypeStruct((t, HV), act_dtype), jax.ShapeDtypeStruct((t, D_MODEL), F32),
                   jax.ShapeDtypeStruct(gshape, F32)],
        scratch_shapes=[pltpu.VMEM((HK + LANES, D_MODEL), BF16)] if prompt else [],
        compiler_params=_params(1), name="mlstm_in",
    )(h, nw, w, bg, hnw)


def _swa_in_body(h_ref, nw_ref, w_ref, q_ref, k_ref, kt_ref, v_ref):
    u = _rms(h_ref[...], nw_ref[...]).astype(BF16)
    q_ref[...] = (_dot(u, w_ref[:, :D_MODEL]) * (HEAD_DIM ** -0.5)).astype(q_ref.dtype)
    k = _dot(u, w_ref[:, D_MODEL:D_MODEL + KV_DIM])
    k_ref[...] = k
    kt_ref[...] = k.T.astype(kt_ref.dtype)
    v_ref[...] = _dot(u, w_ref[:, D_MODEL + KV_DIM:])


def _swa_in(h, nw, w, *, layer, mixer):
    t = h.shape[0]
    tm = _row_tile(t)
    row = lambda n: pl.BlockSpec((tm, n), lambda i: (i, 0))
    return pl.pallas_call(
        _swa_in_body,
        grid=(t // tm,),
        in_specs=[row(D_MODEL), _layer(nw, layer), _layer(w, mixer)],
        out_specs=[row(D_MODEL), row(KV_DIM), pl.BlockSpec((KV_DIM, tm), lambda i: (0, i)), row(KV_DIM)],
        out_shape=[jax.ShapeDtypeStruct((t, D_MODEL), BF16), jax.ShapeDtypeStruct((t, KV_DIM), F32),
                   jax.ShapeDtypeStruct((KV_DIM, t), BF16), jax.ShapeDtypeStruct((t, KV_DIM), F32)],
        compiler_params=_params(1), name="swa_in",
    )(h, nw, w)


FF_CHUNK = 1024


def _post_rows(a, h, p, wo_ref, npm_ref, npf_ref, wup_ref, wdn_ref, npff_ref, wpj_ref, pln_ref, wgate_ref):
    y = _dot(a.astype(BF16), wo_ref[...])
    h1 = h + _rms(y, npm_ref[...])
    u = _rms(h1, npf_ref[...]).astype(BF16)
    acc = jnp.zeros(h1.shape, F32)
    for c in range(D_FF // FF_CHUNK):
        hid = jnp.maximum(_dot(u, wup_ref[:, c * FF_CHUNK:(c + 1) * FF_CHUNK]), 0.0)
        acc = acc + _dot((hid * hid).astype(BF16), wdn_ref[c * FF_CHUNK:(c + 1) * FF_CHUNK, :])
    h2 = h1 + _rms(acc, npff_ref[...])
    e = _rms(_dot(p.astype(BF16), wpj_ref[...]), pln_ref[...])
    g = _sigmoid(_dot(_rms(h2).astype(BF16), wgate_ref[...]))
    return h2 + e * g


def _post_body(a_ref, h_ref, p_ref, *rest):
    w_refs, out_ref = rest[:-1], rest[-1]
    out_ref[...] = _post_rows(a_ref[...], h_ref[...], p_ref[...], *w_refs)


_POST_KEYS = ("npm", "npf", "wup", "wdn", "npff", "wpj", "pln", "wgate")


def _post_weights(wo, w, layer, mixer):
    return ([_layer(wo, mixer)] + [_layer(w[k], layer) for k in _POST_KEYS], [wo] + [w[k] for k in _POST_KEYS])


def _post(a, h, p, wo, w, *, layer, mixer):
    t = h.shape[0]
    tm = _row_tile(t)
    row = lambda n: pl.BlockSpec((tm, n), lambda i: (i, 0))
    w_specs, w_args = _post_weights(wo, w, layer, mixer)
    return pl.pallas_call(
        _post_body,
        grid=(t // tm,),
        in_specs=[row(D_MODEL), row(D_MODEL), pl.BlockSpec((None, tm, PLE_DIM), lambda i: (layer, i, 0))] + w_specs,
        out_specs=row(D_MODEL),
        out_shape=jax.ShapeDtypeStruct((t, D_MODEL), F32),
        compiler_params=_params(1), name="post",
    )(a, h, p, *w_args)


def _skewed_step(s, nt, mixer, post):
    @pl.when(s == 0)
    def _():
        mixer()

    @pl.when((s > 0) & (s < nt))
    def _():
        post()
        mixer()

    @pl.when(s == nt)
    def _():
        post()


def _pipelined_specs(tm, nt, layer):
    mix_i = lambda s: jnp.minimum(s, nt - 1)
    post_i = lambda s: jnp.maximum(s - 1, 0)
    mix_row = lambda n: pl.BlockSpec((tm, n), lambda s: (mix_i(s), 0))
    mix_col = lambda n: pl.BlockSpec((n, tm), lambda s: (0, mix_i(s)))
    post_row = lambda n: pl.BlockSpec((tm, n), lambda s: (post_i(s), 0))
    p_spec = pl.BlockSpec((None, tm, PLE_DIM), lambda s: (layer, post_i(s), 0))
    return mix_i, mix_row, mix_col, post_row, p_spec


STATE_W = 2 * DV


def _scan_chunk(q_ref, kt_ref, v_ref, og_ref, gt_ref, hg_ref, cn_scr, m_scr, r0):
    L = CHUNK
    rows = slice(r0, r0 + L)
    ri = lax.broadcasted_iota(jnp.int32, (L, L), 0)
    ci = lax.broadcasted_iota(jnp.int32, (L, L), 1)
    tri = ci <= ri
    gt = gt_ref[:, rows]
    ig_rows, lf_rows, b_rows = gt[:HEADS], gt[HEADS:2 * HEADS], gt[2 * HEADS:]
    a_rows = ig_rows - b_rows
    m_prev = m_scr[...]
    b_last = jnp.broadcast_to(b_rows[:, L - 1:L], (HEADS, L))
    m_new = b_last + jnp.maximum(m_prev, jnp.max(a_rows, axis=1, keepdims=True))
    decay_rows = jnp.exp(b_last + m_prev - m_new)
    wk_rows = jnp.exp(b_last - b_rows + ig_rows - m_new)
    m_scr[...] = m_new

    ones_rhs = jnp.ones((L, DV), BF16)
    zeros_kt = jnp.zeros((DQK, L), BF16)
    zeros_cn = jnp.zeros((DQK, STATE_W), BF16)
    for j in range(HEADS // 2):
        q_pair = q_ref[rows, j * 128:(j + 1) * 128]
        kt_pair = [kt_ref[h * DQK:(h + 1) * DQK, rows] for h in (2 * j, 2 * j + 1)]
        s_pair = _dot(q_pair, jnp.concatenate([jnp.concatenate([kt_pair[0], zeros_kt], axis=0),
                                               jnp.concatenate([zeros_kt, kt_pair[1]], axis=0)], axis=1))
        for par in range(2):
            h = 2 * j + par
            kt_h = kt_pair[par]
            v_ext = jnp.concatenate([v_ref[rows, h * DV:(h + 1) * DV], ones_rhs], axis=1)
            cn = cn_scr[h]
            cn16 = cn.astype(BF16)
            a_row, m_row = a_rows[h:h + 1], m_prev[h:h + 1]

            big_m = jnp.maximum(jnp.max(jnp.where(tri, a_row, -jnp.inf), axis=1, keepdims=True), m_row)
            b_col = jnp.sum(jnp.where(tri, lf_rows[h:h + 1], 0.0), axis=1, keepdims=True)
            w_inter = jnp.exp(m_row - big_m)
            s = s_pair[:, par * L:(par + 1) * L] * jnp.exp(jnp.where(tri, a_row - big_m, NEG))
            q_w = (q_pair.astype(F32) * w_inter).astype(BF16)
            state = jnp.concatenate([cn16, zeros_cn] if par == 0 else [zeros_cn, cn16], axis=0)
            nd = _dot(jnp.concatenate([s.astype(BF16), q_w], axis=1), jnp.concatenate([v_ext, state], axis=0))
            hh = nd[:, :DV] * (1.0 / jnp.maximum(jnp.abs(nd[:, DV:]), jnp.exp(-(b_col + big_m))))

            kw_t = kt_h.astype(F32) * wk_rows[h:h + 1]
            dec = decay_rows[h:h + 1]
            cn_scr[h] = jnp.concatenate([dec, dec], axis=1) * cn + _dot(kw_t.astype(BF16), v_ext)

            hg_ref[rows, h * DV:(h + 1) * DV] = (_rms(hh) * og_ref[rows, h * DV:(h + 1) * DV]).astype(hg_ref.dtype)


def _mlstm_post_body(q_ref, kt_ref, v_ref, og_ref, gt_ref, h_ref, p_ref, *rest, tiles_per_seq):
    w_refs = rest[:-7]
    out_ref, c_out_ref, n_out_ref, m_out_ref, hg_scr, cn_scr, m_scr = rest[-7:]
    s = pl.program_id(0)
    nt = pl.num_programs(0) - 1
    tile = jnp.minimum(s, nt - 1)

    @pl.when((tile % tiles_per_seq == 0) & (s < nt))
    def _():
        cn_scr[...] = jnp.zeros(cn_scr.shape, F32)
        m_scr[...] = jnp.zeros(m_scr.shape, F32)

    def post():
        out_ref[...] = _post_rows(hg_scr[(s + 1) % 2], h_ref[...], p_ref[...], *w_refs)

    def mixer():
        hg_slot = hg_scr.at[s % 2]
        for r0 in range(0, q_ref.shape[0], CHUNK):
            _scan_chunk(q_ref, kt_ref, v_ref, og_ref, gt_ref, hg_slot, cn_scr, m_scr, r0)

    _skewed_step(s, nt, mixer, post)

    @pl.when((tile % tiles_per_seq == tiles_per_seq - 1) & (s < nt))
    def _():
        for h in range(HEADS):
            c_out_ref[0, h] = cn_scr[h, :, 0:DV]
            n_out_ref[0, h] = cn_scr[h, :, DV:STATE_W]
        m_out_ref[0] = m_scr[...]


def _mlstm_post(q, kt, v, og, gt, h, p, wo, w, *, layer, mixer, batch, seq):
    t = batch * seq
    tm = ROW_TILE
    nt = t // tm
    tiles_per_seq = seq // tm
    mix_i, mix_row, mix_col, post_row, p_spec = _pipelined_specs(tm, nt, layer)
    seq_i = lambda s: mix_i(s) // tiles_per_seq
    w_specs, w_args = _post_weights(wo, w, layer, mixer)
    return pl.pallas_call(
        functools.partial(_mlstm_post_body, tiles_per_seq=tiles_per_seq),
        grid=(nt + 1,),
        in_specs=[mix_row(HK), mix_col(HK), mix_row(HV), mix_row(D_MODEL), mix_col(3 * HEADS),
                  post_row(D_MODEL), p_spec] + w_specs,
        out_specs=[post_row(D_MODEL),
                   pl.BlockSpec((1, HEADS, DQK, DV), lambda s: (seq_i(s), 0, 0, 0)),
                   pl.BlockSpec((1, HEADS, DQK, STATE_W - DV), lambda s: (seq_i(s), 0, 0, 0)),
                   pl.BlockSpec((1, HEADS, LANES), lambda s: (seq_i(s), 0, 0))],
        out_shape=[jax.ShapeDtypeStruct((t, D_MODEL), F32),
                   jax.ShapeDtypeStruct((batch, HEADS, DQK, DV), F32),
                   jax.ShapeDtypeStruct((batch, HEADS, DQK, STATE_W - DV), F32),
                   jax.ShapeDtypeStruct((batch, HEADS, LANES), F32)],
        scratch_shapes=[pltpu.VMEM((2, tm, HV), BF16), pltpu.VMEM((HEADS, DQK, STATE_W), F32),
                        pltpu.VMEM((HEADS, LANES), F32)],
        compiler_params=_params(1), name="mlstm_post",
    )(q, kt, v, og, gt, h, p, *w_args)


STEP_BB = 8
_TN = (((0,), (0,)), ((), ()))


def _mlstm_step_body(c_ref, n_ref, m_ref, q_ref, k_ref, v_ref, og_ref, g_ref,
                     hg_ref, c_out_ref, n_out_ref, m_out_ref, *, own):
    bb = STEP_BB
    for slab in range(c_out_ref.shape[0]):
        if slab != own:
            c_out_ref[slab] = jnp.zeros(c_out_ref.shape[1:], F32)
    g = g_ref[...]
    ig, lf, m_prev = g[:, :HEADS], g[:, HEADS:], m_ref[...]
    inter = lf + m_prev
    m_t = jnp.maximum(inter, ig)
    w_inter = jnp.exp(inter - m_t)
    in_w = jnp.exp(ig - m_t)
    decay = jnp.exp(lf + m_prev - m_t)
    floor = jnp.exp(-m_t)
    m_out_ref[...] = m_t
    diag = (lax.broadcasted_iota(jnp.int32, (bb, bb * DV), 0)
            == lax.broadcasted_iota(jnp.int32, (bb, bb * DV), 1) // DV)
    row_v = lax.broadcasted_iota(jnp.int32, (bb, DV), 0)

    for h in range(HEADS):
        col = lambda a: a[:, h:h + 1]
        q = q_ref[:, h * DQK:(h + 1) * DQK]
        k = k_ref[:, h * DQK:(h + 1) * DQK]
        v = v_ref[:, h * DV:(h + 1) * DV]
        n = n_ref[:, h, :]
        s = jnp.sum(q * k, axis=1, keepdims=True) * col(in_w)
        kw = k * col(in_w)
        c_tiles = [c_ref[i, h * DQK:(h + 1) * DQK, :] for i in range(bb)]
        qc_all = _dot(q.astype(BF16), jnp.concatenate(c_tiles, axis=1).astype(BF16))
        v_bd = jnp.where(diag, jnp.concatenate([v] * bb, axis=1), 0.0).astype(BF16)
        outer = lax.dot_general(kw.astype(BF16), v_bd, _TN, preferred_element_type=F32)
        qc = jnp.zeros((bb, DV), F32)
        for i in range(bb):
            qc = jnp.where(row_v == i, qc_all[:, i * DV:(i + 1) * DV], qc)
            c_out_ref[own, i, h * DQK:(h + 1) * DQK, :] = (decay[i:i + 1, h:h + 1] * c_tiles[i]
                                                           + outer[:, i * DV:(i + 1) * DV])
        num = col(w_inter) * qc + s * v
        den = col(w_inter) * jnp.sum(q * n, axis=1, keepdims=True) + s
        hh = num * (1.0 / jnp.maximum(jnp.abs(den), col(floor)))
        n_out_ref[:, h, :] = col(decay) * n + kw
        hg_ref[:, h * DV:(h + 1) * DV] = _rms(hh) * og_ref[:, h * DV:(h + 1) * DV]


def _mlstm_step(c_all, c_new_prev, n, m, q, k, v, og, g, *, mixer):
    n_slabs, b = c_all.shape[:2]
    bb = STEP_BB
    rows = lambda width: pl.BlockSpec((bb, width), lambda i: (i, 0))
    cspec = pl.BlockSpec((None, bb, HK, DV), lambda i: (mixer, i, 0, 0))
    nspec = pl.BlockSpec((bb, HEADS, DQK), lambda i: (i, 0, 0))
    in_specs = [cspec, nspec, rows(HEADS), rows(HK), rows(HK), rows(HV), rows(D_MODEL), rows(2 * HEADS)]
    args = [c_all, n, m, q, k, v, og, g]
    if c_new_prev is None:
        c_out_spec = pl.BlockSpec((n_slabs, bb, HK, DV), lambda i: (0, i, 0, 0))
        body, aliases = functools.partial(_mlstm_step_body, own=mixer), {}
    else:
        c_out_spec = pl.BlockSpec((1, bb, HK, DV), lambda i: (mixer, i, 0, 0))
        in_specs.append(pl.BlockSpec(memory_space=pl.ANY))
        body, aliases = _skip_ref(functools.partial(_mlstm_step_body, own=0), len(args)), {len(args): 1}
        args.append(c_new_prev)
    return pl.pallas_call(
        body,
        grid=(b // bb,),
        in_specs=in_specs,
        out_specs=[rows(HV), c_out_spec, nspec, rows(HEADS)],
        out_shape=[jax.ShapeDtypeStruct((b, HV), F32), jax.ShapeDtypeStruct(c_all.shape, F32),
                   jax.ShapeDtypeStruct(n.shape, F32), jax.ShapeDtypeStruct((b, HEADS), F32)],
        input_output_aliases=aliases,
        compiler_params=_params(1), name="mlstm_step",
    )(*args)


def _bias_body(bucket_ref, rel_ref, out_ref):
    bucket = bucket_ref[...]
    for h in range(Q_HEADS):
        acc = jnp.zeros(bucket.shape, F32)
        for b in range(REL_BUCKETS):
            acc = jnp.where(bucket == b, rel_ref[b, h], acc)
        out_ref[h] = acc


def _bias_table(bucket, rel_bias):
    shape = bucket.shape
    return pl.pallas_call(
        _bias_body,
        in_specs=[pl.BlockSpec(memory_space=pltpu.VMEM), pl.BlockSpec(memory_space=pltpu.SMEM)],
        out_specs=pl.BlockSpec(memory_space=pltpu.VMEM),
        out_shape=jax.ShapeDtypeStruct((Q_HEADS,) + shape, F32),
        name="rel_bias_table",
    )(bucket, rel_bias)


def _rel_bucket(dist):
    max_exact = REL_BUCKETS // 2
    d = jnp.maximum(dist, 0)
    df = jnp.maximum(d, 1).astype(F32)
    large = max_exact + (jnp.log(df / max_exact) / math.log(REL_MAX_DIST / max_exact)
                         * (REL_BUCKETS - max_exact)).astype(jnp.int32)
    large = jnp.minimum(large, REL_BUCKETS - 1)
    return jnp.where(d < max_exact, d, large)


def _swa_block(q, kt_prev, kt_cur, v_prev, v_cur, bias_ref, sink_ref, in_range, out_ref, rows):
    blk = WINDOW
    qi = lax.broadcasted_iota(jnp.int32, (blk, blk), 0)
    kj = lax.broadcasted_iota(jnp.int32, (blk, blk), 1)
    from_prev = kj > qi
    lane = lax.broadcasted_iota(jnp.int32, (2 * blk, LANES), 1)
    lane_q = lax.broadcasted_iota(jnp.int32, (blk, LANES), 1)
    zero_k = jnp.zeros((HEAD_DIM, 2 * blk), BF16)
    ones_bd = jnp.concatenate([(lane < HEAD_DIM).astype(BF16), (lane >= HEAD_DIM).astype(BF16)], axis=0)

    for g in range(KV_HEADS):
        hd_rows = slice(g * HEAD_DIM, (g + 1) * HEAD_DIM)
        kt_g = jnp.concatenate([kt_prev[hd_rows, :], kt_cur[hd_rows, :]], axis=1)
        rhs = jnp.concatenate([jnp.concatenate([kt_g, zero_k], axis=0),
                               jnp.concatenate([zero_k, kt_g], axis=0)], axis=1)
        v_lanes = slice((g // 2) * LANES, (g // 2 + 1) * LANES)
        vt = jnp.concatenate([v_prev[:, v_lanes], v_cur[:, v_lanes]], axis=0)
        vr = pltpu.roll(vt, HEAD_DIM, axis=1)
        v_lo = jnp.where(lane < HEAD_DIM, vt if g % 2 == 0 else vr, 0.0)
        v_hi = jnp.where(lane >= HEAD_DIM, vr if g % 2 == 0 else vt, 0.0)
        v_bd = jnp.concatenate([jnp.concatenate([v_lo, v_hi], axis=0).astype(BF16), ones_bd], axis=1)
        for pair in range(GROUP // 2):
            jp = g * (GROUP // 2) + pair
            logits = _dot(q[:, jp * LANES:(jp + 1) * LANES], rhs)
            ps, sk = [], []
            for i in range(2):
                hd = 2 * jp + i
                lg = jnp.where(from_prev, logits[:, 2 * i * blk:(2 * i + 1) * blk],
                               logits[:, (2 * i + 1) * blk:(2 * i + 2) * blk]) + bias_ref[hd]
                if in_range is not None:
                    lg = jnp.where(in_range, lg, NEG)
                sink = sink_ref[hd]
                mx = jnp.maximum(jnp.max(lg, axis=1, keepdims=True), sink)
                p = jnp.exp(lg - mx)
                ps += [jnp.where(from_prev, p, 0.0).astype(BF16), jnp.where(from_prev, 0.0, p).astype(BF16)]
                sk.append(jnp.exp(sink - mx))
            pv = _dot(jnp.concatenate(ps, axis=1), v_bd)
            den = pv[:, LANES:] + jnp.where(lane_q < HEAD_DIM, sk[0], sk[1])
            out_ref[rows, jp * LANES:(jp + 1) * LANES] = (pv[:, :LANES] * (1.0 / den)).astype(out_ref.dtype)


def _swa_post_body(q_ref, kt_ref, ktp_ref, v_ref, vp_ref, bias_ref, sink_ref, h_ref, p_ref, *rest, tiles_per_seq):
    w_refs, out_ref, attn_scr = rest[:-2], rest[-2], rest[-1]
    s = pl.program_id(0)
    nt = pl.num_programs(0) - 1
    tile = jnp.minimum(s, nt - 1)

    def post():
        out_ref[...] = _post_rows(attn_scr[(s + 1) % 2], h_ref[...], p_ref[...], *w_refs)

    def mixer():
        attn_slot = attn_scr.at[s % 2]
        blk = WINDOW
        qi = lax.broadcasted_iota(jnp.int32, (blk, blk), 0)
        kj = lax.broadcasted_iota(jnp.int32, (blk, blk), 1)
        has_prev = (kj <= qi) | (tile % tiles_per_seq != 0)
        for b in range(q_ref.shape[0] // blk):
            rows, prev_rows = slice(b * blk, (b + 1) * blk), slice((b - 1) * blk, b * blk)
            _swa_block(q_ref[rows, :],
                       ktp_ref[...] if b == 0 else kt_ref[:, prev_rows], kt_ref[:, rows],
                       vp_ref[...] if b == 0 else v_ref[prev_rows, :], v_ref[rows, :],
                       bias_ref, sink_ref, has_prev if b == 0 else None, attn_slot, rows)

    _skewed_step(s, nt, mixer, post)


def _swa_post(q, kt, v, bias, sinks, h, p, wo, w, *, layer, mixer, batch, seq):
    t = batch * seq
    tm = ROW_TILE
    nt = t // tm
    blocks = tm // WINDOW
    mix_i, mix_row, mix_col, post_row, p_spec = _pipelined_specs(tm, nt, layer)
    prev_blk = lambda s: jnp.maximum(mix_i(s) * blocks - 1, 0)
    w_specs, w_args = _post_weights(wo, w, layer, mixer)
    return pl.pallas_call(
        functools.partial(_swa_post_body, tiles_per_seq=seq // tm),
        grid=(nt + 1,),
        in_specs=[mix_row(D_MODEL), mix_col(KV_DIM), pl.BlockSpec((KV_DIM, WINDOW), lambda s: (0, prev_blk(s))),
                  mix_row(KV_DIM), pl.BlockSpec((WINDOW, KV_DIM), lambda s: (prev_blk(s), 0)),
                  _resident(bias.shape), pl.BlockSpec(memory_space=pltpu.SMEM), post_row(D_MODEL), p_spec] + w_specs,
        out_specs=post_row(D_MODEL),
        out_shape=jax.ShapeDtypeStruct((t, D_MODEL), F32),
        scratch_shapes=[pltpu.VMEM((2, tm, D_MODEL), BF16)],
        compiler_params=_params(1), name="swa_post",
    )(q, kt, kt, v, v, bias, sinks[mixer], h, p, *w_args)


SAMPLE_BB = 8


def _swa_sample_body(q_ref, kc_ref, vc_ref, kn_ref, vn_ref, bias_ref, sink_ref, out_ref, ko_ref, vo_ref):
    w = WINDOW
    row = lax.broadcasted_iota(jnp.int32, (w, KV_DIM), 0)
    rg = lax.broadcasted_iota(jnp.int32, (Q_HEADS, KV_DIM), 0) // GROUP
    lg = lax.broadcasted_iota(jnp.int32, (Q_HEADS, KV_DIM), 1) // HEAD_DIM
    rsel = lax.broadcasted_iota(jnp.int32, (Q_HEADS, HEAD_DIM), 0) // GROUP
    logits, v_wins = [], []
    for i in range(SAMPLE_BB):
        k_win = jnp.where(row == w - 1, kn_ref[i], pltpu.roll(kc_ref[i], w - 1, axis=0))
        v_win = jnp.where(row == w - 1, vn_ref[i], pltpu.roll(vc_ref[i], w - 1, axis=0))
        ko_ref[i] = k_win
        vo_ref[i] = v_win
        qb = q_ref[i].astype(F32)
        q_bd = jnp.where(rg == lg, jnp.concatenate([qb] * KV_HEADS, axis=1), 0.0).astype(BF16)
        logits.append(lax.dot_general(q_bd, k_win.astype(BF16), _NT, preferred_element_type=F32))
        v_wins.append(v_win.astype(BF16))
    lg_all = jnp.concatenate(logits, axis=0) + bias_ref[...]
    sink = sink_ref[...]
    mx = jnp.maximum(jnp.max(lg_all, axis=1, keepdims=True), sink)
    p = jnp.exp(lg_all - mx)
    probs = (p * (1.0 / (jnp.sum(p, axis=1, keepdims=True) + jnp.exp(sink - mx)))).astype(BF16)
    for i in range(SAMPLE_BB):
        o = _dot(probs[i * Q_HEADS:(i + 1) * Q_HEADS], v_wins[i])
        sel = o[:, 0:HEAD_DIM]
        for g in range(1, KV_HEADS):
            sel = jnp.where(rsel == g, o[:, g * HEAD_DIM:(g + 1) * HEAD_DIM], sel)
        out_ref[i] = sel.astype(out_ref.dtype)


def _swa_sample(q, kc_all, vc_all, kn, vn, bias, sinks, *, mixer):
    b = q.shape[0]
    bb = SAMPLE_BB
    blk = lambda shape: pl.BlockSpec((bb,) + shape, lambda i: (i, 0, 0))
    cache = pl.BlockSpec((None, bb, WINDOW, KV_DIM), lambda i: (mixer, i, 0, 0))
    return pl.pallas_call(
        _swa_sample_body,
        grid=(b // bb,),
        in_specs=[blk((Q_HEADS, HEAD_DIM)), cache, cache, blk((1, KV_DIM)), blk((1, KV_DIM)), _resident(bias.shape),
                  _layer(sinks, mixer)],
        out_specs=[blk((Q_HEADS, HEAD_DIM)), cache, cache],
        out_shape=[jax.ShapeDtypeStruct((b, Q_HEADS, HEAD_DIM), BF16), jax.ShapeDtypeStruct(kc_all.shape, F32),
                   jax.ShapeDtypeStruct(vc_all.shape, F32)],
        input_output_aliases={1: 1, 2: 2},
        compiler_params=_params(1), name="swa_sample",
    )(q, kc_all, vc_all, kn, vn, bias, sinks)


def kernel(x_prompt, x_sample, state_mlstm_C, state_mlstm_n, state_mlstm_m, cache_swa_k, cache_swa_v, p_prompt, p_sample, rel_bias, norm_pre_mix, norm_post_mix, norm_pre_ffn, norm_post_ffn, mlstm_w_in, mlstm_b_i, mlstm_b_f, mlstm_norm, mlstm_w_out, swa_w_qkv, swa_sinks, swa_w_o, ffn_w_up, ffn_w_down, ple_w_proj, ple_norm, ple_w_gate):
    bp, sp, _ = x_prompt.shape
    bs = x_sample.shape[0]
    win = cache_swa_k.shape[2]
    assert win == WINDOW and x_sample.shape[1] == 1
    bf = lambda a: a.astype(BF16)
    vecs = lambda a: a.astype(F32)[:, None, :]

    post_w = dict(npm=vecs(norm_post_mix), npf=vecs(norm_pre_ffn), wup=bf(ffn_w_up), wdn=bf(ffn_w_down),
                  npff=vecs(norm_post_ffn), wpj=bf(ple_w_proj), pln=vecs(ple_norm), wgate=bf(ple_w_gate))
    n_pre = vecs(norm_pre_mix)
    m_w = bf(mlstm_w_in)
    m_bias = jnp.concatenate([mlstm_b_i, mlstm_b_f], axis=1).astype(F32)
    m_wout, m_norm = bf(mlstm_w_out), vecs(mlstm_norm)
    s_w, s_wo = bf(swa_w_qkv), bf(swa_w_o)
    sinks = swa_sinks.astype(F32)

    qi = jnp.arange(WINDOW)[:, None]
    kj = jnp.arange(WINDOW)[None, :]
    dist_p = jnp.where(kj > qi, qi + WINDOW - kj, qi - kj)
    bias_p = _bias_table(_rel_bucket(dist_p).astype(jnp.int32), rel_bias.astype(F32))
    dist_s = (win - 1) - jnp.arange(win)
    bias_s = _bias_table(jnp.broadcast_to(_rel_bucket(dist_s)[None, :], (8, win)).astype(jnp.int32),
                         rel_bias.astype(F32))[:, 0, :]
    bias_s = jnp.tile(bias_s, (SAMPLE_BB, 1))
    sinks_s = jnp.tile(sinks[:, :, None], (1, SAMPLE_BB, 1))

    hp = x_prompt.reshape(bp * sp, D_MODEL)
    hs = x_sample.reshape(bs, D_MODEL)
    pp = p_prompt.reshape(DEPTH, bp * sp, PLE_DIM)
    ps = p_sample.reshape(DEPTH, bs, PLE_DIM)
    c_old = state_mlstm_C.astype(F32).reshape(-1, bs, HK, DV)
    k_win = cache_swa_k.astype(F32).reshape(-1, bs, win, KV_DIM)
    v_win = cache_swa_v.astype(F32).reshape(-1, bs, win, KV_DIM)
    c_new = None
    outs = dict(Cp=[], np=[], mp=[], kp=[], vp=[], ns=[], ms=[])

    for i in range(DEPTH):
        j = i // N_MIXERS
        if i % N_MIXERS == 0:
            wo = m_wout
            q, kt, v, og, gt = _mlstm_in(hp, n_pre, m_w, m_bias[j].reshape(-1, 1), m_norm,
                                         layer=i, mixer=j, prompt=True, act_dtype=BF16)
            hp, c_fin, n_fin, m_fin = _mlstm_post(q, kt, v, og, gt, hp, pp, wo, post_w,
                                                  layer=i, mixer=j, batch=bp, seq=sp)
            outs["Cp"].append(c_fin)
            outs["np"].append(n_fin[..., 0])
            outs["mp"].append(m_fin[..., 0])
            q, k, v, og, g = _mlstm_in(hs, n_pre, m_w, m_bias[j].reshape(1, -1), m_norm,
                                       layer=i, mixer=j, prompt=False, act_dtype=F32)
            a_s, c_new, n_new, m_new = _mlstm_step(
                c_old, c_new, state_mlstm_n[j].astype(F32), state_mlstm_m[j].astype(F32),
                q, k, v, og, g, mixer=j)
            outs["ns"].append(n_new)
            outs["ms"].append(m_new)
        else:
            wo = s_wo
            q, k, kt, v = _swa_in(hp, n_pre, s_w, layer=i, mixer=j)
            hp = _swa_post(q, kt, v, bias_p, sinks, hp, pp, wo, post_w, layer=i, mixer=j, batch=bp, seq=sp)
            last = lambda a: a.reshape(bp, sp, KV_DIM)[:, sp - win:].reshape(bp, win, KV_HEADS, HEAD_DIM)
            outs["kp"].append(last(k))
            outs["vp"].append(last(v))
            q, k, _, v = _swa_in(hs, n_pre, s_w, layer=i, mixer=j)
            a_s, k_win, v_win = _swa_sample(
                q.reshape(bs, Q_HEADS, HEAD_DIM), k_win, v_win, k.reshape(bs, 1, KV_DIM), v.reshape(bs, 1, KV_DIM),
                bias_s, sinks_s, mixer=j)
            a_s = a_s.reshape(bs, D_MODEL)
        hs = _post(a_s, hs, ps, wo, post_w, layer=i, mixer=j)

    st = lambda key, like: jnp.stack(outs[key]).astype(like.dtype)
    kv_out = lambda a, like: a.reshape(-1, bs, win, KV_HEADS, HEAD_DIM).astype(like.dtype)
    return (hp.reshape(x_prompt.shape), hs.reshape(x_sample.shape),
            st("Cp", state_mlstm_C), st("np", state_mlstm_n), st("mp", state_mlstm_m),
            st("kp", cache_swa_k), st("vp", cache_swa_v),
            c_new.reshape(state_mlstm_C.shape).astype(state_mlstm_C.dtype), st("ns", state_mlstm_n),
            st("ms", state_mlstm_m), kv_out(k_win, cache_swa_k), kv_out(v_win, cache_swa_v))
```

```python
import functools
import math

import jax
import jax.numpy as jnp
from jax import lax
from jax.experimental import pallas as pl
from jax.experimental.pallas import tpu as pltpu

F32 = jnp.float32
BF16 = jnp.bfloat16

D_MODEL = 1024
DEPTH = 4
N_MIXERS = 2
HEADS = 8
DQK = 64
DV = 128
CHUNK = 128
Q_HEADS = 16
KV_HEADS = 4
GROUP = Q_HEADS // KV_HEADS
HEAD_DIM = 64
KV_DIM = KV_HEADS * HEAD_DIM
WINDOW = 128
REL_BUCKETS = 32
REL_MAX_DIST = 128
PLE_DIM = 256
D_FF = 4 * D_MODEL
EPS = 1e-6
NEG = -1e30

HK, HV = HEADS * DQK, HEADS * DV
Q0, K0, V0, O0, G0 = 0, HK, 2 * HK, 2 * HK + HV, 2 * HK + HV + D_MODEL
MLSTM_IN = G0 + 2 * HEADS

V7X_VMEM_BYTES = 64 * 1024 * 1024
VMEM_LIMIT = V7X_VMEM_BYTES - 8 * 1024 * 1024
LANES = 128
ROW_TILE = 512

_NT = (((1,), (1,)), ((), ()))


def _row_tile(t):
    return ROW_TILE if t % ROW_TILE == 0 else t


def _params(n_grid):
    return pltpu.CompilerParams(dimension_semantics=("arbitrary",) * n_grid, vmem_limit_bytes=VMEM_LIMIT)


def _resident(shape):
    nd = len(shape)
    return pl.BlockSpec(shape, lambda *_: (0,) * nd, pipeline_mode=pl.Buffered(1))


def _layer(arr, layer):
    tail = arr.shape[1:]
    return pl.BlockSpec((None,) + tail, lambda *_: (layer,) + (0,) * len(tail), pipeline_mode=pl.Buffered(1))


def _skip_ref(body, pos):
    def wrapped(*refs):
        return body(*refs[:pos], *refs[pos + 1:])
    return wrapped


def _rms(x, w=None):
    y = x * lax.rsqrt(jnp.mean(x * x, axis=-1, keepdims=True) + EPS)
    return y if w is None else y * w


def _log_sigmoid(x):
    return -(jnp.maximum(-x, 0.0) + jnp.log1p(jnp.exp(-jnp.abs(x))))


def _sigmoid(x):
    return 1.0 / (1.0 + jnp.exp(-x))


def _dot(a, b):
    return jnp.dot(a, b, preferred_element_type=F32)


def _mlstm_in_body(h_ref, nw_ref, w_ref, bg_ref, hnw_ref, q_ref, k_ref, v_ref, o_ref, g_ref, *scratch, prompt):
    if prompt:
        wkgt_scr, = scratch

        @pl.when(pl.program_id(0) == 0)
        def _():
            wkgt_scr[0:HK, :] = w_ref[:, K0:V0].astype(F32).T.astype(BF16)
            wg = jnp.concatenate([w_ref[:, G0:MLSTM_IN].astype(F32),
                                  jnp.zeros((D_MODEL, LANES - 2 * HEADS), F32)], axis=1)
            wkgt_scr[HK:, :] = wg.T.astype(BF16)

    u = _rms(h_ref[...], nw_ref[...]).astype(BF16)
    q_ref[...] = (_dot(u, w_ref[:, Q0:K0]) * (DQK ** -0.5)).astype(q_ref.dtype)
    v_ref[...] = _dot(u, w_ref[:, V0:O0]).astype(v_ref.dtype)
    o_ref[...] = _sigmoid(_dot(u, w_ref[:, O0:G0])) * hnw_ref[...]
    if prompt:
        kgt = lax.dot_general(wkgt_scr[...], u, _NT, preferred_element_type=F32)
        k_ref[...] = kgt[:HK].astype(k_ref.dtype)
        gt = kgt[HK:HK + 2 * HEADS] + bg_ref[...]
        ig, lf = gt[:HEADS], _log_sigmoid(gt[HEADS:])
        r = lax.broadcasted_iota(jnp.int32, (CHUNK, CHUNK), 0)
        t = lax.broadcasted_iota(jnp.int32, (CHUNK, CHUNK), 1)
        upper = (r <= t).astype(F32)
        b = jnp.concatenate([jnp.dot(lf[:, c:c + CHUNK], upper, precision=lax.Precision.HIGHEST,
                                     preferred_element_type=F32) for c in range(0, lf.shape[1], CHUNK)], axis=1)
        g_ref[...] = jnp.concatenate([ig, lf, b], axis=0)
    else:
        k_ref[...] = _dot(u, w_ref[:, K0:V0]).astype(k_ref.dtype)
        g = _dot(u, w_ref[:, G0:MLSTM_IN]) + bg_ref[...]
        lane = lax.broadcasted_iota(jnp.int32, g.shape, 1)
        g_ref[...] = jnp.where(lane < HEADS, g, _log_sigmoid(g))


def _mlstm_in(h, nw, w, bg, hnw, *, layer, mixer, prompt, act_dtype):
    t = h.shape[0]
    tm = _row_tile(t)
    row = lambda n: pl.BlockSpec((tm, n), lambda i: (i, 0))
    col = lambda n: pl.BlockSpec((n, tm), lambda i: (0, i))
    kshape, kspec = ((HK, t), col(HK)) if prompt else ((t, HK), row(HK))
    gshape, gspec = ((3 * HEADS, t), col(3 * HEADS)) if prompt else ((t, 2 * HEADS), row(2 * HEADS))
    return pl.pallas_call(
        functools.partial(_mlstm_in_body, prompt=prompt),
        grid=(t // tm,),
        in_specs=[row(D_MODEL), _layer(nw, layer), _layer(w, mixer), _resident(bg.shape), _layer(hnw, mixer)],
        out_specs=[row(HK), kspec, row(HV), row(D_MODEL), gspec],
        out_shape=[jax.ShapeDtypeStruct((t, HK), act_dtype), jax.ShapeDtypeStruct(kshape, act_dtype),
                   jax.ShapeDtypeStruct((t, HV), act_dtype), jax.ShapeDtypeStruct((t, D_MODEL), F32),
                   jax.ShapeDtypeStruct(gshape, F32)],
        scratch_shapes=[pltpu.VMEM((HK + LANES, D_MODEL), BF16)] if prompt else [],
        compiler_params=_params(1), name="mlstm_in",
    )(h, nw, w, bg, hnw)


def _swa_in_body(h_ref, nw_ref, w_ref, q_ref, k_ref, kt_ref, v_ref):
    u = _rms(h_ref[...], nw_ref[...]).astype(BF16)
    q_ref[...] = (_dot(u, w_ref[:, :D_MODEL]) * (HEAD_DIM ** -0.5)).astype(q_ref.dtype)
    k = _dot(u, w_ref[:, D_MODEL:D_MODEL + KV_DIM])
    k_ref[...] = k
    kt_ref[...] = k.T.astype(kt_ref.dtype)
    v_ref[...] = _dot(u, w_ref[:, D_MODEL + KV_DIM:])


def _swa_in(h, nw, w, *, layer, mixer):
    t = h.shape[0]
    tm = _row_tile(t)
    row = lambda n: pl.BlockSpec((tm, n), lambda i: (i, 0))
    return pl.pallas_call(
        _swa_in_body,
        grid=(t // tm,),
        in_specs=[row(D_MODEL), _layer(nw, layer), _layer(w, mixer)],
        out_specs=[row(D_MODEL), row(KV_DIM), pl.BlockSpec((KV_DIM, tm), lambda i: (0, i)), row(KV_DIM)],
        out_shape=[jax.ShapeDtypeStruct((t, D_MODEL), BF16), jax.ShapeDtypeStruct((t, KV_DIM), F32),
                   jax.ShapeDtypeStruct((KV_DIM, t), BF16), jax.ShapeDtypeStruct((t, KV_DIM), F32)],
        compiler_params=_params(1), name="swa_in",
    )(h, nw, w)


FF_CHUNK = 1024


def _post_rows(a, h, p, wo_ref, npm_ref, npf_ref, wup_ref, wdn_ref, npff_ref, wpj_ref, pln_ref, wgate_ref):
    y = _dot(a.astype(BF16), wo_ref[...])
    h1 = h + _rms(y, npm_ref[...])
    u = _rms(h1, npf_ref[...]).astype(BF16)
    acc = jnp.zeros(h1.shape, F32)
    for c in range(D_FF // FF_CHUNK):
        hid = jnp.maximum(_dot(u, wup_ref[:, c * FF_CHUNK:(c + 1) * FF_CHUNK]), 0.0)
        acc = acc + _dot((hid * hid).astype(BF16), wdn_ref[c * FF_CHUNK:(c + 1) * FF_CHUNK, :])
    h2 = h1 + _rms(acc, npff_ref[...])
    e = _rms(_dot(p.astype(BF16), wpj_ref[...]), pln_ref[...])
    g = _sigmoid(_dot(_rms(h2).astype(BF16), wgate_ref[...]))
    return h2 + e * g


def _post_body(a_ref, h_ref, p_ref, *rest):
    w_refs, out_ref = rest[:-1], rest[-1]
    out_ref[...] = _post_rows(a_ref[...], h_ref[...], p_ref[...], *w_refs)


_POST_KEYS = ("npm", "npf", "wup", "wdn", "npff", "wpj", "pln", "wgate")


def _post_weights(wo, w, layer, mixer):
    return ([_layer(wo, mixer)] + [_layer(w[k], layer) for k in _POST_KEYS], [wo] + [w[k] for k in _POST_KEYS])


def _post(a, h, p, wo, w, *, layer, mixer):
    t = h.shape[0]
    tm = _row_tile(t)
    row = lambda n: pl.BlockSpec((tm, n), lambda i: (i, 0))
    w_specs, w_args = _post_weights(wo, w, layer, mixer)
    return pl.pallas_call(
        _post_body,
        grid=(t // tm,),
        in_specs=[row(D_MODEL), row(D_MODEL), pl.BlockSpec((None, tm, PLE_DIM), lambda i: (layer, i, 0))] + w_specs,
        out_specs=row(D_MODEL),
        out_shape=jax.ShapeDtypeStruct((t, D_MODEL), F32),
        compiler_params=_params(1), name="post",
    )(a, h, p, *w_args)


def _skewed_step(s, nt, mixer, post):
    @pl.when(s == 0)
    def _():
        mixer()

    @pl.when((s > 0) & (s < nt))
    def _():
        post()
        mixer()

    @pl.when(s == nt)
    def _():
        post()


def _pipelined_specs(tm, nt, layer):
    mix_i = lambda s: jnp.minimum(s, nt - 1)
    post_i = lambda s: jnp.maximum(s - 1, 0)
    mix_row = lambda n: pl.BlockSpec((tm, n), lambda s: (mix_i(s), 0))
    mix_col = lambda n: pl.BlockSpec((n, tm), lambda s: (0, mix_i(s)))
    post_row = lambda n: pl.BlockSpec((tm, n), lambda s: (post_i(s), 0))
    p_spec = pl.BlockSpec((None, tm, PLE_DIM), lambda s: (layer, post_i(s), 0))
    return mix_i, mix_row, mix_col, post_row, p_spec


STATE_W = 2 * DV


def _scan_chunk(q_ref, kt_ref, v_ref, og_ref, gt_ref, hg_ref, cn_scr, m_scr, r0):
    L = CHUNK
    rows = slice(r0, r0 + L)
    ri = lax.broadcasted_iota(jnp.int32, (L, L), 0)
    ci = lax.broadcasted_iota(jnp.int32, (L, L), 1)
    tri = ci <= ri
    gt = gt_ref[:, rows]
    ig_rows, lf_rows, b_rows = gt[:HEADS], gt[HEADS:2 * HEADS], gt[2 * HEADS:]
    a_rows = ig_rows - b_rows
    m_prev = m_scr[...]
    b_last = jnp.broadcast_to(b_rows[:, L - 1:L], (HEADS, L))
    m_new = b_last + jnp.maximum(m_prev, jnp.max(a_rows, axis=1, keepdims=True))
    decay_rows = jnp.exp(b_last + m_prev - m_new)
    wk_rows = jnp.exp(b_last - b_rows + ig_rows - m_new)
    m_scr[...] = m_new

    ones_rhs = jnp.ones((L, DV), BF16)
    zeros_kt = jnp.zeros((DQK, L), BF16)
    zeros_cn = jnp.zeros((DQK, STATE_W), BF16)
    for j in range(HEADS // 2):
        q_pair = q_ref[rows, j * 128:(j + 1) * 128]
        kt_pair = [kt_ref[h * DQK:(h + 1) * DQK, rows] for h in (2 * j, 2 * j + 1)]
        s_pair = _dot(q_pair, jnp.concatenate([jnp.concatenate([kt_pair[0], zeros_kt], axis=0),
                                               jnp.concatenate([zeros_kt, kt_pair[1]], axis=0)], axis=1))
        for par in range(2):
            h = 2 * j + par
            kt_h = kt_pair[par]
            v_ext = jnp.concatenate([v_ref[rows, h * DV:(h + 1) * DV], ones_rhs], axis=1)
            cn = cn_scr[h]
            cn16 = cn.astype(BF16)
            a_row, m_row = a_rows[h:h + 1], m_prev[h:h + 1]

            big_m = jnp.maximum(jnp.max(jnp.where(tri, a_row, -jnp.inf), axis=1, keepdims=True), m_row)
            b_col = jnp.sum(jnp.where(tri, lf_rows[h:h + 1], 0.0), axis=1, keepdims=True)
            w_inter = jnp.exp(m_row - big_m)
            s = s_pair[:, par * L:(par + 1) * L] * jnp.exp(jnp.where(tri, a_row - big_m, NEG))
            q_w = (q_pair.astype(F32) * w_inter).astype(BF16)
            state = jnp.concatenate([cn16, zeros_cn] if par == 0 else [zeros_cn, cn16], axis=0)
            nd = _dot(jnp.concatenate([s.astype(BF16), q_w], axis=1), jnp.concatenate([v_ext, state], axis=0))
            hh = nd[:, :DV] * (1.0 / jnp.maximum(jnp.abs(nd[:, DV:]), jnp.exp(-(b_col + big_m))))

            kw_t = kt_h.astype(F32) * wk_rows[h:h + 1]
            dec = decay_rows[h:h + 1]
            cn_scr[h] = jnp.concatenate([dec, dec], axis=1) * cn + _dot(kw_t.astype(BF16), v_ext)

            hg_ref[rows, h * DV:(h + 1) * DV] = (_rms(hh) * og_ref[rows, h * DV:(h + 1) * DV]).astype(hg_ref.dtype)


def _mlstm_post_body(q_ref, kt_ref, v_ref, og_ref, gt_ref, h_ref, p_ref, *rest, tiles_per_seq):
    w_refs = rest[:-7]
    out_ref, c_out_ref, n_out_ref, m_out_ref, hg_scr, cn_scr, m_scr = rest[-7:]
    s = pl.program_id(0)
    nt = pl.num_programs(0) - 1
    tile = jnp.minimum(s, nt - 1)

    @pl.when((tile % tiles_per_seq == 0) & (s < nt))
    def _():
        cn_scr[...] = jnp.zeros(cn_scr.shape, F32)
        m_scr[...] = jnp.zeros(m_scr.shape, F32)

    def post():
        out_ref[...] = _post_rows(hg_scr[(s + 1) % 2], h_ref[...], p_ref[...], *w_refs)

    def mixer():
        hg_slot = hg_scr.at[s % 2]
        for r0 in range(0, q_ref.shape[0], CHUNK):
            _scan_chunk(q_ref, kt_ref, v_ref, og_ref, gt_ref, hg_slot, cn_scr, m_scr, r0)

    _skewed_step(s, nt, mixer, post)

    @pl.when((tile % tiles_per_seq == tiles_per_seq - 1) & (s < nt))
    def _():
        for h in range(HEADS):
            c_out_ref[0, h] = cn_scr[h, :, 0:DV]
            n_out_ref[0, h] = cn_scr[h, :, DV:STATE_W]
        m_out_ref[0] = m_scr[...]


def _mlstm_post(q, kt, v, og, gt, h, p, wo, w, *, layer, mixer, batch, seq):
    t = batch * seq
    tm = ROW_TILE
    nt = t // tm
    tiles_per_seq = seq // tm
    mix_i, mix_row, mix_col, post_row, p_spec = _pipelined_specs(tm, nt, layer)
    seq_i = lambda s: mix_i(s) // tiles_per_seq
    w_specs, w_args = _post_weights(wo, w, layer, mixer)
    return pl.pallas_call(
        functools.partial(_mlstm_post_body, tiles_per_seq=tiles_per_seq),
        grid=(nt + 1,),
        in_specs=[mix_row(HK), mix_col(HK), mix_row(HV), mix_row(D_MODEL), mix_col(3 * HEADS),
                  post_row(D_MODEL), p_spec] + w_specs,
        out_specs=[post_row(D_MODEL),
                   pl.BlockSpec((1, HEADS, DQK, DV), lambda s: (seq_i(s), 0, 0, 0)),
                   pl.BlockSpec((1, HEADS, DQK, STATE_W - DV), lambda s: (seq_i(s), 0, 0, 0)),
                   pl.BlockSpec((1, HEADS, LANES), lambda s: (seq_i(s), 0, 0))],
        out_shape=[jax.ShapeDtypeStruct((t, D_MODEL), F32),
                   jax.ShapeDtypeStruct((batch, HEADS, DQK, DV), F32),
                   jax.ShapeDtypeStruct((batch, HEADS, DQK, STATE_W - DV), F32),
                   jax.ShapeDtypeStruct((batch, HEADS, LANES), F32)],
        scratch_shapes=[pltpu.VMEM((2, tm, HV), BF16), pltpu.VMEM((HEADS, DQK, STATE_W), F32),
                        pltpu.VMEM((HEADS, LANES), F32)],
        compiler_params=_params(1), name="mlstm_post",
    )(q, kt, v, og, gt, h, p, *w_args)


STEP_BB = 8
_TN = (((0,), (0,)), ((), ()))


def _mlstm_step_body(c_ref, n_ref, m_ref, q_ref, k_ref, v_ref, og_ref, g_ref,
                     hg_ref, c_out_ref, n_out_ref, m_out_ref, *, own):
    bb = STEP_BB
    for slab in range(c_out_ref.shape[0]):
        if slab != own:
            c_out_ref[slab] = jnp.zeros(c_out_ref.shape[1:], F32)
    g = g_ref[...]
    ig, lf, m_prev = g[:, :HEADS], g[:, HEADS:], m_ref[...]
    inter = lf + m_prev
    m_t = jnp.maximum(inter, ig)
    w_inter = jnp.exp(inter - m_t)
    in_w = jnp.exp(ig - m_t)
    decay = jnp.exp(lf + m_prev - m_t)
    floor = jnp.exp(-m_t)
    m_out_ref[...] = m_t
    diag = (lax.broadcasted_iota(jnp.int32, (bb, bb * DV), 0)
            == lax.broadcasted_iota(jnp.int32, (bb, bb * DV), 1) // DV)
    row_v = lax.broadcasted_iota(jnp.int32, (bb, DV), 0)

    for h in range(HEADS):
        col = lambda a: a[:, h:h + 1]
        q = q_ref[:, h * DQK:(h + 1) * DQK]
        k = k_ref[:, h * DQK:(h + 1) * DQK]
        v = v_ref[:, h * DV:(h + 1) * DV]
        n = n_ref[:, h, :]
        s = jnp.sum(q * k, axis=1, keepdims=True) * col(in_w)
        kw = k * col(in_w)
        c_tiles = [c_ref[i, h * DQK:(h + 1) * DQK, :] for i in range(bb)]
        qc_all = _dot(q.astype(BF16), jnp.concatenate(c_tiles, axis=1).astype(BF16))
        v_bd = jnp.where(diag, jnp.concatenate([v] * bb, axis=1), 0.0).astype(BF16)
        outer = lax.dot_general(kw.astype(BF16), v_bd, _TN, preferred_element_type=F32)
        qc = jnp.zeros((bb, DV), F32)
        for i in range(bb):
            qc = jnp.where(row_v == i, qc_all[:, i * DV:(i + 1) * DV], qc)
            c_out_ref[own, i, h * DQK:(h + 1) * DQK, :] = (decay[i:i + 1, h:h + 1] * c_tiles[i]
                                                           + outer[:, i * DV:(i + 1) * DV])
        num = col(w_inter) * qc + s * v
        den = col(w_inter) * jnp.sum(q * n, axis=1, keepdims=True) + s
        hh = num * (1.0 / jnp.maximum(jnp.abs(den), col(floor)))
        n_out_ref[:, h, :] = col(decay) * n + kw
        hg_ref[:, h * DV:(h + 1) * DV] = _rms(hh) * og_ref[:, h * DV:(h + 1) * DV]


def _mlstm_step(c_all, c_new_prev, n, m, q, k, v, og, g, *, mixer):
    n_slabs, b = c_all.shape[:2]
    bb = STEP_BB
    rows = lambda width: pl.BlockSpec((bb, width), lambda i: (i, 0))
    cspec = pl.BlockSpec((None, bb, HK, DV), lambda i: (mixer, i, 0, 0))
    nspec = pl.BlockSpec((bb, HEADS, DQK), lambda i: (i, 0, 0))
    in_specs = [cspec, nspec, rows(HEADS), rows(HK), rows(HK), rows(HV), rows(D_MODEL), rows(2 * HEADS)]
    args = [c_all, n, m, q, k, v, og, g]
    if c_new_prev is None:
        c_out_spec = pl.BlockSpec((n_slabs, bb, HK, DV), lambda i: (0, i, 0, 0))
        body, aliases = functools.partial(_mlstm_step_body, own=mixer), {}
    else:
        c_out_spec = pl.BlockSpec((1, bb, HK, DV), lambda i: (mixer, i, 0, 0))
        in_specs.append(pl.BlockSpec(memory_space=pl.ANY))
        body, aliases = _skip_ref(functools.partial(_mlstm_step_body, own=0), len(args)), {len(args): 1}
        args.append(c_new_prev)
    return pl.pallas_call(
        body,
        grid=(b // bb,),
        in_specs=in_specs,
        out_specs=[rows(HV), c_out_spec, nspec, rows(HEADS)],
        out_shape=[jax.ShapeDtypeStruct((b, HV), F32), jax.ShapeDtypeStruct(c_all.shape, F32),
                   jax.ShapeDtypeStruct(n.shape, F32), jax.ShapeDtypeStruct((b, HEADS), F32)],
        input_output_aliases=aliases,
        compiler_params=_params(1), name="mlstm_step",
    )(*args)


def _bias_body(bucket_ref, rel_ref, out_ref):
    bucket = bucket_ref[...]
    for h in range(Q_HEADS):
        acc = jnp.zeros(bucket.shape, F32)
        for b in range(REL_BUCKETS):
            acc = jnp.where(bucket == b, rel_ref[b, h], acc)
        out_ref[h] = acc


def _bias_table(bucket, rel_bias):
    shape = bucket.shape
    return pl.pallas_call(
        _bias_body,
        in_specs=[pl.BlockSpec(memory_space=pltpu.VMEM), pl.BlockSpec(memory_space=pltpu.SMEM)],
        out_specs=pl.BlockSpec(memory_space=pltpu.VMEM),
        out_shape=jax.ShapeDtypeStruct((Q_HEADS,) + shape, F32),
        name="rel_bias_table",
    )(bucket, rel_bias)


def _rel_bucket(dist):
    max_exact = REL_BUCKETS // 2
    d = jnp.maximum(dist, 0)
    df = jnp.maximum(d, 1).astype(F32)
    large = max_exact + (jnp.log(df / max_exact) / math.log(REL_MAX_DIST / max_exact)
                         * (REL_BUCKETS - max_exact)).astype(jnp.int32)
    large = jnp.minimum(large, REL_BUCKETS - 1)
    return jnp.where(d < max_exact, d, large)


def _swa_block(q, kt_prev, kt_cur, v_prev, v_cur, bias_ref, sink_ref, in_range, out_ref, rows):
    blk = WINDOW
    qi = lax.broadcasted_iota(jnp.int32, (blk, blk), 0)
    kj = lax.broadcasted_iota(jnp.int32, (blk, blk), 1)
    from_prev = kj > qi
    lane = lax.broadcasted_iota(jnp.int32, (2 * blk, LANES), 1)
    lane_q = lax.broadcasted_iota(jnp.int32, (blk, LANES), 1)
    zero_k = jnp.zeros((HEAD_DIM, 2 * blk), BF16)
    ones_bd = jnp.concatenate([(lane < HEAD_DIM).astype(BF16), (lane >= HEAD_DIM).astype(BF16)], axis=0)

    for g in range(KV_HEADS):
        hd_rows = slice(g * HEAD_DIM, (g + 1) * HEAD_DIM)
        kt_g = jnp.concatenate([kt_prev[hd_rows, :], kt_cur[hd_rows, :]], axis=1)
        rhs = jnp.concatenate([jnp.concatenate([kt_g, zero_k], axis=0),
                               jnp.concatenate([zero_k, kt_g], axis=0)], axis=1)
        v_lanes = slice((g // 2) * LANES, (g // 2 + 1) * LANES)
        vt = jnp.concatenate([v_prev[:, v_lanes], v_cur[:, v_lanes]], axis=0)
        vr = pltpu.roll(vt, HEAD_DIM, axis=1)
        v_lo = jnp.where(lane < HEAD_DIM, vt if g % 2 == 0 else vr, 0.0)
        v_hi = jnp.where(lane >= HEAD_DIM, vr if g % 2 == 0 else vt, 0.0)
        v_bd = jnp.concatenate([jnp.concatenate([v_lo, v_hi], axis=0).astype(BF16), ones_bd], axis=1)
        for pair in range(GROUP // 2):
            jp = g * (GROUP // 2) + pair
            logits = _dot(q[:, jp * LANES:(jp + 1) * LANES], rhs)
            ps, sk = [], []
            for i in range(2):
                hd = 2 * jp + i
                lg = jnp.where(from_prev, logits[:, 2 * i * blk:(2 * i + 1) * blk],
                               logits[:, (2 * i + 1) * blk:(2 * i + 2) * blk]) + bias_ref[hd]
                if in_range is not None:
                    lg = jnp.where(in_range, lg, NEG)
                sink = sink_ref[hd]
                mx = jnp.maximum(jnp.max(lg, axis=1, keepdims=True), sink)
                p = jnp.exp(lg - mx)
                ps += [jnp.where(from_prev, p, 0.0).astype(BF16), jnp.where(from_prev, 0.0, p).astype(BF16)]
                sk.append(jnp.exp(sink - mx))
            pv = _dot(jnp.concatenate(ps, axis=1), v_bd)
            den = pv[:, LANES:] + jnp.where(lane_q < HEAD_DIM, sk[0], sk[1])
            out_ref[rows, jp * LANES:(jp + 1) * LANES] = (pv[:, :LANES] * (1.0 / den)).astype(out_ref.dtype)


def _swa_post_body(q_ref, kt_ref, ktp_ref, v_ref, vp_ref, bias_ref, sink_ref, h_ref, p_ref, *rest, tiles_per_seq):
    w_refs, out_ref, attn_scr = rest[:-2], rest[-2], rest[-1]
    s = pl.program_id(0)
    nt = pl.num_programs(0) - 1
    tile = jnp.minimum(s, nt - 1)

    def post():
        out_ref[...] = _post_rows(attn_scr[(s + 1) % 2], h_ref[...], p_ref[...], *w_refs)

    def mixer():
        attn_slot = attn_scr.at[s % 2]
        blk = WINDOW
        qi = lax.broadcasted_iota(jnp.int32, (blk, blk), 0)
        kj = lax.broadcasted_iota(jnp.int32, (blk, blk), 1)
        has_prev = (kj <= qi) | (tile % tiles_per_seq != 0)
        for b in range(q_ref.shape[0] // blk):
            rows, prev_rows = slice(b * blk, (b + 1) * blk), slice((b - 1) * blk, b * blk)
            _swa_block(q_ref[rows, :],
                       ktp_ref[...] if b == 0 else kt_ref[:, prev_rows], kt_ref[:, rows],
                       vp_ref[...] if b == 0 else v_ref[prev_rows, :], v_ref[rows, :],
                       bias_ref, sink_ref, has_prev if b == 0 else None, attn_slot, rows)

    _skewed_step(s, nt, mixer, post)


def _swa_post(q, kt, v, bias, sinks, h, p, wo, w, *, layer, mixer, batch, seq):
    t = batch * seq
    tm = ROW_TILE
    nt = t // tm
    blocks = tm // WINDOW
    mix_i, mix_row, mix_col, post_row, p_spec = _pipelined_specs(tm, nt, layer)
    prev_blk = lambda s: jnp.maximum(mix_i(s) * blocks - 1, 0)
    w_specs, w_args = _post_weights(wo, w, layer, mixer)
    return pl.pallas_call(
        functools.partial(_swa_post_body, tiles_per_seq=seq // tm),
        grid=(nt + 1,),
        in_specs=[mix_row(D_MODEL), mix_col(KV_DIM), pl.BlockSpec((KV_DIM, WINDOW), lambda s: (0, prev_blk(s))),
                  mix_row(KV_DIM), pl.BlockSpec((WINDOW, KV_DIM), lambda s: (prev_blk(s), 0)),
                  _resident(bias.shape), pl.BlockSpec(memory_space=pltpu.SMEM), post_row(D_MODEL), p_spec] + w_specs,
        out_specs=post_row(D_MODEL),
        out_shape=jax.ShapeDtypeStruct((t, D_MODEL), F32),
        scratch_shapes=[pltpu.VMEM((2, tm, D_MODEL), BF16)],
        compiler_params=_params(1), name="swa_post",
    )(q, kt, kt, v, v, bias, sinks[mixer], h, p, *w_args)


SAMPLE_BB = 8


def _swa_sample_body(q_ref, kc_ref, vc_ref, kn_ref, vn_ref, bias_ref, sink_ref, out_ref, ko_ref, vo_ref):
    w = WINDOW
    row = lax.broadcasted_iota(jnp.int32, (w, KV_DIM), 0)
    rg = lax.broadcasted_iota(jnp.int32, (Q_HEADS, KV_DIM), 0) // GROUP
    lg = lax.broadcasted_iota(jnp.int32, (Q_HEADS, KV_DIM), 1) // HEAD_DIM
    rsel = lax.broadcasted_iota(jnp.int32, (Q_HEADS, HEAD_DIM), 0) // GROUP
    logits, v_wins = [], []
    for i in range(SAMPLE_BB):
        k_win = jnp.where(row == w - 1, kn_ref[i], pltpu.roll(kc_ref[i], w - 1, axis=0))
        v_win = jnp.where(row == w - 1, vn_ref[i], pltpu.roll(vc_ref[i], w - 1, axis=0))
        ko_ref[i] = k_win
        vo_ref[i] = v_win
        qb = q_ref[i].astype(F32)
        q_bd = jnp.where(rg == lg, jnp.concatenate([qb] * KV_HEADS, axis=1), 0.0).astype(BF16)
        logits.append(lax.dot_general(q_bd, k_win.astype(BF16), _NT, preferred_element_type=F32))
        v_wins.append(v_win.astype(BF16))
    lg_all = jnp.concatenate(logits, axis=0) + bias_ref[...]
    sink = sink_ref[...]
    mx = jnp.maximum(jnp.max(lg_all, axis=1, keepdims=True), sink)
    p = jnp.exp(lg_all - mx)
    probs = (p * (1.0 / (jnp.sum(p, axis=1, keepdims=True) + jnp.exp(sink - mx)))).astype(BF16)
    for i in range(SAMPLE_BB):
        o = _dot(probs[i * Q_HEADS:(i + 1) * Q_HEADS], v_wins[i])
        sel = o[:, 0:HEAD_DIM]
        for g in range(1, KV_HEADS):
            sel = jnp.where(rsel == g, o[:, g * HEAD_DIM:(g + 1) * HEAD_DIM], sel)
        out_ref[i] = sel.astype(out_ref.dtype)


def _swa_sample(q, kc_all, vc_all, kn, vn, bias, sinks, *, mixer):
    b = q.shape[0]
    bb = SAMPLE_BB
    blk = lambda shape: pl.BlockSpec((bb,) + shape, lambda i: (i, 0, 0))
    cache = pl.BlockSpec((None, bb, WINDOW, KV_DIM), lambda i: (mixer, i, 0, 0))
    return pl.pallas_call(
        _swa_sample_body,
        grid=(b // bb,),
        in_specs=[blk((Q_HEADS, HEAD_DIM)), cache, cache, blk((1, KV_DIM)), blk((1, KV_DIM)), _resident(bias.shape),
                  _layer(sinks, mixer)],
        out_specs=[blk((Q_HEADS, HEAD_DIM)), cache, cache],
        out_shape=[jax.ShapeDtypeStruct((b, Q_HEADS, HEAD_DIM), BF16), jax.ShapeDtypeStruct(kc_all.shape, F32),
                   jax.ShapeDtypeStruct(vc_all.shape, F32)],
        input_output_aliases={1: 1, 2: 2},
        compiler_params=_params(1), name="swa_sample",
    )(q, kc_all, vc_all, kn, vn, bias, sinks)


def kernel(x_prompt, x_sample, state_mlstm_C, state_mlstm_n, state_mlstm_m, cache_swa_k, cache_swa_v, p_prompt, p_sample, rel_bias, norm_pre_mix, norm_post_mix, norm_pre_ffn, norm_post_ffn, mlstm_w_in, mlstm_b_i, mlstm_b_f, mlstm_norm, mlstm_w_out, swa_w_qkv, swa_sinks, swa_w_o, ffn_w_up, ffn_w_down, ple_w_proj, ple_norm, ple_w_gate):
    bp, sp, _ = x_prompt.shape
    bs = x_sample.shape[0]
    win = cache_swa_k.shape[2]
    assert win == WINDOW and x_sample.shape[1] == 1
    bf = lambda a: a.astype(BF16)
    vecs = lambda a: a.astype(F32)[:, None, :]

    post_w = dict(npm=vecs(norm_post_mix), npf=vecs(norm_pre_ffn), wup=bf(ffn_w_up), wdn=bf(ffn_w_down),
                  npff=vecs(norm_post_ffn), wpj=bf(ple_w_proj), pln=vecs(ple_norm), wgate=bf(ple_w_gate))
    n_pre = vecs(norm_pre_mix)
    m_w = bf(mlstm_w_in)
    m_bias = jnp.concatenate([mlstm_b_i, mlstm_b_f], axis=1).astype(F32)
    m_wout, m_norm = bf(mlstm_w_out), vecs(mlstm_norm)
    s_w, s_wo = bf(swa_w_qkv), bf(swa_w_o)
    sinks = swa_sinks.astype(F32)

    qi = jnp.arange(WINDOW)[:, None]
    kj = jnp.arange(WINDOW)[None, :]
    dist_p = jnp.where(kj > qi, qi + WINDOW - kj, qi - kj)
    bias_p = _bias_table(_rel_bucket(dist_p).astype(jnp.int32), rel_bias.astype(F32))
    dist_s = (win - 1) - jnp.arange(win)
    bias_s = _bias_table(jnp.broadcast_to(_rel_bucket(dist_s)[None, :], (8, win)).astype(jnp.int32),
                         rel_bias.astype(F32))[:, 0, :]
    bias_s = jnp.tile(bias_s, (SAMPLE_BB, 1))
    sinks_s = jnp.tile(sinks[:, :, None], (1, SAMPLE_BB, 1))

    hp = x_prompt.reshape(bp * sp, D_MODEL)
    hs = x_sample.reshape(bs, D_MODEL)
    pp = p_prompt.reshape(DEPTH, bp * sp, PLE_DIM)
    ps = p_sample.reshape(DEPTH, bs, PLE_DIM)
    c_old = state_mlstm_C.astype(F32).reshape(-1, bs, HK, DV)
    k_win = cache_swa_k.astype(F32).reshape(-1, bs, win, KV_DIM)
    v_win = cache_swa_v.astype(F32).reshape(-1, bs, win, KV_DIM)
    c_new = None
    outs = dict(Cp=[], np=[], mp=[], kp=[], vp=[], ns=[], ms=[])

    for i in range(DEPTH):
        j = i // N_MIXERS
        if i % N_MIXERS == 0:
            wo = m_wout
            q, kt, v, og, gt = _mlstm_in(hp, n_pre, m_w, m_bias[j].reshape(-1, 1), m_norm,
                                         layer=i, mixer=j, prompt=True, act_dtype=BF16)
            hp, c_fin, n_fin, m_fin = _mlstm_post(q, kt, v, og, gt, hp, pp, wo, post_w,
                                                  layer=i, mixer=j, batch=bp, seq=sp)
            outs["Cp"].append(c_fin)
            outs["np"].append(n_fin[..., 0])
            outs["mp"].append(m_fin[..., 0])
            q, k, v, og, g = _mlstm_in(hs, n_pre, m_w, m_bias[j].reshape(1, -1), m_norm,
                                       layer=i, mixer=j, prompt=False, act_dtype=F32)
            a_s, c_new, n_new, m_new = _mlstm_step(
                c_old, c_new, state_mlstm_n[j].astype(F32), state_mlstm_m[j].astype(F32),
                q, k, v, og, g, mixer=j)
            outs["ns"].append(n_new)
            outs["ms"].append(m_new)
        else:
            wo = s_wo
            q, k, kt, v = _swa_in(hp, n_pre, s_w, layer=i, mixer=j)
            hp = _swa_post(q, kt, v, bias_p, sinks, hp, pp, wo, post_w, layer=i, mixer=j, batch=bp, seq=sp)
            last = lambda a: a.reshape(bp, sp, KV_DIM)[:, sp - win:].reshape(bp, win, KV_HEADS, HEAD_DIM)
            outs["kp"].append(last(k))
            outs["vp"].append(last(v))
            q, k, _, v = _swa_in(hs, n_pre, s_w, layer=i, mixer=j)
            a_s, k_win, v_win = _swa_sample(
                q.reshape(bs, Q_HEADS, HEAD_DIM), k_win, v_win, k.reshape(bs, 1, KV_DIM), v.reshape(bs, 1, KV_DIM),
                bias_s, sinks_s, mixer=j)
            a_s = a_s.reshape(bs, D_MODEL)
        hs = _post(a_s, hs, ps, wo, post_w, layer=i, mixer=j)

    st = lambda key, like: jnp.stack(outs[key]).astype(like.dtype)
    kv_out = lambda a, like: a.reshape(-1, bs, win, KV_HEADS, HEAD_DIM).astype(like.dtype)
    return (hp.reshape(x_prompt.shape), hs.reshape(x_sample.shape),
            st("Cp", state_mlstm_C), st("np", state_mlstm_n), st("mp", state_mlstm_m),
            st("kp", cache_swa_k), st("vp", cache_swa_v),
            c_new.reshape(state_mlstm_C.shape).astype(state_mlstm_C.dtype), st("ns", state_mlstm_n),
            st("ms", state_mlstm_m), kv_out(k_win, cache_swa_k), kv_out(v_win, cache_swa_v))
```

```python
import functools
import math

import jax
import jax.numpy as jnp
from jax import lax
from jax.experimental import pallas as pl
from jax.experimental.pallas import tpu as pltpu

F32 = jnp.float32
BF16 = jnp.bfloat16

D_MODEL = 1024
DEPTH = 4
N_MIXERS = 2
HEADS = 8
DQK = 64
DV = 128
CHUNK = 128
Q_HEADS = 16
KV_HEADS = 4
GROUP = Q_HEADS // KV_HEADS
HEAD_DIM = 64
KV_DIM = KV_HEADS * HEAD_DIM
WINDOW = 128
REL_BUCKETS = 32
REL_MAX_DIST = 128
PLE_DIM = 256
D_FF = 4 * D_MODEL
EPS = 1e-6
NEG = -1e30

HK, HV = HEADS * DQK, HEADS * DV
Q0, K0, V0, O0, G0 = 0, HK, 2 * HK, 2 * HK + HV, 2 * HK + HV + D_MODEL
MLSTM_IN = G0 + 2 * HEADS

V7X_VMEM_BYTES = 64 * 1024 * 1024
VMEM_LIMIT = V7X_VMEM_BYTES - 8 * 1024 * 1024
LANES = 128
ROW_TILE = 512

_NT = (((1,), (1,)), ((), ()))


def _row_tile(t):
    return ROW_TILE if t % ROW_TILE == 0 else t


def _params(n_grid):
    return pltpu.CompilerParams(dimension_semantics=("arbitrary",) * n_grid, vmem_limit_bytes=VMEM_LIMIT)


def _resident(shape):
    nd = len(shape)
    return pl.BlockSpec(shape, lambda *_: (0,) * nd, pipeline_mode=pl.Buffered(1))


def _layer(arr, layer):
    tail = arr.shape[1:]
    return pl.BlockSpec((None,) + tail, lambda *_: (layer,) + (0,) * len(tail), pipeline_mode=pl.Buffered(1))


def _skip_ref(body, pos):
    def wrapped(*refs):
        return body(*refs[:pos], *refs[pos + 1:])
    return wrapped


def _rms(x, w=None):
    y = x * lax.rsqrt(jnp.mean(x * x, axis=-1, keepdims=True) + EPS)
    return y if w is None else y * w


def _log_sigmoid(x):
    return -(jnp.maximum(-x, 0.0) + jnp.log1p(jnp.exp(-jnp.abs(x))))


def _sigmoid(x):
    return 1.0 / (1.0 + jnp.exp(-x))


def _dot(a, b):
    return jnp.dot(a, b, preferred_element_type=F32)


def _mlstm_in_body(h_ref, nw_ref, w_ref, bg_ref, hnw_ref, q_ref, k_ref, v_ref, o_ref, g_ref, *scratch, prompt):
    if prompt:
        wkgt_scr, = scratch

        @pl.when(pl.program_id(0) == 0)
        def _():
            wkgt_scr[0:HK, :] = w_ref[:, K0:V0].astype(F32).T.astype(BF16)
            wg = jnp.concatenate([w_ref[:, G0:MLSTM_IN].astype(F32),
                                  jnp.zeros((D_MODEL, LANES - 2 * HEADS), F32)], axis=1)
            wkgt_scr[HK:, :] = wg.T.astype(BF16)

    u = _rms(h_ref[...], nw_ref[...]).astype(BF16)
    q_ref[...] = (_dot(u, w_ref[:, Q0:K0]) * (DQK ** -0.5)).astype(q_ref.dtype)
    v_ref[...] = _dot(u, w_ref[:, V0:O0]).astype(v_ref.dtype)
    o_ref[...] = _sigmoid(_dot(u, w_ref[:, O0:G0])) * hnw_ref[...]
    if prompt:
        kgt = lax.dot_general(wkgt_scr[...], u, _NT, preferred_element_type=F32)
        k_ref[...] = kgt[:HK].astype(k_ref.dtype)
        gt = kgt[HK:HK + 2 * HEADS] + bg_ref[...]
        ig, lf = gt[:HEADS], _log_sigmoid(gt[HEADS:])
        r = lax.broadcasted_iota(jnp.int32, (CHUNK, CHUNK), 0)
        t = lax.broadcasted_iota(jnp.int32, (CHUNK, CHUNK), 1)
        upper = (r <= t).astype(F32)
        b = jnp.concatenate([jnp.dot(lf[:, c:c + CHUNK], upper, precision=lax.Precision.HIGHEST,
                                     preferred_element_type=F32) for c in range(0, lf.shape[1], CHUNK)], axis=1)
        g_ref[...] = jnp.concatenate([ig, lf, b], axis=0)
    else:
        k_ref[...] = _dot(u, w_ref[:, K0:V0]).astype(k_ref.dtype)
        g = _dot(u, w_ref[:, G0:MLSTM_IN]) + bg_ref[...]
        lane = lax.broadcasted_iota(jnp.int32, g.shape, 1)
        g_ref[...] = jnp.where(lane < HEADS, g, _log_sigmoid(g))


def _mlstm_in(h, nw, w, bg, hnw, *, layer, mixer, prompt, act_dtype):
    t = h.shape[0]
    tm = _row_tile(t)
    row = lambda n: pl.BlockSpec((tm, n), lambda i: (i, 0))
    col = lambda n: pl.BlockSpec((None, n, tm), lambda i: (i, 0, 0))
    kshape, kspec = ((t // tm, HK, tm), col(HK)) if prompt else ((t, HK), row(HK))
    gshape, gspec = ((t // tm, 3 * HEADS, tm), col(3 * HEADS)) if prompt else ((t, 2 * HEADS), row(2 * HEADS))
    return pl.pallas_call(
        functools.partial(_mlstm_in_body, prompt=prompt),
        grid=(t // tm,),
        in_specs=[row(D_MODEL), _layer(nw, layer), _layer(w, mixer), _resident(bg.shape), _layer(hnw, mixer)],
        out_specs=[row(HK), kspec, row(HV), row(D_MODEL), gspec],
        out_shape=[jax.ShapeDtypeStruct((t, HK), act_dtype), jax.ShapeDtypeStruct(kshape, act_dtype),
                   jax.ShapeDtypeStruct((t, HV), act_dtype), jax.ShapeDtypeStruct((t, D_MODEL), F32),
                   jax.ShapeDtypeStruct(gshape, F32)],
        scratch_shapes=[pltpu.VMEM((HK + LANES, D_MODEL), BF16)] if prompt else [],
        compiler_params=_params(1), name="mlstm_in",
    )(h, nw, w, bg, hnw)


def _swa_in_body(h_ref, nw_ref, w_ref, q_ref, k_ref, kt_ref, v_ref):
    u = _rms(h_ref[...], nw_ref[...]).astype(BF16)
    q_ref[...] = (_dot(u, w_ref[:, :D_MODEL]) * (HEAD_DIM ** -0.5)).astype(q_ref.dtype)
    k = _dot(u, w_ref[:, D_MODEL:D_MODEL + KV_DIM])
    k_ref[...] = k
    kt_ref[...] = k.T.astype(kt_ref.dtype)
    v_ref[...] = _dot(u, w_ref[:, D_MODEL + KV_DIM:])


def _swa_in(h, nw, w, *, layer, mixer):
    t = h.shape[0]
    tm = _row_tile(t)
    row = lambda n: pl.BlockSpec((tm, n), lambda i: (i, 0))
    return pl.pallas_call(
        _swa_in_body,
        grid=(t // tm,),
        in_specs=[row(D_MODEL), _layer(nw, layer), _layer(w, mixer)],
        out_specs=[row(D_MODEL), row(KV_DIM), pl.BlockSpec((None, KV_DIM, tm), lambda i: (i, 0, 0)), row(KV_DIM)],
        out_shape=[jax.ShapeDtypeStruct((t, D_MODEL), BF16), jax.ShapeDtypeStruct((t, KV_DIM), F32),
                   jax.ShapeDtypeStruct((t // tm, KV_DIM, tm), BF16),
                   jax.ShapeDtypeStruct((t, KV_DIM), F32)],
        compiler_params=_params(1), name="swa_in",
    )(h, nw, w)


FF_CHUNK = 1024


def _post_rows(a, h, p, wo_ref, npm_ref, npf_ref, wup_ref, wdn_ref, npff_ref, wpj_ref, pln_ref, wgate_ref):
    y = _dot(a.astype(BF16), wo_ref[...])
    h1 = h + _rms(y, npm_ref[...])
    u = _rms(h1, npf_ref[...]).astype(BF16)
    acc = jnp.zeros(h1.shape, F32)
    for c in range(D_FF // FF_CHUNK):
        hid = jnp.maximum(_dot(u, wup_ref[:, c * FF_CHUNK:(c + 1) * FF_CHUNK]), 0.0)
        acc = acc + _dot((hid * hid).astype(BF16), wdn_ref[c * FF_CHUNK:(c + 1) * FF_CHUNK, :])
    h2 = h1 + _rms(acc, npff_ref[...])
    e = _rms(_dot(p.astype(BF16), wpj_ref[...]), pln_ref[...])
    g = _sigmoid(_dot(_rms(h2).astype(BF16), wgate_ref[...]))
    return h2 + e * g


def _post_body(a_ref, h_ref, p_ref, *rest):
    w_refs, out_ref = rest[:-1], rest[-1]
    out_ref[...] = _post_rows(a_ref[...], h_ref[...], p_ref[...], *w_refs)


_POST_KEYS = ("npm", "npf", "wup", "wdn", "npff", "wpj", "pln", "wgate")


def _post_weights(wo, w, layer, mixer):
    return ([_layer(wo, mixer)] + [_layer(w[k], layer) for k in _POST_KEYS], [wo] + [w[k] for k in _POST_KEYS])


def _post(a, h, p, wo, w, *, layer, mixer):
    t = h.shape[0]
    tm = _row_tile(t)
    row = lambda n: pl.BlockSpec((tm, n), lambda i: (i, 0))
    w_specs, w_args = _post_weights(wo, w, layer, mixer)
    return pl.pallas_call(
        _post_body,
        grid=(t // tm,),
        in_specs=[row(D_MODEL), row(D_MODEL), pl.BlockSpec((None, tm, PLE_DIM), lambda i: (layer, i, 0))] + w_specs,
        out_specs=row(D_MODEL),
        out_shape=jax.ShapeDtypeStruct((t, D_MODEL), F32),
        compiler_params=_params(1), name="post",
    )(a, h, p, *w_args)


def _skewed_step(s, nt, mixer, post):
    @pl.when(s == 0)
    def _():
        mixer()

    @pl.when((s > 0) & (s < nt))
    def _():
        post()
        mixer()

    @pl.when(s == nt)
    def _():
        post()


def _pipelined_specs(tm, nt, layer):
    mix_i = lambda s: jnp.minimum(s, nt - 1)
    post_i = lambda s: jnp.maximum(s - 1, 0)
    mix_row = lambda n: pl.BlockSpec((tm, n), lambda s: (mix_i(s), 0))
    mix_col = lambda n: pl.BlockSpec((None, n, tm), lambda s: (mix_i(s), 0, 0))
    post_row = lambda n: pl.BlockSpec((tm, n), lambda s: (post_i(s), 0))
    p_spec = pl.BlockSpec((None, tm, PLE_DIM), lambda s: (layer, post_i(s), 0))
    return mix_i, mix_row, mix_col, post_row, p_spec


STATE_W = 2 * DV


def _scan_chunk(q_ref, kt_ref, v_ref, og_ref, gt_ref, hg_ref, cn_scr, m_scr, r0):
    L = CHUNK
    rows = slice(r0, r0 + L)
    ri = lax.broadcasted_iota(jnp.int32, (L, L), 0)
    ci = lax.broadcasted_iota(jnp.int32, (L, L), 1)
    tri = ci <= ri
    gt = gt_ref[:, rows]
    ig_rows, lf_rows, b_rows = gt[:HEADS], gt[HEADS:2 * HEADS], gt[2 * HEADS:]
    a_rows = ig_rows - b_rows
    m_prev = m_scr[...]
    b_last = jnp.broadcast_to(b_rows[:, L - 1:L], (HEADS, L))
    m_new = b_last + jnp.maximum(m_prev, jnp.max(a_rows, axis=1, keepdims=True))
    decay_rows = jnp.exp(b_last + m_prev - m_new)
    wk_rows = jnp.exp(b_last - b_rows + ig_rows - m_new)
    m_scr[...] = m_new

    ones_rhs = jnp.ones((L, DV), BF16)
    zeros_kt = jnp.zeros((DQK, L), BF16)
    zeros_cn = jnp.zeros((DQK, STATE_W), BF16)
    for j in range(HEADS // 2):
        q_pair = q_ref[rows, j * 128:(j + 1) * 128]
        kt_pair = [kt_ref[h * DQK:(h + 1) * DQK, rows] for h in (2 * j, 2 * j + 1)]
        s_pair = _dot(q_pair, jnp.concatenate([jnp.concatenate([kt_pair[0], zeros_kt], axis=0),
                                               jnp.concatenate([zeros_kt, kt_pair[1]], axis=0)], axis=1))
        for par in range(2):
            h = 2 * j + par
            kt_h = kt_pair[par]
            v_ext = jnp.concatenate([v_ref[rows, h * DV:(h + 1) * DV], ones_rhs], axis=1)
            cn = cn_scr[h]
            cn16 = cn.astype(BF16)
            a_row, m_row = a_rows[h:h + 1], m_prev[h:h + 1]

            big_m = jnp.maximum(jnp.max(jnp.where(tri, a_row, -jnp.inf), axis=1, keepdims=True), m_row)
            b_col = jnp.sum(jnp.where(tri, lf_rows[h:h + 1], 0.0), axis=1, keepdims=True)
            w_inter = jnp.exp(m_row - big_m)
            s = s_pair[:, par * L:(par + 1) * L] * jnp.exp(jnp.where(tri, a_row - big_m, NEG))
            q_w = (q_pair.astype(F32) * w_inter).astype(BF16)
            state = jnp.concatenate([cn16, zeros_cn] if par == 0 else [zeros_cn, cn16], axis=0)
            nd = _dot(jnp.concatenate([s.astype(BF16), q_w], axis=1), jnp.concatenate([v_ext, state], axis=0))
            hh = nd[:, :DV] * (1.0 / jnp.maximum(jnp.abs(nd[:, DV:]), jnp.exp(-(b_col + big_m))))

            kw_t = kt_h.astype(F32) * wk_rows[h:h + 1]
            dec = decay_rows[h:h + 1]
            cn_scr[h] = jnp.concatenate([dec, dec], axis=1) * cn + _dot(kw_t.astype(BF16), v_ext)

            hg_ref[rows, h * DV:(h + 1) * DV] = (_rms(hh) * og_ref[rows, h * DV:(h + 1) * DV]).astype(hg_ref.dtype)


def _mlstm_post_body(q_ref, kt_ref, v_ref, og_ref, gt_ref, h_ref, p_ref, *rest, tiles_per_seq):
    w_refs = rest[:-7]
    out_ref, c_out_ref, n_out_ref, m_out_ref, hg_scr, cn_scr, m_scr = rest[-7:]
    s = pl.program_id(0)
    nt = pl.num_programs(0) - 1
    tile = jnp.minimum(s, nt - 1)

    @pl.when((tile % tiles_per_seq == 0) & (s < nt))
    def _():
        cn_scr[...] = jnp.zeros(cn_scr.shape, F32)
        m_scr[...] = jnp.zeros(m_scr.shape, F32)

    def post():
        out_ref[...] = _post_rows(hg_scr[(s + 1) % 2], h_ref[...], p_ref[...], *w_refs)

    def mixer():
        hg_slot = hg_scr.at[s % 2]
        for r0 in range(0, q_ref.shape[0], CHUNK):
            _scan_chunk(q_ref, kt_ref, v_ref, og_ref, gt_ref, hg_slot, cn_scr, m_scr, r0)

    _skewed_step(s, nt, mixer, post)

    @pl.when((tile % tiles_per_seq == tiles_per_seq - 1) & (s < nt))
    def _():
        for h in range(HEADS):
            c_out_ref[0, h] = cn_scr[h, :, 0:DV]
            n_out_ref[0, h] = cn_scr[h, :, DV:STATE_W]
        m_out_ref[0] = m_scr[...]


def _mlstm_post(q, kt, v, og, gt, h, p, wo, w, *, layer, mixer, batch, seq):
    t = batch * seq
    tm = ROW_TILE
    nt = t // tm
    tiles_per_seq = seq // tm
    mix_i, mix_row, mix_col, post_row, p_spec = _pipelined_specs(tm, nt, layer)
    seq_i = lambda s: mix_i(s) // tiles_per_seq
    w_specs, w_args = _post_weights(wo, w, layer, mixer)
    return pl.pallas_call(
        functools.partial(_mlstm_post_body, tiles_per_seq=tiles_per_seq),
        grid=(nt + 1,),
        in_specs=[mix_row(HK), mix_col(HK), mix_row(HV), mix_row(D_MODEL), mix_col(3 * HEADS),
                  post_row(D_MODEL), p_spec] + w_specs,
        out_specs=[post_row(D_MODEL),
                   pl.BlockSpec((1, HEADS, DQK, DV), lambda s: (seq_i(s), 0, 0, 0)),
                   pl.BlockSpec((1, HEADS, DQK, STATE_W - DV), lambda s: (seq_i(s), 0, 0, 0)),
                   pl.BlockSpec((1, HEADS, LANES), lambda s: (seq_i(s), 0, 0))],
        out_shape=[jax.ShapeDtypeStruct((t, D_MODEL), F32),
                   jax.ShapeDtypeStruct((batch, HEADS, DQK, DV), F32),
                   jax.ShapeDtypeStruct((batch, HEADS, DQK, STATE_W - DV), F32),
                   jax.ShapeDtypeStruct((batch, HEADS, LANES), F32)],
        scratch_shapes=[pltpu.VMEM((2, tm, HV), BF16), pltpu.VMEM((HEADS, DQK, STATE_W), F32),
                        pltpu.VMEM((HEADS, LANES), F32)],
        compiler_params=_params(1), name="mlstm_post",
    )(q, kt, v, og, gt, h, p, *w_args)


STEP_BB = 8
_TN = (((0,), (0,)), ((), ()))


def _mlstm_step_body(c_ref, n_ref, m_ref, q_ref, k_ref, v_ref, og_ref, g_ref,
                     hg_ref, c_out_ref, n_out_ref, m_out_ref, *, own):
    bb = STEP_BB
    for slab in range(c_out_ref.shape[0]):
        if slab != own:
            c_out_ref[slab] = jnp.zeros(c_out_ref.shape[1:], F32)
    g = g_ref[...]
    ig, lf, m_prev = g[:, :HEADS], g[:, HEADS:], m_ref[...]
    inter = lf + m_prev
    m_t = jnp.maximum(inter, ig)
    w_inter = jnp.exp(inter - m_t)
    in_w = jnp.exp(ig - m_t)
    decay = jnp.exp(lf + m_prev - m_t)
    floor = jnp.exp(-m_t)
    m_out_ref[...] = m_t
    diag = (lax.broadcasted_iota(jnp.int32, (bb, bb * DV), 0)
            == lax.broadcasted_iota(jnp.int32, (bb, bb * DV), 1) // DV)
    row_v = lax.broadcasted_iota(jnp.int32, (bb, DV), 0)

    for h in range(HEADS):
        col = lambda a: a[:, h:h + 1]
        q = q_ref[:, h * DQK:(h + 1) * DQK]
        k = k_ref[:, h * DQK:(h + 1) * DQK]
        v = v_ref[:, h * DV:(h + 1) * DV]
        n = n_ref[:, h, :]
        s = jnp.sum(q * k, axis=1, keepdims=True) * col(in_w)
        kw = k * col(in_w)
        c_tiles = [c_ref[i, h * DQK:(h + 1) * DQK, :] for i in range(bb)]
        qc_all = _dot(q.astype(BF16), jnp.concatenate(c_tiles, axis=1).astype(BF16))
        v_bd = jnp.where(diag, jnp.concatenate([v] * bb, axis=1), 0.0).astype(BF16)
        outer = lax.dot_general(kw.astype(BF16), v_bd, _TN, preferred_element_type=F32)
        qc = jnp.zeros((bb, DV), F32)
        for i in range(bb):
            qc = jnp.where(row_v == i, qc_all[:, i * DV:(i + 1) * DV], qc)
            c_out_ref[own, i, h * DQK:(h + 1) * DQK, :] = (decay[i:i + 1, h:h + 1] * c_tiles[i]
                                                           + outer[:, i * DV:(i + 1) * DV])
        num = col(w_inter) * qc + s * v
        den = col(w_inter) * jnp.sum(q * n, axis=1, keepdims=True) + s
        hh = num * (1.0 / jnp.maximum(jnp.abs(den), col(floor)))
        n_out_ref[:, h, :] = col(decay) * n + kw
        hg_ref[:, h * DV:(h + 1) * DV] = _rms(hh) * og_ref[:, h * DV:(h + 1) * DV]


def _mlstm_step(c_all, c_new_prev, n, m, q, k, v, og, g, *, mixer):
    n_slabs, b = c_all.shape[:2]
    bb = STEP_BB
    rows = lambda width: pl.BlockSpec((bb, width), lambda i: (i, 0))
    cspec = pl.BlockSpec((None, bb, HK, DV), lambda i: (mixer, i, 0, 0))
    nspec = pl.BlockSpec((bb, HEADS, DQK), lambda i: (i, 0, 0))
    in_specs = [cspec, nspec, rows(HEADS), rows(HK), rows(HK), rows(HV), rows(D_MODEL), rows(2 * HEADS)]
    args = [c_all, n, m, q, k, v, og, g]
    if c_new_prev is None:
        c_out_spec = pl.BlockSpec((n_slabs, bb, HK, DV), lambda i: (0, i, 0, 0))
        body, aliases = functools.partial(_mlstm_step_body, own=mixer), {}
    else:
        c_out_spec = pl.BlockSpec((1, bb, HK, DV), lambda i: (mixer, i, 0, 0))
        in_specs.append(pl.BlockSpec(memory_space=pl.ANY))
        body, aliases = _skip_ref(functools.partial(_mlstm_step_body, own=0), len(args)), {len(args): 1}
        args.append(c_new_prev)
    return pl.pallas_call(
        body,
        grid=(b // bb,),
        in_specs=in_specs,
        out_specs=[rows(HV), c_out_spec, nspec, rows(HEADS)],
        out_shape=[jax.ShapeDtypeStruct((b, HV), F32), jax.ShapeDtypeStruct(c_all.shape, F32),
                   jax.ShapeDtypeStruct(n.shape, F32), jax.ShapeDtypeStruct((b, HEADS), F32)],
        input_output_aliases=aliases,
        compiler_params=_params(1), name="mlstm_step",
    )(*args)


def _bias_body(bucket_ref, rel_ref, out_ref):
    bucket = bucket_ref[...]
    for h in range(Q_HEADS):
        acc = jnp.zeros(bucket.shape, F32)
        for b in range(REL_BUCKETS):
            acc = jnp.where(bucket == b, rel_ref[b, h], acc)
        out_ref[h] = acc


def _bias_table(bucket, rel_bias):
    shape = bucket.shape
    return pl.pallas_call(
        _bias_body,
        in_specs=[pl.BlockSpec(memory_space=pltpu.VMEM), pl.BlockSpec(memory_space=pltpu.SMEM)],
        out_specs=pl.BlockSpec(memory_space=pltpu.VMEM),
        out_shape=jax.ShapeDtypeStruct((Q_HEADS,) + shape, F32),
        name="rel_bias_table",
    )(bucket, rel_bias)


def _rel_bucket(dist):
    max_exact = REL_BUCKETS // 2
    d = jnp.maximum(dist, 0)
    df = jnp.maximum(d, 1).astype(F32)
    large = max_exact + (jnp.log(df / max_exact) / math.log(REL_MAX_DIST / max_exact)
                         * (REL_BUCKETS - max_exact)).astype(jnp.int32)
    large = jnp.minimum(large, REL_BUCKETS - 1)
    return jnp.where(d < max_exact, d, large)


def _swa_block(q, kt_prev, kt_cur, v_prev, v_cur, bias_ref, sink_ref, in_range, out_ref, rows):
    blk = WINDOW
    qi = lax.broadcasted_iota(jnp.int32, (blk, blk), 0)
    kj = lax.broadcasted_iota(jnp.int32, (blk, blk), 1)
    from_prev = kj > qi
    lane = lax.broadcasted_iota(jnp.int32, (2 * blk, LANES), 1)
    lane_q = lax.broadcasted_iota(jnp.int32, (blk, LANES), 1)
    zero_k = jnp.zeros((HEAD_DIM, 2 * blk), BF16)
    ones_bd = jnp.concatenate([(lane < HEAD_DIM).astype(BF16), (lane >= HEAD_DIM).astype(BF16)], axis=0)

    for g in range(KV_HEADS):
        hd_rows = slice(g * HEAD_DIM, (g + 1) * HEAD_DIM)
        kt_g = jnp.concatenate([kt_prev[hd_rows, :], kt_cur[hd_rows, :]], axis=1)
        rhs = jnp.concatenate([jnp.concatenate([kt_g, zero_k], axis=0),
                               jnp.concatenate([zero_k, kt_g], axis=0)], axis=1)
        v_lanes = slice((g // 2) * LANES, (g // 2 + 1) * LANES)
        vt = jnp.concatenate([v_prev[:, v_lanes], v_cur[:, v_lanes]], axis=0)
        vr = pltpu.roll(vt, HEAD_DIM, axis=1)
        v_lo = jnp.where(lane < HEAD_DIM, vt if g % 2 == 0 else vr, 0.0)
        v_hi = jnp.where(lane >= HEAD_DIM, vr if g % 2 == 0 else vt, 0.0)
        v_bd = jnp.concatenate([jnp.concatenate([v_lo, v_hi], axis=0).astype(BF16), ones_bd], axis=1)
        for pair in range(GROUP // 2):
            jp = g * (GROUP // 2) + pair
            logits = _dot(q[:, jp * LANES:(jp + 1) * LANES], rhs)
            ps, sk = [], []
            for i in range(2):
                hd = 2 * jp + i
                lg = jnp.where(from_prev, logits[:, 2 * i * blk:(2 * i + 1) * blk],
                               logits[:, (2 * i + 1) * blk:(2 * i + 2) * blk]) + bias_ref[hd]
                if in_range is not None:
                    lg = jnp.where(in_range, lg, NEG)
                sink = sink_ref[hd]
                mx = jnp.maximum(jnp.max(lg, axis=1, keepdims=True), sink)
                p = jnp.exp(lg - mx)
                ps += [jnp.where(from_prev, p, 0.0).astype(BF16), jnp.where(from_prev, 0.0, p).astype(BF16)]
                sk.append(jnp.exp(sink - mx))
            pv = _dot(jnp.concatenate(ps, axis=1), v_bd)
            den = pv[:, LANES:] + jnp.where(lane_q < HEAD_DIM, sk[0], sk[1])
            out_ref[rows, jp * LANES:(jp + 1) * LANES] = (pv[:, :LANES] * (1.0 / den)).astype(out_ref.dtype)


def _swa_post_body(q_ref, kt_ref, ktp_ref, v_ref, vp_ref, bias_ref, sink_ref, h_ref, p_ref, *rest, tiles_per_seq):
    w_refs, out_ref, attn_scr = rest[:-2], rest[-2], rest[-1]
    s = pl.program_id(0)
    nt = pl.num_programs(0) - 1
    tile = jnp.minimum(s, nt - 1)

    def post():
        out_ref[...] = _post_rows(attn_scr[(s + 1) % 2], h_ref[...], p_ref[...], *w_refs)

    def mixer():
        attn_slot = attn_scr.at[s % 2]
        blk = WINDOW
        qi = lax.broadcasted_iota(jnp.int32, (blk, blk), 0)
        kj = lax.broadcasted_iota(jnp.int32, (blk, blk), 1)
        has_prev = (kj <= qi) | (tile % tiles_per_seq != 0)
        for b in range(q_ref.shape[0] // blk):
            rows, prev_rows = slice(b * blk, (b + 1) * blk), slice((b - 1) * blk, b * blk)
            _swa_block(q_ref[rows, :],
                       ktp_ref[...] if b == 0 else kt_ref[:, prev_rows], kt_ref[:, rows],
                       vp_ref[...] if b == 0 else v_ref[prev_rows, :], v_ref[rows, :],
                       bias_ref, sink_ref, has_prev if b == 0 else None, attn_slot, rows)

    _skewed_step(s, nt, mixer, post)


def _swa_post(q, kt, v, bias, sinks, h, p, wo, w, *, layer, mixer, batch, seq):
    t = batch * seq
    tm = ROW_TILE
    nt = t // tm
    blocks = tm // WINDOW
    mix_i, mix_row, mix_col, post_row, p_spec = _pipelined_specs(tm, nt, layer)
    prev_tile = lambda s: jnp.maximum(mix_i(s) - 1, 0)
    prev_blk = lambda s: prev_tile(s) * blocks + (blocks - 1)
    w_specs, w_args = _post_weights(wo, w, layer, mixer)
    return pl.pallas_call(
        functools.partial(_swa_post_body, tiles_per_seq=seq // tm),
        grid=(nt + 1,),
        in_specs=[mix_row(D_MODEL), mix_col(KV_DIM),
                  pl.BlockSpec((None, KV_DIM, WINDOW), lambda s: (prev_tile(s), 0, blocks - 1)),
                  mix_row(KV_DIM), pl.BlockSpec((WINDOW, KV_DIM), lambda s: (prev_blk(s), 0)),
                  _resident(bias.shape), pl.BlockSpec(memory_space=pltpu.SMEM), post_row(D_MODEL), p_spec] + w_specs,
        out_specs=post_row(D_MODEL),
        out_shape=jax.ShapeDtypeStruct((t, D_MODEL), F32),
        scratch_shapes=[pltpu.VMEM((2, tm, D_MODEL), BF16)],
        compiler_params=_params(1), name="swa_post",
    )(q, kt, kt, v, v, bias, sinks[mixer], h, p, *w_args)


SAMPLE_BB = 8


def _swa_sample_body(q_ref, kc_ref, vc_ref, kn_ref, vn_ref, bias_ref, sink_ref, out_ref, ko_ref, vo_ref):
    w = WINDOW
    row = lax.broadcasted_iota(jnp.int32, (w, KV_DIM), 0)
    rg = lax.broadcasted_iota(jnp.int32, (Q_HEADS, KV_DIM), 0) // GROUP
    lg = lax.broadcasted_iota(jnp.int32, (Q_HEADS, KV_DIM), 1) // HEAD_DIM
    rsel = lax.broadcasted_iota(jnp.int32, (Q_HEADS, HEAD_DIM), 0) // GROUP
    logits, v_wins = [], []
    for i in range(SAMPLE_BB):
        k_win = jnp.where(row == w - 1, kn_ref[i], pltpu.roll(kc_ref[i], w - 1, axis=0))
        v_win = jnp.where(row == w - 1, vn_ref[i], pltpu.roll(vc_ref[i], w - 1, axis=0))
        ko_ref[i] = k_win
        vo_ref[i] = v_win
        qb = q_ref[i].astype(F32)
        q_bd = jnp.where(rg == lg, jnp.concatenate([qb] * KV_HEADS, axis=1), 0.0).astype(BF16)
        logits.append(lax.dot_general(q_bd, k_win.astype(BF16), _NT, preferred_element_type=F32))
        v_wins.append(v_win.astype(BF16))
    lg_all = jnp.concatenate(logits, axis=0) + bias_ref[...]
    sink = sink_ref[...]
    mx = jnp.maximum(jnp.max(lg_all, axis=1, keepdims=True), sink)
    p = jnp.exp(lg_all - mx)
    probs = (p * (1.0 / (jnp.sum(p, axis=1, keepdims=True) + jnp.exp(sink - mx)))).astype(BF16)
    for i in range(SAMPLE_BB):
        o = _dot(probs[i * Q_HEADS:(i + 1) * Q_HEADS], v_wins[i])
        sel = o[:, 0:HEAD_DIM]
        for g in range(1, KV_HEADS):
            sel = jnp.where(rsel == g, o[:, g * HEAD_DIM:(g + 1) * HEAD_DIM], sel)
        out_ref[i] = sel.astype(out_ref.dtype)


def _swa_sample(q, kc_all, vc_all, kn, vn, bias, sinks, *, mixer):
    b = q.shape[0]
    bb = SAMPLE_BB
    blk = lambda shape: pl.BlockSpec((bb,) + shape, lambda i: (i, 0, 0))
    cache = pl.BlockSpec((None, bb, WINDOW, KV_DIM), lambda i: (mixer, i, 0, 0))
    return pl.pallas_call(
        _swa_sample_body,
        grid=(b // bb,),
        in_specs=[blk((Q_HEADS, HEAD_DIM)), cache, cache, blk((1, KV_DIM)), blk((1, KV_DIM)), _resident(bias.shape),
                  _layer(sinks, mixer)],
        out_specs=[blk((Q_HEADS, HEAD_DIM)), cache, cache],
        out_shape=[jax.ShapeDtypeStruct((b, Q_HEADS, HEAD_DIM), BF16), jax.ShapeDtypeStruct(kc_all.shape, F32),
                   jax.ShapeDtypeStruct(vc_all.shape, F32)],
        input_output_aliases={1: 1, 2: 2},
        compiler_params=_params(1), name="swa_sample",
    )(q, kc_all, vc_all, kn, vn, bias, sinks)


def kernel(x_prompt, x_sample, state_mlstm_C, state_mlstm_n, state_mlstm_m, cache_swa_k, cache_swa_v, p_prompt, p_sample, rel_bias, norm_pre_mix, norm_post_mix, norm_pre_ffn, norm_post_ffn, mlstm_w_in, mlstm_b_i, mlstm_b_f, mlstm_norm, mlstm_w_out, swa_w_qkv, swa_sinks, swa_w_o, ffn_w_up, ffn_w_down, ple_w_proj, ple_norm, ple_w_gate):
    bp, sp, _ = x_prompt.shape
    bs = x_sample.shape[0]
    win = cache_swa_k.shape[2]
    assert win == WINDOW and x_sample.shape[1] == 1
    bf = lambda a: a.astype(BF16)
    vecs = lambda a: a.astype(F32)[:, None, :]

    post_w = dict(npm=vecs(norm_post_mix), npf=vecs(norm_pre_ffn), wup=bf(ffn_w_up), wdn=bf(ffn_w_down),
                  npff=vecs(norm_post_ffn), wpj=bf(ple_w_proj), pln=vecs(ple_norm), wgate=bf(ple_w_gate))
    n_pre = vecs(norm_pre_mix)
    m_w = bf(mlstm_w_in)
    m_bias = jnp.concatenate([mlstm_b_i, mlstm_b_f], axis=1).astype(F32)
    m_wout, m_norm = bf(mlstm_w_out), vecs(mlstm_norm)
    s_w, s_wo = bf(swa_w_qkv), bf(swa_w_o)
    sinks = swa_sinks.astype(F32)

    qi = jnp.arange(WINDOW)[:, None]
    kj = jnp.arange(WINDOW)[None, :]
    dist_p = jnp.where(kj > qi, qi + WINDOW - kj, qi - kj)
    bias_p = _bias_table(_rel_bucket(dist_p).astype(jnp.int32), rel_bias.astype(F32))
    dist_s = (win - 1) - jnp.arange(win)
    bias_s = _bias_table(jnp.broadcast_to(_rel_bucket(dist_s)[None, :], (8, win)).astype(jnp.int32),
                         rel_bias.astype(F32))[:, 0, :]
    bias_s = jnp.tile(bias_s, (SAMPLE_BB, 1))
    sinks_s = jnp.tile(sinks[:, :, None], (1, SAMPLE_BB, 1))

    hp = x_prompt.reshape(bp * sp, D_MODEL)
    hs = x_sample.reshape(bs, D_MODEL)
    pp = p_prompt.reshape(DEPTH, bp * sp, PLE_DIM)
    ps = p_sample.reshape(DEPTH, bs, PLE_DIM)
    c_old = state_mlstm_C.astype(F32).reshape(-1, bs, HK, DV)
    k_win = cache_swa_k.astype(F32).reshape(-1, bs, win, KV_DIM)
    v_win = cache_swa_v.astype(F32).reshape(-1, bs, win, KV_DIM)
    c_new = None
    outs = dict(Cp=[], np=[], mp=[], kp=[], vp=[], ns=[], ms=[])

    for i in range(DEPTH):
        j = i // N_MIXERS
        if i % N_MIXERS == 0:
            wo = m_wout
            q, kt, v, og, gt = _mlstm_in(hp, n_pre, m_w, m_bias[j].reshape(-1, 1), m_norm,
                                         layer=i, mixer=j, prompt=True, act_dtype=BF16)
            hp, c_fin, n_fin, m_fin = _mlstm_post(q, kt, v, og, gt, hp, pp, wo, post_w,
                                                  layer=i, mixer=j, batch=bp, seq=sp)
            outs["Cp"].append(c_fin)
            outs["np"].append(n_fin[..., 0])
            outs["mp"].append(m_fin[..., 0])
            q, k, v, og, g = _mlstm_in(hs, n_pre, m_w, m_bias[j].reshape(1, -1), m_norm,
                                       layer=i, mixer=j, prompt=False, act_dtype=F32)
            a_s, c_new, n_new, m_new = _mlstm_step(
                c_old, c_new, state_mlstm_n[j].astype(F32), state_mlstm_m[j].astype(F32),
                q, k, v, og, g, mixer=j)
            outs["ns"].append(n_new)
            outs["ms"].append(m_new)
        else:
            wo = s_wo
            q, k, kt, v = _swa_in(hp, n_pre, s_w, layer=i, mixer=j)
            hp = _swa_post(q, kt, v, bias_p, sinks, hp, pp, wo, post_w, layer=i, mixer=j, batch=bp, seq=sp)
            last = lambda a: a.reshape(bp, sp, KV_DIM)[:, sp - win:].reshape(bp, win, KV_HEADS, HEAD_DIM)
            outs["kp"].append(last(k))
            outs["vp"].append(last(v))
            q, k, _, v = _swa_in(hs, n_pre, s_w, layer=i, mixer=j)
            a_s, k_win, v_win = _swa_sample(
                q.reshape(bs, Q_HEADS, HEAD_DIM), k_win, v_win, k.reshape(bs, 1, KV_DIM), v.reshape(bs, 1, KV_DIM),
                bias_s, sinks_s, mixer=j)
            a_s = a_s.reshape(bs, D_MODEL)
        hs = _post(a_s, hs, ps, wo, post_w, layer=i, mixer=j)

    st = lambda key, like: jnp.stack(outs[key]).astype(like.dtype)
    kv_out = lambda a, like: a.reshape(-1, bs, win, KV_HEADS, HEAD_DIM).astype(like.dtype)
    return (hp.reshape(x_prompt.shape), hs.reshape(x_sample.shape),
            st("Cp", state_mlstm_C), st("np", state_mlstm_n), st("mp", state_mlstm_m),
            st("kp", cache_swa_k), st("vp", cache_swa_v),
            c_new.reshape(state_mlstm_C.shape).astype(state_mlstm_C.dtype), st("ns", state_mlstm_n),
            st("ms", state_mlstm_m), kv_out(k_win, cache_swa_k), kv_out(v_win, cache_swa_v))
```

```python
import functools
import math

import jax
import jax.numpy as jnp
from jax import lax
from jax.experimental import pallas as pl
from jax.experimental.pallas import tpu as pltpu

F32 = jnp.float32
BF16 = jnp.bfloat16

D_MODEL = 1024
DEPTH = 4
N_MIXERS = 2
HEADS = 8
DQK = 64
DV = 128
CHUNK = 128
Q_HEADS = 16
KV_HEADS = 4
GROUP = Q_HEADS // KV_HEADS
HEAD_DIM = 64
KV_DIM = KV_HEADS * HEAD_DIM
WINDOW = 128
REL_BUCKETS = 32
REL_MAX_DIST = 128
PLE_DIM = 256
D_FF = 4 * D_MODEL
EPS = 1e-6
NEG = -1e30

HK, HV = HEADS * DQK, HEADS * DV
Q0, K0, V0, O0, G0 = 0, HK, 2 * HK, 2 * HK + HV, 2 * HK + HV + D_MODEL
MLSTM_IN = G0 + 2 * HEADS

V7X_VMEM_BYTES = 64 * 1024 * 1024
VMEM_LIMIT = V7X_VMEM_BYTES - 8 * 1024 * 1024
LANES = 128
ROW_TILE = 512
IN_ROW_TILE = 1024

_NT = (((1,), (1,)), ((), ()))


def _row_tile(t, tile=ROW_TILE):
    return tile if t % tile == 0 else t


def _params(n_grid):
    return pltpu.CompilerParams(dimension_semantics=("arbitrary",) * n_grid, vmem_limit_bytes=VMEM_LIMIT)


def _resident(shape):
    nd = len(shape)
    return pl.BlockSpec(shape, lambda *_: (0,) * nd, pipeline_mode=pl.Buffered(1))


def _layer(arr, layer):
    tail = arr.shape[1:]
    return pl.BlockSpec((None,) + tail, lambda *_: (layer,) + (0,) * len(tail), pipeline_mode=pl.Buffered(1))


def _skip_ref(body, pos):
    def wrapped(*refs):
        return body(*refs[:pos], *refs[pos + 1:])
    return wrapped


def _rms(x, w=None):
    y = x * lax.rsqrt(jnp.mean(x * x, axis=-1, keepdims=True) + EPS)
    return y if w is None else y * w


def _log_sigmoid(x):
    return -(jnp.maximum(-x, 0.0) + jnp.log1p(jnp.exp(-jnp.abs(x))))


def _sigmoid(x):
    return 1.0 / (1.0 + jnp.exp(-x))


def _dot(a, b):
    return jnp.dot(a, b, preferred_element_type=F32)


def _mlstm_in_body(h_ref, nw_ref, w_ref, bg_ref, hnw_ref, q_ref, k_ref, v_ref, o_ref, g_ref, *scratch, prompt):
    if prompt:
        wkgt_scr, = scratch

        @pl.when(pl.program_id(0) == 0)
        def _():
            wkgt_scr[0:HK, :] = w_ref[:, K0:V0].astype(F32).T.astype(BF16)
            wg = jnp.concatenate([w_ref[:, G0:MLSTM_IN].astype(F32),
                                  jnp.zeros((D_MODEL, LANES - 2 * HEADS), F32)], axis=1)
            wkgt_scr[HK:, :] = wg.T.astype(BF16)

    u = _rms(h_ref[...], nw_ref[...]).astype(BF16)
    q_ref[...] = (_dot(u, w_ref[:, Q0:K0]) * (DQK ** -0.5)).astype(q_ref.dtype)
    v_ref[...] = _dot(u, w_ref[:, V0:O0]).astype(v_ref.dtype)
    o_ref[...] = _sigmoid(_dot(u, w_ref[:, O0:G0])) * hnw_ref[...]
    if prompt:
        kgt = lax.dot_general(wkgt_scr[...], u, _NT, preferred_element_type=F32)
        k_ref[...] = kgt[:HK].astype(k_ref.dtype)
        gt = kgt[HK:HK + 2 * HEADS] + bg_ref[...]
        ig, lf = gt[:HEADS], _log_sigmoid(gt[HEADS:])
        r = lax.broadcasted_iota(jnp.int32, (CHUNK, CHUNK), 0)
        t = lax.broadcasted_iota(jnp.int32, (CHUNK, CHUNK), 1)
        upper = (r <= t).astype(F32)
        b = jnp.concatenate([jnp.dot(lf[:, c:c + CHUNK], upper, precision=lax.Precision.HIGHEST,
                                     preferred_element_type=F32) for c in range(0, lf.shape[1], CHUNK)], axis=1)
        g_ref[...] = jnp.concatenate([ig, lf, b], axis=0)
    else:
        k_ref[...] = _dot(u, w_ref[:, K0:V0]).astype(k_ref.dtype)
        g = _dot(u, w_ref[:, G0:MLSTM_IN]) + bg_ref[...]
        lane = lax.broadcasted_iota(jnp.int32, g.shape, 1)
        g_ref[...] = jnp.where(lane < HEADS, g, _log_sigmoid(g))


def _mlstm_in(h, nw, w, bg, hnw, *, layer, mixer, prompt, act_dtype):
    t = h.shape[0]
    tm = _row_tile(t, IN_ROW_TILE)
    row = lambda n: pl.BlockSpec((tm, n), lambda i: (i, 0))
    col = lambda n: pl.BlockSpec((n, tm), lambda i: (0, i))
    kshape, kspec = ((HK, t), col(HK)) if prompt else ((t, HK), row(HK))
    gshape, gspec = ((3 * HEADS, t), col(3 * HEADS)) if prompt else ((t, 2 * HEADS), row(2 * HEADS))
    return pl.pallas_call(
        functools.partial(_mlstm_in_body, prompt=prompt),
        grid=(t // tm,),
        in_specs=[row(D_MODEL), _layer(nw, layer), _layer(w, mixer), _resident(bg.shape), _layer(hnw, mixer)],
        out_specs=[row(HK), kspec, row(HV), row(D_MODEL), gspec],
        out_shape=[jax.ShapeDtypeStruct((t, HK), act_dtype), jax.ShapeDtypeStruct(kshape, act_dtype),
                   jax.ShapeDtypeStruct((t, HV), act_dtype), jax.ShapeDtypeStruct((t, D_MODEL), F32),
                   jax.ShapeDtypeStruct(gshape, F32)],
        scratch_shapes=[pltpu.VMEM((HK + LANES, D_MODEL), BF16)] if prompt else [],
        compiler_params=_params(1), name="mlstm_in",
    )(h, nw, w, bg, hnw)


def _swa_in_body(h_ref, nw_ref, w_ref, q_ref, k_ref, kt_ref, v_ref):
    u = _rms(h_ref[...], nw_ref[...]).astype(BF16)
    q_ref[...] = (_dot(u, w_ref[:, :D_MODEL]) * (HEAD_DIM ** -0.5)).astype(q_ref.dtype)
    k = _dot(u, w_ref[:, D_MODEL:D_MODEL + KV_DIM])
    k_ref[...] = k
    kt_ref[...] = k.T.astype(kt_ref.dtype)
    v_ref[...] = _dot(u, w_ref[:, D_MODEL + KV_DIM:])


def _swa_in(h, nw, w, *, layer, mixer):
    t = h.shape[0]
    tm = _row_tile(t, IN_ROW_TILE)
    row = lambda n: pl.BlockSpec((tm, n), lambda i: (i, 0))
    return pl.pallas_call(
        _swa_in_body,
        grid=(t // tm,),
        in_specs=[row(D_MODEL), _layer(nw, layer), _layer(w, mixer)],
        out_specs=[row(D_MODEL), row(KV_DIM), pl.BlockSpec((KV_DIM, tm), lambda i: (0, i)), row(KV_DIM)],
        out_shape=[jax.ShapeDtypeStruct((t, D_MODEL), BF16), jax.ShapeDtypeStruct((t, KV_DIM), F32),
                   jax.ShapeDtypeStruct((KV_DIM, t), BF16), jax.ShapeDtypeStruct((t, KV_DIM), F32)],
        compiler_params=_params(1), name="swa_in",
    )(h, nw, w)


FF_CHUNK = 1024


def _post_rows(a, h, p, wo_ref, npm_ref, npf_ref, wup_ref, wdn_ref, npff_ref, wpj_ref, pln_ref, wgate_ref):
    y = _dot(a.astype(BF16), wo_ref[...])
    h1 = h + _rms(y, npm_ref[...])
    u = _rms(h1, npf_ref[...]).astype(BF16)
    acc = jnp.zeros(h1.shape, F32)
    for c in range(D_FF // FF_CHUNK):
        hid = jnp.maximum(_dot(u, wup_ref[:, c * FF_CHUNK:(c + 1) * FF_CHUNK]), 0.0)
        acc = acc + _dot((hid * hid).astype(BF16), wdn_ref[c * FF_CHUNK:(c + 1) * FF_CHUNK, :])
    h2 = h1 + _rms(acc, npff_ref[...])
    e = _rms(_dot(p.astype(BF16), wpj_ref[...]), pln_ref[...])
    g = _sigmoid(_dot(_rms(h2).astype(BF16), wgate_ref[...]))
    return h2 + e * g


def _post_body(a_ref, h_ref, p_ref, *rest):
    w_refs, out_ref = rest[:-1], rest[-1]
    out_ref[...] = _post_rows(a_ref[...], h_ref[...], p_ref[...], *w_refs)


_POST_KEYS = ("npm", "npf", "wup", "wdn", "npff", "wpj", "pln", "wgate")


def _post_weights(wo, w, layer, mixer):
    return ([_layer(wo, mixer)] + [_layer(w[k], layer) for k in _POST_KEYS], [wo] + [w[k] for k in _POST_KEYS])


def _post(a, h, p, wo, w, *, layer, mixer):
    t = h.shape[0]
    tm = _row_tile(t)
    row = lambda n: pl.BlockSpec((tm, n), lambda i: (i, 0))
    w_specs, w_args = _post_weights(wo, w, layer, mixer)
    return pl.pallas_call(
        _post_body,
        grid=(t // tm,),
        in_specs=[row(D_MODEL), row(D_MODEL), pl.BlockSpec((None, tm, PLE_DIM), lambda i: (layer, i, 0))] + w_specs,
        out_specs=row(D_MODEL),
        out_shape=jax.ShapeDtypeStruct((t, D_MODEL), F32),
        compiler_params=_params(1), name="post",
    )(a, h, p, *w_args)


def _skewed_step(s, nt, mixer, post):
    @pl.when(s == 0)
    def _():
        mixer()

    @pl.when((s > 0) & (s < nt))
    def _():
        post()
        mixer()

    @pl.when(s == nt)
    def _():
        post()


def _pipelined_specs(tm, nt, layer):
    mix_i = lambda s: jnp.minimum(s, nt - 1)
    post_i = lambda s: jnp.maximum(s - 1, 0)
    mix_row = lambda n: pl.BlockSpec((tm, n), lambda s: (mix_i(s), 0))
    mix_col = lambda n: pl.BlockSpec((n, tm), lambda s: (0, mix_i(s)))
    post_row = lambda n: pl.BlockSpec((tm, n), lambda s: (post_i(s), 0))
    p_spec = pl.BlockSpec((None, tm, PLE_DIM), lambda s: (layer, post_i(s), 0))
    return mix_i, mix_row, mix_col, post_row, p_spec


STATE_W = 2 * DV


def _scan_chunk(q_ref, kt_ref, v_ref, og_ref, gt_ref, hg_ref, cn_scr, m_scr, r0):
    L = CHUNK
    rows = slice(r0, r0 + L)
    ri = lax.broadcasted_iota(jnp.int32, (L, L), 0)
    ci = lax.broadcasted_iota(jnp.int32, (L, L), 1)
    tri = ci <= ri
    gt = gt_ref[:, rows]
    ig_rows, lf_rows, b_rows = gt[:HEADS], gt[HEADS:2 * HEADS], gt[2 * HEADS:]
    a_rows = ig_rows - b_rows
    m_prev = m_scr[...]
    b_last = jnp.broadcast_to(b_rows[:, L - 1:L], (HEADS, L))
    m_new = b_last + jnp.maximum(m_prev, jnp.max(a_rows, axis=1, keepdims=True))
    decay_rows = jnp.exp(b_last + m_prev - m_new)
    wk_rows = jnp.exp(b_last - b_rows + ig_rows - m_new)
    m_scr[...] = m_new

    ones_rhs = jnp.ones((L, DV), BF16)
    zeros_kt = jnp.zeros((DQK, L), BF16)
    zeros_cn = jnp.zeros((DQK, STATE_W), BF16)
    for j in range(HEADS // 2):
        q_pair = q_ref[rows, j * 128:(j + 1) * 128]
        kt_pair = [kt_ref[h * DQK:(h + 1) * DQK, rows] for h in (2 * j, 2 * j + 1)]
        s_pair = _dot(q_pair, jnp.concatenate([jnp.concatenate([kt_pair[0], zeros_kt], axis=0),
                                               jnp.concatenate([zeros_kt, kt_pair[1]], axis=0)], axis=1))
        for par in range(2):
            h = 2 * j + par
            kt_h = kt_pair[par]
            v_ext = jnp.concatenate([v_ref[rows, h * DV:(h + 1) * DV], ones_rhs], axis=1)
            cn = cn_scr[h]
            cn16 = cn.astype(BF16)
            a_row, m_row = a_rows[h:h + 1], m_prev[h:h + 1]

            big_m = jnp.maximum(jnp.max(jnp.where(tri, a_row, -jnp.inf), axis=1, keepdims=True), m_row)
            b_col = jnp.sum(jnp.where(tri, lf_rows[h:h + 1], 0.0), axis=1, keepdims=True)
            w_inter = jnp.exp(m_row - big_m)
            s = s_pair[:, par * L:(par + 1) * L] * jnp.exp(jnp.where(tri, a_row - big_m, NEG))
            q_w = (q_pair.astype(F32) * w_inter).astype(BF16)
            state = jnp.concatenate([cn16, zeros_cn] if par == 0 else [zeros_cn, cn16], axis=0)
            nd = _dot(jnp.concatenate([s.astype(BF16), q_w], axis=1), jnp.concatenate([v_ext, state], axis=0))
            hh = nd[:, :DV] * (1.0 / jnp.maximum(jnp.abs(nd[:, DV:]), jnp.exp(-(b_col + big_m))))

            kw_t = kt_h.astype(F32) * wk_rows[h:h + 1]
            dec = decay_rows[h:h + 1]
            cn_scr[h] = jnp.concatenate([dec, dec], axis=1) * cn + _dot(kw_t.astype(BF16), v_ext)

            hg_ref[rows, h * DV:(h + 1) * DV] = (_rms(hh) * og_ref[rows, h * DV:(h + 1) * DV]).astype(hg_ref.dtype)


def _mlstm_post_body(q_ref, kt_ref, v_ref, og_ref, gt_ref, h_ref, p_ref, *rest, tiles_per_seq):
    w_refs = rest[:-7]
    out_ref, c_out_ref, n_out_ref, m_out_ref, hg_scr, cn_scr, m_scr = rest[-7:]
    s = pl.program_id(0)
    nt = pl.num_programs(0) - 1
    tile = jnp.minimum(s, nt - 1)

    @pl.when((tile % tiles_per_seq == 0) & (s < nt))
    def _():
        cn_scr[...] = jnp.zeros(cn_scr.shape, F32)
        m_scr[...] = jnp.zeros(m_scr.shape, F32)

    def post():
        out_ref[...] = _post_rows(hg_scr[(s + 1) % 2], h_ref[...], p_ref[...], *w_refs)

    def mixer():
        hg_slot = hg_scr.at[s % 2]
        for r0 in range(0, q_ref.shape[0], CHUNK):
            _scan_chunk(q_ref, kt_ref, v_ref, og_ref, gt_ref, hg_slot, cn_scr, m_scr, r0)

    _skewed_step(s, nt, mixer, post)

    @pl.when((tile % tiles_per_seq == tiles_per_seq - 1) & (s < nt))
    def _():
        for h in range(HEADS):
            c_out_ref[0, h] = cn_scr[h, :, 0:DV]
            n_out_ref[0, h] = cn_scr[h, :, DV:STATE_W]
        m_out_ref[0] = m_scr[...]


def _mlstm_post(q, kt, v, og, gt, h, p, wo, w, *, layer, mixer, batch, seq):
    t = batch * seq
    tm = ROW_TILE
    nt = t // tm
    tiles_per_seq = seq // tm
    mix_i, mix_row, mix_col, post_row, p_spec = _pipelined_specs(tm, nt, layer)
    seq_i = lambda s: mix_i(s) // tiles_per_seq
    w_specs, w_args = _post_weights(wo, w, layer, mixer)
    return pl.pallas_call(
        functools.partial(_mlstm_post_body, tiles_per_seq=tiles_per_seq),
        grid=(nt + 1,),
        in_specs=[mix_row(HK), mix_col(HK), mix_row(HV), mix_row(D_MODEL), mix_col(3 * HEADS),
                  post_row(D_MODEL), p_spec] + w_specs,
        out_specs=[post_row(D_MODEL),
                   pl.BlockSpec((1, HEADS, DQK, DV), lambda s: (seq_i(s), 0, 0, 0)),
                   pl.BlockSpec((1, HEADS, DQK, STATE_W - DV), lambda s: (seq_i(s), 0, 0, 0)),
                   pl.BlockSpec((1, HEADS, LANES), lambda s: (seq_i(s), 0, 0))],
        out_shape=[jax.ShapeDtypeStruct((t, D_MODEL), F32),
                   jax.ShapeDtypeStruct((batch, HEADS, DQK, DV), F32),
                   jax.ShapeDtypeStruct((batch, HEADS, DQK, STATE_W - DV), F32),
                   jax.ShapeDtypeStruct((batch, HEADS, LANES), F32)],
        scratch_shapes=[pltpu.VMEM((2, tm, HV), BF16), pltpu.VMEM((HEADS, DQK, STATE_W), F32),
                        pltpu.VMEM((HEADS, LANES), F32)],
        compiler_params=_params(1), name="mlstm_post",
    )(q, kt, v, og, gt, h, p, *w_args)


STEP_BB = 8
_TN = (((0,), (0,)), ((), ()))


def _mlstm_step_body(c_ref, n_ref, m_ref, q_ref, k_ref, v_ref, og_ref, g_ref,
                     hg_ref, c_out_ref, n_out_ref, m_out_ref, *, own):
    bb = STEP_BB
    for slab in range(c_out_ref.shape[0]):
        if slab != own:
            c_out_ref[slab] = jnp.zeros(c_out_ref.shape[1:], F32)
    g = g_ref[...]
    ig, lf, m_prev = g[:, :HEADS], g[:, HEADS:], m_ref[...]
    inter = lf + m_prev
    m_t = jnp.maximum(inter, ig)
    w_inter = jnp.exp(inter - m_t)
    in_w = jnp.exp(ig - m_t)
    decay = jnp.exp(lf + m_prev - m_t)
    floor = jnp.exp(-m_t)
    m_out_ref[...] = m_t
    diag = (lax.broadcasted_iota(jnp.int32, (bb, bb * DV), 0)
            == lax.broadcasted_iota(jnp.int32, (bb, bb * DV), 1) // DV)
    row_v = lax.broadcasted_iota(jnp.int32, (bb, DV), 0)

    for h in range(HEADS):
        col = lambda a: a[:, h:h + 1]
        q = q_ref[:, h * DQK:(h + 1) * DQK]
        k = k_ref[:, h * DQK:(h + 1) * DQK]
        v = v_ref[:, h * DV:(h + 1) * DV]
        n = n_ref[:, h, :]
        s = jnp.sum(q * k, axis=1, keepdims=True) * col(in_w)
        kw = k * col(in_w)
        c_tiles = [c_ref[i, h * DQK:(h + 1) * DQK, :] for i in range(bb)]
        qc_all = _dot(q.astype(BF16), jnp.concatenate(c_tiles, axis=1).astype(BF16))
        v_bd = jnp.where(diag, jnp.concatenate([v] * bb, axis=1), 0.0).astype(BF16)
        outer = lax.dot_general(kw.astype(BF16), v_bd, _TN, preferred_element_type=F32)
        qc = jnp.zeros((bb, DV), F32)
        for i in range(bb):
            qc = jnp.where(row_v == i, qc_all[:, i * DV:(i + 1) * DV], qc)
            c_out_ref[own, i, h * DQK:(h + 1) * DQK, :] = (decay[i:i + 1, h:h + 1] * c_tiles[i]
                                                           + outer[:, i * DV:(i + 1) * DV])
        num = col(w_inter) * qc + s * v
        den = col(w_inter) * jnp.sum(q * n, axis=1, keepdims=True) + s
        hh = num * (1.0 / jnp.maximum(jnp.abs(den), col(floor)))
        n_out_ref[:, h, :] = col(decay) * n + kw
        hg_ref[:, h * DV:(h + 1) * DV] = _rms(hh) * og_ref[:, h * DV:(h + 1) * DV]


def _mlstm_step(c_all, c_new_prev, n, m, q, k, v, og, g, *, mixer):
    n_slabs, b = c_all.shape[:2]
    bb = STEP_BB
    rows = lambda width: pl.BlockSpec((bb, width), lambda i: (i, 0))
    cspec = pl.BlockSpec((None, bb, HK, DV), lambda i: (mixer, i, 0, 0))
    nspec = pl.BlockSpec((bb, HEADS, DQK), lambda i: (i, 0, 0))
    in_specs = [cspec, nspec, rows(HEADS), rows(HK), rows(HK), rows(HV), rows(D_MODEL), rows(2 * HEADS)]
    args = [c_all, n, m, q, k, v, og, g]
    if c_new_prev is None:
        c_out_spec = pl.BlockSpec((n_slabs, bb, HK, DV), lambda i: (0, i, 0, 0))
        body, aliases = functools.partial(_mlstm_step_body, own=mixer), {}
    else:
        c_out_spec = pl.BlockSpec((1, bb, HK, DV), lambda i: (mixer, i, 0, 0))
        in_specs.append(pl.BlockSpec(memory_space=pl.ANY))
        body, aliases = _skip_ref(functools.partial(_mlstm_step_body, own=0), len(args)), {len(args): 1}
        args.append(c_new_prev)
    return pl.pallas_call(
        body,
        grid=(b // bb,),
        in_specs=in_specs,
        out_specs=[rows(HV), c_out_spec, nspec, rows(HEADS)],
        out_shape=[jax.ShapeDtypeStruct((b, HV), F32), jax.ShapeDtypeStruct(c_all.shape, F32),
                   jax.ShapeDtypeStruct(n.shape, F32), jax.ShapeDtypeStruct((b, HEADS), F32)],
        input_output_aliases=aliases,
        compiler_params=_params(1), name="mlstm_step",
    )(*args)


def _bias_body(bucket_ref, rel_ref, out_ref):
    bucket = bucket_ref[...]
    for h in range(Q_HEADS):
        acc = jnp.zeros(bucket.shape, F32)
        for b in range(REL_BUCKETS):
            acc = jnp.where(bucket == b, rel_ref[b, h], acc)
        out_ref[h] = acc


def _bias_table(bucket, rel_bias):
    shape = bucket.shape
    return pl.pallas_call(
        _bias_body,
        in_specs=[pl.BlockSpec(memory_space=pltpu.VMEM), pl.BlockSpec(memory_space=pltpu.SMEM)],
        out_specs=pl.BlockSpec(memory_space=pltpu.VMEM),
        out_shape=jax.ShapeDtypeStruct((Q_HEADS,) + shape, F32),
        name="rel_bias_table",
    )(bucket, rel_bias)


def _rel_bucket(dist):
    max_exact = REL_BUCKETS // 2
    d = jnp.maximum(dist, 0)
    df = jnp.maximum(d, 1).astype(F32)
    large = max_exact + (jnp.log(df / max_exact) / math.log(REL_MAX_DIST / max_exact)
                         * (REL_BUCKETS - max_exact)).astype(jnp.int32)
    large = jnp.minimum(large, REL_BUCKETS - 1)
    return jnp.where(d < max_exact, d, large)


def _swa_block(q, kt_prev, kt_cur, v_prev, v_cur, bias_ref, sink_ref, in_range, out_ref, rows):
    blk = WINDOW
    qi = lax.broadcasted_iota(jnp.int32, (blk, blk), 0)
    kj = lax.broadcasted_iota(jnp.int32, (blk, blk), 1)
    from_prev = kj > qi
    lane = lax.broadcasted_iota(jnp.int32, (2 * blk, LANES), 1)
    lane_q = lax.broadcasted_iota(jnp.int32, (blk, LANES), 1)
    zero_k = jnp.zeros((HEAD_DIM, 2 * blk), BF16)
    ones_bd = jnp.concatenate([(lane < HEAD_DIM).astype(BF16), (lane >= HEAD_DIM).astype(BF16)], axis=0)

    for g in range(KV_HEADS):
        hd_rows = slice(g * HEAD_DIM, (g + 1) * HEAD_DIM)
        kt_g = jnp.concatenate([kt_prev[hd_rows, :], kt_cur[hd_rows, :]], axis=1)
        rhs = jnp.concatenate([jnp.concatenate([kt_g, zero_k], axis=0),
                               jnp.concatenate([zero_k, kt_g], axis=0)], axis=1)
        v_lanes = slice((g // 2) * LANES, (g // 2 + 1) * LANES)
        vt = jnp.concatenate([v_prev[:, v_lanes], v_cur[:, v_lanes]], axis=0)
        vr = pltpu.roll(vt, HEAD_DIM, axis=1)
        v_lo = jnp.where(lane < HEAD_DIM, vt if g % 2 == 0 else vr, 0.0)
        v_hi = jnp.where(lane >= HEAD_DIM, vr if g % 2 == 0 else vt, 0.0)
        v_bd = jnp.concatenate([jnp.concatenate([v_lo, v_hi], axis=0).astype(BF16), ones_bd], axis=1)
        for pair in range(GROUP // 2):
            jp = g * (GROUP // 2) + pair
            logits = _dot(q[:, jp * LANES:(jp + 1) * LANES], rhs)
            ps, sk = [], []
            for i in range(2):
                hd = 2 * jp + i
                lg = jnp.where(from_prev, logits[:, 2 * i * blk:(2 * i + 1) * blk],
                               logits[:, (2 * i + 1) * blk:(2 * i + 2) * blk]) + bias_ref[hd]
                if in_range is not None:
                    lg = jnp.where(in_range, lg, NEG)
                sink = sink_ref[hd]
                mx = jnp.maximum(jnp.max(lg, axis=1, keepdims=True), sink)
                p = jnp.exp(lg - mx)
                ps += [jnp.where(from_prev, p, 0.0).astype(BF16), jnp.where(from_prev, 0.0, p).astype(BF16)]
                sk.append(jnp.exp(sink - mx))
            pv = _dot(jnp.concatenate(ps, axis=1), v_bd)
            den = pv[:, LANES:] + jnp.where(lane_q < HEAD_DIM, sk[0], sk[1])
            out_ref[rows, jp * LANES:(jp + 1) * LANES] = (pv[:, :LANES] * (1.0 / den)).astype(out_ref.dtype)


def _swa_post_body(q_ref, kt_ref, ktp_ref, v_ref, vp_ref, bias_ref, sink_ref, h_ref, p_ref, *rest, tiles_per_seq):
    w_refs, out_ref, attn_scr = rest[:-2], rest[-2], rest[-1]
    s = pl.program_id(0)
    nt = pl.num_programs(0) - 1
    tile = jnp.minimum(s, nt - 1)

    def post():
        out_ref[...] = _post_rows(attn_scr[(s + 1) % 2], h_ref[...], p_ref[...], *w_refs)

    def mixer():
        attn_slot = attn_scr.at[s % 2]
        blk = WINDOW
        qi = lax.broadcasted_iota(jnp.int32, (blk, blk), 0)
        kj = lax.broadcasted_iota(jnp.int32, (blk, blk), 1)
        has_prev = (kj <= qi) | (tile % tiles_per_seq != 0)
        for b in range(q_ref.shape[0] // blk):
            rows, prev_rows = slice(b * blk, (b + 1) * blk), slice((b - 1) * blk, b * blk)
            _swa_block(q_ref[rows, :],
                       ktp_ref[...] if b == 0 else kt_ref[:, prev_rows], kt_ref[:, rows],
                       vp_ref[...] if b == 0 else v_ref[prev_rows, :], v_ref[rows, :],
                       bias_ref, sink_ref, has_prev if b == 0 else None, attn_slot, rows)

    _skewed_step(s, nt, mixer, post)


def _swa_post(q, kt, v, bias, sinks, h, p, wo, w, *, layer, mixer, batch, seq):
    t = batch * seq
    tm = ROW_TILE
    nt = t // tm
    blocks = tm // WINDOW
    mix_i, mix_row, mix_col, post_row, p_spec = _pipelined_specs(tm, nt, layer)
    prev_blk = lambda s: jnp.maximum(mix_i(s) * blocks - 1, 0)
    w_specs, w_args = _post_weights(wo, w, layer, mixer)
    return pl.pallas_call(
        functools.partial(_swa_post_body, tiles_per_seq=seq // tm),
        grid=(nt + 1,),
        in_specs=[mix_row(D_MODEL), mix_col(KV_DIM), pl.BlockSpec((KV_DIM, WINDOW), lambda s: (0, prev_blk(s))),
                  mix_row(KV_DIM), pl.BlockSpec((WINDOW, KV_DIM), lambda s: (prev_blk(s), 0)),
                  _resident(bias.shape), pl.BlockSpec(memory_space=pltpu.SMEM), post_row(D_MODEL), p_spec] + w_specs,
        out_specs=post_row(D_MODEL),
        out_shape=jax.ShapeDtypeStruct((t, D_MODEL), F32),
        scratch_shapes=[pltpu.VMEM((2, tm, D_MODEL), BF16)],
        compiler_params=_params(1), name="swa_post",
    )(q, kt, kt, v, v, bias, sinks[mixer], h, p, *w_args)


SAMPLE_BB = 8


def _swa_sample_body(q_ref, kc_ref, vc_ref, kn_ref, vn_ref, bias_ref, sink_ref, out_ref, ko_ref, vo_ref):
    w = WINDOW
    row = lax.broadcasted_iota(jnp.int32, (w, KV_DIM), 0)
    rg = lax.broadcasted_iota(jnp.int32, (Q_HEADS, KV_DIM), 0) // GROUP
    lg = lax.broadcasted_iota(jnp.int32, (Q_HEADS, KV_DIM), 1) // HEAD_DIM
    rsel = lax.broadcasted_iota(jnp.int32, (Q_HEADS, HEAD_DIM), 0) // GROUP
    logits, v_wins = [], []
    for i in range(SAMPLE_BB):
        k_win = jnp.where(row == w - 1, kn_ref[i], pltpu.roll(kc_ref[i], w - 1, axis=0))
        v_win = jnp.where(row == w - 1, vn_ref[i], pltpu.roll(vc_ref[i], w - 1, axis=0))
        ko_ref[i] = k_win
        vo_ref[i] = v_win
        qb = q_ref[i].astype(F32)
        q_bd = jnp.where(rg == lg, jnp.concatenate([qb] * KV_HEADS, axis=1), 0.0).astype(BF16)
        logits.append(lax.dot_general(q_bd, k_win.astype(BF16), _NT, preferred_element_type=F32))
        v_wins.append(v_win.astype(BF16))
    lg_all = jnp.concatenate(logits, axis=0) + bias_ref[...]
    sink = sink_ref[...]
    mx = jnp.maximum(jnp.max(lg_all, axis=1, keepdims=True), sink)
    p = jnp.exp(lg_all - mx)
    probs = (p * (1.0 / (jnp.sum(p, axis=1, keepdims=True) + jnp.exp(sink - mx)))).astype(BF16)
    for i in range(SAMPLE_BB):
        o = _dot(probs[i * Q_HEADS:(i + 1) * Q_HEADS], v_wins[i])
        sel = o[:, 0:HEAD_DIM]
        for g in range(1, KV_HEADS):
            sel = jnp.where(rsel == g, o[:, g * HEAD_DIM:(g + 1) * HEAD_DIM], sel)
        out_ref[i] = sel.astype(out_ref.dtype)


def _swa_sample(q, kc_all, vc_all, kn, vn, bias, sinks, *, mixer):
    b = q.shape[0]
    bb = SAMPLE_BB
    blk = lambda shape: pl.BlockSpec((bb,) + shape, lambda i: (i, 0, 0))
    cache = pl.BlockSpec((None, bb, WINDOW, KV_DIM), lambda i: (mixer, i, 0, 0))
    return pl.pallas_call(
        _swa_sample_body,
        grid=(b // bb,),
        in_specs=[blk((Q_HEADS, HEAD_DIM)), cache, cache, blk((1, KV_DIM)), blk((1, KV_DIM)), _resident(bias.shape),
                  _layer(sinks, mixer)],
        out_specs=[blk((Q_HEADS, HEAD_DIM)), cache, cache],
        out_shape=[jax.ShapeDtypeStruct((b, Q_HEADS, HEAD_DIM), BF16), jax.ShapeDtypeStruct(kc_all.shape, F32),
                   jax.ShapeDtypeStruct(vc_all.shape, F32)],
        input_output_aliases={1: 1, 2: 2},
        compiler_params=_params(1), name="swa_sample",
    )(q, kc_all, vc_all, kn, vn, bias, sinks)


def kernel(x_prompt, x_sample, state_mlstm_C, state_mlstm_n, state_mlstm_m, cache_swa_k, cache_swa_v, p_prompt, p_sample, rel_bias, norm_pre_mix, norm_post_mix, norm_pre_ffn, norm_post_ffn, mlstm_w_in, mlstm_b_i, mlstm_b_f, mlstm_norm, mlstm_w_out, swa_w_qkv, swa_sinks, swa_w_o, ffn_w_up, ffn_w_down, ple_w_proj, ple_norm, ple_w_gate):
    bp, sp, _ = x_prompt.shape
    bs = x_sample.shape[0]
    win = cache_swa_k.shape[2]
    assert win == WINDOW and x_sample.shape[1] == 1
    bf = lambda a: a.astype(BF16)
    vecs = lambda a: a.astype(F32)[:, None, :]

    post_w = dict(npm=vecs(norm_post_mix), npf=vecs(norm_pre_ffn), wup=bf(ffn_w_up), wdn=bf(ffn_w_down),
                  npff=vecs(norm_post_ffn), wpj=bf(ple_w_proj), pln=vecs(ple_norm), wgate=bf(ple_w_gate))
    n_pre = vecs(norm_pre_mix)
    m_w = bf(mlstm_w_in)
    m_bias = jnp.concatenate([mlstm_b_i, mlstm_b_f], axis=1).astype(F32)
    m_wout, m_norm = bf(mlstm_w_out), vecs(mlstm_norm)
    s_w, s_wo = bf(swa_w_qkv), bf(swa_w_o)
    sinks = swa_sinks.astype(F32)

    qi = jnp.arange(WINDOW)[:, None]
    kj = jnp.arange(WINDOW)[None, :]
    dist_p = jnp.where(kj > qi, qi + WINDOW - kj, qi - kj)
    bias_p = _bias_table(_rel_bucket(dist_p).astype(jnp.int32), rel_bias.astype(F32))
    dist_s = (win - 1) - jnp.arange(win)
    bias_s = _bias_table(jnp.broadcast_to(_rel_bucket(dist_s)[None, :], (8, win)).astype(jnp.int32),
                         rel_bias.astype(F32))[:, 0, :]
    bias_s = jnp.tile(bias_s, (SAMPLE_BB, 1))
    sinks_s = jnp.tile(sinks[:, :, None], (1, SAMPLE_BB, 1))

    hp = x_prompt.reshape(bp * sp, D_MODEL)
    hs = x_sample.reshape(bs, D_MODEL)
    pp = p_prompt.reshape(DEPTH, bp * sp, PLE_DIM)
    ps = p_sample.reshape(DEPTH, bs, PLE_DIM)
    c_old = state_mlstm_C.astype(F32).reshape(-1, bs, HK, DV)
    k_win = cache_swa_k.astype(F32).reshape(-1, bs, win, KV_DIM)
    v_win = cache_swa_v.astype(F32).reshape(-1, bs, win, KV_DIM)
    c_new = None
    outs = dict(Cp=[], np=[], mp=[], kp=[], vp=[], ns=[], ms=[])

    for i in range(DEPTH):
        j = i // N_MIXERS
        if i % N_MIXERS == 0:
            wo = m_wout
            q, kt, v, og, gt = _mlstm_in(hp, n_pre, m_w, m_bias[j].reshape(-1, 1), m_norm,
                                         layer=i, mixer=j, prompt=True, act_dtype=BF16)
            hp, c_fin, n_fin, m_fin = _mlstm_post(q, kt, v, og, gt, hp, pp, wo, post_w,
                                                  layer=i, mixer=j, batch=bp, seq=sp)
            outs["Cp"].append(c_fin)
            outs["np"].append(n_fin[..., 0])
            outs["mp"].append(m_fin[..., 0])
            q, k, v, og, g = _mlstm_in(hs, n_pre, m_w, m_bias[j].reshape(1, -1), m_norm,
                                       layer=i, mixer=j, prompt=False, act_dtype=F32)
            a_s, c_new, n_new, m_new = _mlstm_step(
                c_old, c_new, state_mlstm_n[j].astype(F32), state_mlstm_m[j].astype(F32),
                q, k, v, og, g, mixer=j)
            outs["ns"].append(n_new)
            outs["ms"].append(m_new)
        else:
            wo = s_wo
            q, k, kt, v = _swa_in(hp, n_pre, s_w, layer=i, mixer=j)
            hp = _swa_post(q, kt, v, bias_p, sinks, hp, pp, wo, post_w, layer=i, mixer=j, batch=bp, seq=sp)
            last = lambda a: a.reshape(bp, sp, KV_DIM)[:, sp - win:].reshape(bp, win, KV_HEADS, HEAD_DIM)
            outs["kp"].append(last(k))
            outs["vp"].append(last(v))
            q, k, _, v = _swa_in(hs, n_pre, s_w, layer=i, mixer=j)
            a_s, k_win, v_win = _swa_sample(
                q.reshape(bs, Q_HEADS, HEAD_DIM), k_win, v_win, k.reshape(bs, 1, KV_DIM), v.reshape(bs, 1, KV_DIM),
                bias_s, sinks_s, mixer=j)
            a_s = a_s.reshape(bs, D_MODEL)
        hs = _post(a_s, hs, ps, wo, post_w, layer=i, mixer=j)

    st = lambda key, like: jnp.stack(outs[key]).astype(like.dtype)
    kv_out = lambda a, like: a.reshape(-1, bs, win, KV_HEADS, HEAD_DIM).astype(like.dtype)
    return (hp.reshape(x_prompt.shape), hs.reshape(x_sample.shape),
            st("Cp", state_mlstm_C), st("np", state_mlstm_n), st("mp", state_mlstm_m),
            st("kp", cache_swa_k), st("vp", cache_swa_v),
            c_new.reshape(state_mlstm_C.shape).astype(state_mlstm_C.dtype), st("ns", state_mlstm_n),
            st("ms", state_mlstm_m), kv_out(k_win, cache_swa_k), kv_out(v_win, cache_swa_v))
```

```python
import functools
import math

import jax
import jax.numpy as jnp
from jax import lax
from jax.experimental import pallas as pl
from jax.experimental.pallas import tpu as pltpu

F32 = jnp.float32
BF16 = jnp.bfloat16

D_MODEL = 1024
DEPTH = 4
N_MIXERS = 2
HEADS = 8
DQK = 64
DV = 128
CHUNK = 128
Q_HEADS = 16
KV_HEADS = 4
GROUP = Q_HEADS // KV_HEADS
HEAD_DIM = 64
KV_DIM = KV_HEADS * HEAD_DIM
WINDOW = 128
REL_BUCKETS = 32
REL_MAX_DIST = 128
PLE_DIM = 256
D_FF = 4 * D_MODEL
EPS = 1e-6
NEG = -1e30

HK, HV = HEADS * DQK, HEADS * DV
Q0, K0, V0, O0, G0 = 0, HK, 2 * HK, 2 * HK + HV, 2 * HK + HV + D_MODEL
MLSTM_IN = G0 + 2 * HEADS

V7X_VMEM_BYTES = 64 * 1024 * 1024
VMEM_LIMIT = V7X_VMEM_BYTES - 8 * 1024 * 1024
LANES = 128
ROW_TILE = 512
IN_ROW_TILE = 1024

_NT = (((1,), (1,)), ((), ()))


def _row_tile(t, tile=ROW_TILE):
    return tile if t % tile == 0 else t


def _params(n_grid):
    return pltpu.CompilerParams(dimension_semantics=("arbitrary",) * n_grid, vmem_limit_bytes=VMEM_LIMIT)


def _resident(shape):
    nd = len(shape)
    return pl.BlockSpec(shape, lambda *_: (0,) * nd, pipeline_mode=pl.Buffered(1))


def _layer(arr, layer):
    tail = arr.shape[1:]
    return pl.BlockSpec((None,) + tail, lambda *_: (layer,) + (0,) * len(tail), pipeline_mode=pl.Buffered(1))


def _skip_ref(body, pos):
    def wrapped(*refs):
        return body(*refs[:pos], *refs[pos + 1:])
    return wrapped


def _rms(x, w=None):
    y = x * lax.rsqrt(jnp.mean(x * x, axis=-1, keepdims=True) + EPS)
    return y if w is None else y * w


def _log_sigmoid(x):
    return -(jnp.maximum(-x, 0.0) + jnp.log1p(jnp.exp(-jnp.abs(x))))


def _sigmoid(x):
    return 1.0 / (1.0 + jnp.exp(-x))


def _dot(a, b):
    return jnp.dot(a, b, preferred_element_type=F32)


def _mlstm_in_body(h_ref, nw_ref, w_ref, bg_ref, hnw_ref, q_ref, k_ref, v_ref, o_ref, g_ref, *scratch, prompt):
    if prompt:
        wkgt_scr, = scratch

        @pl.when(pl.program_id(0) == 0)
        def _():
            wkgt_scr[0:HK, :] = w_ref[:, K0:V0].astype(F32).T.astype(BF16)
            wg = jnp.concatenate([w_ref[:, G0:MLSTM_IN].astype(F32),
                                  jnp.zeros((D_MODEL, LANES - 2 * HEADS), F32)], axis=1)
            wkgt_scr[HK:, :] = wg.T.astype(BF16)

    u = _rms(h_ref[...], nw_ref[...]).astype(BF16)
    q_ref[...] = (_dot(u, w_ref[:, Q0:K0]) * (DQK ** -0.5)).astype(q_ref.dtype)
    v_ref[...] = _dot(u, w_ref[:, V0:O0]).astype(v_ref.dtype)
    o_ref[...] = _sigmoid(_dot(u, w_ref[:, O0:G0])) * hnw_ref[...]
    if prompt:
        kgt = lax.dot_general(wkgt_scr[...], u, _NT, preferred_element_type=F32)
        k_ref[...] = kgt[:HK].astype(k_ref.dtype)
        gt = kgt[HK:HK + 2 * HEADS] + bg_ref[...]
        ig, lf = gt[:HEADS], _log_sigmoid(gt[HEADS:])
        r = lax.broadcasted_iota(jnp.int32, (CHUNK, CHUNK), 0)
        t = lax.broadcasted_iota(jnp.int32, (CHUNK, CHUNK), 1)
        upper = (r <= t).astype(F32)
        b = jnp.concatenate([jnp.dot(lf[:, c:c + CHUNK], upper, precision=lax.Precision.HIGHEST,
                                     preferred_element_type=F32) for c in range(0, lf.shape[1], CHUNK)], axis=1)
        g_ref[...] = jnp.concatenate([ig, lf, b], axis=0)
    else:
        k_ref[...] = _dot(u, w_ref[:, K0:V0]).astype(k_ref.dtype)
        g = _dot(u, w_ref[:, G0:MLSTM_IN]) + bg_ref[...]
        lane = lax.broadcasted_iota(jnp.int32, g.shape, 1)
        g_ref[...] = jnp.where(lane < HEADS, g, _log_sigmoid(g))


def _mlstm_in(h, nw, w, bg, hnw, *, layer, mixer, prompt, act_dtype):
    t = h.shape[0]
    tm = _row_tile(t, IN_ROW_TILE)
    row = lambda n: pl.BlockSpec((tm, n), lambda i: (i, 0))
    col = lambda n: pl.BlockSpec((n, tm), lambda i: (0, i))
    kshape, kspec = ((HK, t), col(HK)) if prompt else ((t, HK), row(HK))
    gshape, gspec = ((3 * HEADS, t), col(3 * HEADS)) if prompt else ((t, 2 * HEADS), row(2 * HEADS))
    return pl.pallas_call(
        functools.partial(_mlstm_in_body, prompt=prompt),
        grid=(t // tm,),
        in_specs=[row(D_MODEL), _layer(nw, layer), _layer(w, mixer), _resident(bg.shape), _layer(hnw, mixer)],
        out_specs=[row(HK), kspec, row(HV), row(D_MODEL), gspec],
        out_shape=[jax.ShapeDtypeStruct((t, HK), act_dtype), jax.ShapeDtypeStruct(kshape, act_dtype),
                   jax.ShapeDtypeStruct((t, HV), act_dtype), jax.ShapeDtypeStruct((t, D_MODEL), F32),
                   jax.ShapeDtypeStruct(gshape, F32)],
        scratch_shapes=[pltpu.VMEM((HK + LANES, D_MODEL), BF16)] if prompt else [],
        compiler_params=_params(1), name="mlstm_in",
    )(h, nw, w, bg, hnw)


def _swa_in_body(h_ref, nw_ref, w_ref, q_ref, k_ref, kt_ref, v_ref):
    u = _rms(h_ref[...], nw_ref[...]).astype(BF16)
    q_ref[...] = (_dot(u, w_ref[:, :D_MODEL]) * (HEAD_DIM ** -0.5)).astype(q_ref.dtype)
    k = _dot(u, w_ref[:, D_MODEL:D_MODEL + KV_DIM])
    k_ref[...] = k
    kt_ref[...] = k.T.astype(kt_ref.dtype)
    v_ref[...] = _dot(u, w_ref[:, D_MODEL + KV_DIM:])


def _swa_in(h, nw, w, *, layer, mixer):
    t = h.shape[0]
    tm = _row_tile(t, IN_ROW_TILE)
    row = lambda n: pl.BlockSpec((tm, n), lambda i: (i, 0))
    return pl.pallas_call(
        _swa_in_body,
        grid=(t // tm,),
        in_specs=[row(D_MODEL), _layer(nw, layer), _layer(w, mixer)],
        out_specs=[row(D_MODEL), row(KV_DIM), pl.BlockSpec((KV_DIM, tm), lambda i: (0, i)), row(KV_DIM)],
        out_shape=[jax.ShapeDtypeStruct((t, D_MODEL), BF16), jax.ShapeDtypeStruct((t, KV_DIM), F32),
                   jax.ShapeDtypeStruct((KV_DIM, t), BF16), jax.ShapeDtypeStruct((t, KV_DIM), F32)],
        compiler_params=_params(1), name="swa_in",
    )(h, nw, w)


FF_CHUNK = 1024


def _post_rows(a, h, p, wo_ref, npm_ref, npf_ref, wup_ref, wdn_ref, npff_ref, wpj_ref, pln_ref, wgate_ref):
    y = _dot(a.astype(BF16), wo_ref[...])
    h1 = h + _rms(y, npm_ref[...])
    u = _rms(h1, npf_ref[...]).astype(BF16)
    acc = jnp.zeros(h1.shape, F32)
    for c in range(D_FF // FF_CHUNK):
        hid = jnp.maximum(_dot(u, wup_ref[:, c * FF_CHUNK:(c + 1) * FF_CHUNK]), 0.0)
        acc = acc + _dot((hid * hid).astype(BF16), wdn_ref[c * FF_CHUNK:(c + 1) * FF_CHUNK, :])
    h2 = h1 + _rms(acc, npff_ref[...])
    e = _rms(_dot(p.astype(BF16), wpj_ref[...]), pln_ref[...])
    g = _sigmoid(_dot(_rms(h2).astype(BF16), wgate_ref[...]))
    return h2 + e * g


def _post_body(a_ref, h_ref, p_ref, *rest):
    w_refs, out_ref = rest[:-1], rest[-1]
    out_ref[...] = _post_rows(a_ref[...], h_ref[...], p_ref[...], *w_refs)


_POST_KEYS = ("npm", "npf", "wup", "wdn", "npff", "wpj", "pln", "wgate")


def _post_weights(wo, w, layer, mixer):
    return ([_layer(wo, mixer)] + [_layer(w[k], layer) for k in _POST_KEYS], [wo] + [w[k] for k in _POST_KEYS])


def _post(a, h, p, wo, w, *, layer, mixer):
    t = h.shape[0]
    tm = _row_tile(t)
    row = lambda n: pl.BlockSpec((tm, n), lambda i: (i, 0))
    w_specs, w_args = _post_weights(wo, w, layer, mixer)
    return pl.pallas_call(
        _post_body,
        grid=(t // tm,),
        in_specs=[row(D_MODEL), row(D_MODEL), pl.BlockSpec((None, tm, PLE_DIM), lambda i: (layer, i, 0))] + w_specs,
        out_specs=row(D_MODEL),
        out_shape=jax.ShapeDtypeStruct((t, D_MODEL), F32),
        compiler_params=_params(1), name="post",
    )(a, h, p, *w_args)


def _skewed_step(s, nt, mixer, post):
    @pl.when(s == 0)
    def _():
        mixer()

    @pl.when((s > 0) & (s < nt))
    def _():
        post()
        mixer()

    @pl.when(s == nt)
    def _():
        post()


def _pipelined_specs(tm, nt, layer):
    mix_i = lambda s: jnp.minimum(s, nt - 1)
    post_i = lambda s: jnp.maximum(s - 1, 0)
    mix_row = lambda n: pl.BlockSpec((tm, n), lambda s: (mix_i(s), 0))
    mix_col = lambda n: pl.BlockSpec((n, tm), lambda s: (0, mix_i(s)))
    post_row = lambda n: pl.BlockSpec((tm, n), lambda s: (post_i(s), 0))
    p_spec = pl.BlockSpec((None, tm, PLE_DIM), lambda s: (layer, post_i(s), 0))
    return mix_i, mix_row, mix_col, post_row, p_spec


STATE_W = 2 * DV


def _scan_chunk(q_ref, kt_ref, v_ref, og_ref, gt_ref, hg_ref, cn_scr, m_scr, r0):
    L = CHUNK
    rows = slice(r0, r0 + L)
    ri = lax.broadcasted_iota(jnp.int32, (L, L), 0)
    ci = lax.broadcasted_iota(jnp.int32, (L, L), 1)
    tri = ci <= ri
    gt = gt_ref[:, rows]
    ig_rows, lf_rows, b_rows = gt[:HEADS], gt[HEADS:2 * HEADS], gt[2 * HEADS:]
    a_rows = ig_rows - b_rows
    m_prev = m_scr[...]
    b_last = jnp.broadcast_to(b_rows[:, L - 1:L], (HEADS, L))
    m_new = b_last + jnp.maximum(m_prev, jnp.max(a_rows, axis=1, keepdims=True))
    decay_rows = jnp.exp(b_last + m_prev - m_new)
    wk_rows = jnp.exp(b_last - b_rows + ig_rows - m_new)
    m_scr[...] = m_new

    ones_rhs = jnp.ones((L, DV), BF16)
    zeros_kt = jnp.zeros((DQK, L), BF16)
    zeros_cn = jnp.zeros((DQK, STATE_W), BF16)
    for j in range(HEADS // 2):
        q_pair = q_ref[rows, j * 128:(j + 1) * 128]
        kt_pair = [kt_ref[h * DQK:(h + 1) * DQK, rows] for h in (2 * j, 2 * j + 1)]
        s_pair = _dot(q_pair, jnp.concatenate([jnp.concatenate([kt_pair[0], zeros_kt], axis=0),
                                               jnp.concatenate([zeros_kt, kt_pair[1]], axis=0)], axis=1))
        for par in range(2):
            h = 2 * j + par
            kt_h = kt_pair[par]
            v_ext = jnp.concatenate([v_ref[rows, h * DV:(h + 1) * DV], ones_rhs], axis=1)
            cn = cn_scr[h]
            cn16 = cn.astype(BF16)
            a_row, m_row = a_rows[h:h + 1], m_prev[h:h + 1]

            big_m = jnp.maximum(jnp.max(jnp.where(tri, a_row, -jnp.inf), axis=1, keepdims=True), m_row)
            b_col = jnp.sum(jnp.where(tri, lf_rows[h:h + 1], 0.0), axis=1, keepdims=True)
            w_inter = jnp.exp(m_row - big_m)
            s = s_pair[:, par * L:(par + 1) * L] * jnp.exp(jnp.where(tri, a_row - big_m, NEG))
            q_w = (q_pair.astype(F32) * w_inter).astype(BF16)
            state = jnp.concatenate([cn16, zeros_cn] if par == 0 else [zeros_cn, cn16], axis=0)
            nd = _dot(jnp.concatenate([s.astype(BF16), q_w], axis=1), jnp.concatenate([v_ext, state], axis=0))
            hh = nd[:, :DV] * (1.0 / jnp.maximum(jnp.abs(nd[:, DV:]), jnp.exp(-(b_col + big_m))))

            kw_t = kt_h.astype(F32) * wk_rows[h:h + 1]
            dec = decay_rows[h:h + 1]
            cn_scr[h] = jnp.concatenate([dec, dec], axis=1) * cn + _dot(kw_t.astype(BF16), v_ext)

            hg_ref[rows, h * DV:(h + 1) * DV] = (_rms(hh) * og_ref[rows, h * DV:(h + 1) * DV]).astype(hg_ref.dtype)


def _mlstm_post_body(q_ref, kt_ref, v_ref, og_ref, gt_ref, h_ref, p_ref, *rest, tiles_per_seq):
    w_refs = rest[:-7]
    out_ref, c_out_ref, n_out_ref, m_out_ref, hg_scr, cn_scr, m_scr = rest[-7:]
    s = pl.program_id(0)
    nt = pl.num_programs(0) - 1
    tile = jnp.minimum(s, nt - 1)

    @pl.when((tile % tiles_per_seq == 0) & (s < nt))
    def _():
        cn_scr[...] = jnp.zeros(cn_scr.shape, F32)
        m_scr[...] = jnp.zeros(m_scr.shape, F32)

    def post():
        out_ref[...] = _post_rows(hg_scr[(s + 1) % 2], h_ref[...], p_ref[...], *w_refs)

    def mixer():
        hg_slot = hg_scr.at[s % 2]
        for r0 in range(0, q_ref.shape[0], CHUNK):
            _scan_chunk(q_ref, kt_ref, v_ref, og_ref, gt_ref, hg_slot, cn_scr, m_scr, r0)

    _skewed_step(s, nt, mixer, post)

    @pl.when((tile % tiles_per_seq == tiles_per_seq - 1) & (s < nt))
    def _():
        for h in range(HEADS):
            c_out_ref[0, h] = cn_scr[h, :, 0:DV]
            n_out_ref[0, h] = cn_scr[h, :, DV:STATE_W]
        m_out_ref[0] = m_scr[...]


def _mlstm_post(q, kt, v, og, gt, h, p, wo, w, *, layer, mixer, batch, seq):
    t = batch * seq
    tm = ROW_TILE
    nt = t // tm
    tiles_per_seq = seq // tm
    mix_i, mix_row, mix_col, post_row, p_spec = _pipelined_specs(tm, nt, layer)
    seq_i = lambda s: mix_i(s) // tiles_per_seq
    w_specs, w_args = _post_weights(wo, w, layer, mixer)
    return pl.pallas_call(
        functools.partial(_mlstm_post_body, tiles_per_seq=tiles_per_seq),
        grid=(nt + 1,),
        in_specs=[mix_row(HK), mix_col(HK), mix_row(HV), mix_row(D_MODEL), mix_col(3 * HEADS),
                  post_row(D_MODEL), p_spec] + w_specs,
        out_specs=[post_row(D_MODEL),
                   pl.BlockSpec((1, HEADS, DQK, DV), lambda s: (seq_i(s), 0, 0, 0)),
                   pl.BlockSpec((1, HEADS, DQK, STATE_W - DV), lambda s: (seq_i(s), 0, 0, 0)),
                   pl.BlockSpec((1, HEADS, LANES), lambda s: (seq_i(s), 0, 0))],
        out_shape=[jax.ShapeDtypeStruct((t, D_MODEL), F32),
                   jax.ShapeDtypeStruct((batch, HEADS, DQK, DV), F32),
                   jax.ShapeDtypeStruct((batch, HEADS, DQK, STATE_W - DV), F32),
                   jax.ShapeDtypeStruct((batch, HEADS, LANES), F32)],
        scratch_shapes=[pltpu.VMEM((2, tm, HV), BF16), pltpu.VMEM((HEADS, DQK, STATE_W), F32),
                        pltpu.VMEM((HEADS, LANES), F32)],
        compiler_params=_params(1), name="mlstm_post",
    )(q, kt, v, og, gt, h, p, *w_args)


STEP_BB = 16
_TN = (((0,), (0,)), ((), ()))


def _mlstm_step_body(c_ref, n_ref, m_ref, q_ref, k_ref, v_ref, og_ref, g_ref,
                     hg_ref, c_out_ref, n_out_ref, m_out_ref, *, own):
    bb = STEP_BB
    for slab in range(c_out_ref.shape[0]):
        if slab != own:
            c_out_ref[slab] = jnp.zeros(c_out_ref.shape[1:], F32)
    g = g_ref[...]
    ig, lf, m_prev = g[:, :HEADS], g[:, HEADS:], m_ref[...]
    inter = lf + m_prev
    m_t = jnp.maximum(inter, ig)
    w_inter = jnp.exp(inter - m_t)
    in_w = jnp.exp(ig - m_t)
    decay = jnp.exp(lf + m_prev - m_t)
    floor = jnp.exp(-m_t)
    m_out_ref[...] = m_t
    diag = (lax.broadcasted_iota(jnp.int32, (bb, bb * DV), 0)
            == lax.broadcasted_iota(jnp.int32, (bb, bb * DV), 1) // DV)
    row_v = lax.broadcasted_iota(jnp.int32, (bb, DV), 0)

    for h in range(HEADS):
        col = lambda a: a[:, h:h + 1]
        q = q_ref[:, h * DQK:(h + 1) * DQK]
        k = k_ref[:, h * DQK:(h + 1) * DQK]
        v = v_ref[:, h * DV:(h + 1) * DV]
        n = n_ref[:, h, :]
        s = jnp.sum(q * k, axis=1, keepdims=True) * col(in_w)
        kw = k * col(in_w)
        c_tiles = [c_ref[i, h * DQK:(h + 1) * DQK, :] for i in range(bb)]
        qc_all = _dot(q.astype(BF16), jnp.concatenate(c_tiles, axis=1).astype(BF16))
        v_bd = jnp.where(diag, jnp.concatenate([v] * bb, axis=1), 0.0).astype(BF16)
        outer = lax.dot_general(kw.astype(BF16), v_bd, _TN, preferred_element_type=F32)
        qc = jnp.zeros((bb, DV), F32)
        for i in range(bb):
            qc = jnp.where(row_v == i, qc_all[:, i * DV:(i + 1) * DV], qc)
            c_out_ref[own, i, h * DQK:(h + 1) * DQK, :] = (decay[i:i + 1, h:h + 1] * c_tiles[i]
                                                           + outer[:, i * DV:(i + 1) * DV])
        num = col(w_inter) * qc + s * v
        den = col(w_inter) * jnp.sum(q * n, axis=1, keepdims=True) + s
        hh = num * (1.0 / jnp.maximum(jnp.abs(den), col(floor)))
        n_out_ref[:, h, :] = col(decay) * n + kw
        hg_ref[:, h * DV:(h + 1) * DV] = _rms(hh) * og_ref[:, h * DV:(h + 1) * DV]


def _mlstm_step(c_all, c_new_prev, n, m, q, k, v, og, g, *, mixer):
    n_slabs, b = c_all.shape[:2]
    bb = STEP_BB
    rows = lambda width: pl.BlockSpec((bb, width), lambda i: (i, 0))
    cspec = pl.BlockSpec((None, bb, HK, DV), lambda i: (mixer, i, 0, 0))
    nspec = pl.BlockSpec((bb, HEADS, DQK), lambda i: (i, 0, 0))
    in_specs = [cspec, nspec, rows(HEADS), rows(HK), rows(HK), rows(HV), rows(D_MODEL), rows(2 * HEADS)]
    args = [c_all, n, m, q, k, v, og, g]
    if c_new_prev is None:
        c_out_spec = pl.BlockSpec((n_slabs, bb, HK, DV), lambda i: (0, i, 0, 0))
        body, aliases = functools.partial(_mlstm_step_body, own=mixer), {}
    else:
        c_out_spec = pl.BlockSpec((1, bb, HK, DV), lambda i: (mixer, i, 0, 0))
        in_specs.append(pl.BlockSpec(memory_space=pl.ANY))
        body, aliases = _skip_ref(functools.partial(_mlstm_step_body, own=0), len(args)), {len(args): 1}
        args.append(c_new_prev)
    return pl.pallas_call(
        body,
        grid=(b // bb,),
        in_specs=in_specs,
        out_specs=[rows(HV), c_out_spec, nspec, rows(HEADS)],
        out_shape=[jax.ShapeDtypeStruct((b, HV), F32), jax.ShapeDtypeStruct(c_all.shape, F32),
                   jax.ShapeDtypeStruct(n.shape, F32), jax.ShapeDtypeStruct((b, HEADS), F32)],
        input_output_aliases=aliases,
        compiler_params=_params(1), name="mlstm_step",
    )(*args)


def _bias_body(bucket_ref, rel_ref, out_ref):
    bucket = bucket_ref[...]
    for h in range(Q_HEADS):
        acc = jnp.zeros(bucket.shape, F32)
        for b in range(REL_BUCKETS):
            acc = jnp.where(bucket == b, rel_ref[b, h], acc)
        out_ref[h] = acc


def _bias_table(bucket, rel_bias):
    shape = bucket.shape
    return pl.pallas_call(
        _bias_body,
        in_specs=[pl.BlockSpec(memory_space=pltpu.VMEM), pl.BlockSpec(memory_space=pltpu.SMEM)],
        out_specs=pl.BlockSpec(memory_space=pltpu.VMEM),
        out_shape=jax.ShapeDtypeStruct((Q_HEADS,) + shape, F32),
        name="rel_bias_table",
    )(bucket, rel_bias)


def _rel_bucket(dist):
    max_exact = REL_BUCKETS // 2
    d = jnp.maximum(dist, 0)
    df = jnp.maximum(d, 1).astype(F32)
    large = max_exact + (jnp.log(df / max_exact) / math.log(REL_MAX_DIST / max_exact)
                         * (REL_BUCKETS - max_exact)).astype(jnp.int32)
    large = jnp.minimum(large, REL_BUCKETS - 1)
    return jnp.where(d < max_exact, d, large)


def _swa_block(q, kt_prev, kt_cur, v_prev, v_cur, bias_ref, sink_ref, in_range, out_ref, rows):
    blk = WINDOW
    qi = lax.broadcasted_iota(jnp.int32, (blk, blk), 0)
    kj = lax.broadcasted_iota(jnp.int32, (blk, blk), 1)
    from_prev = kj > qi
    lane = lax.broadcasted_iota(jnp.int32, (2 * blk, LANES), 1)
    lane_q = lax.broadcasted_iota(jnp.int32, (blk, LANES), 1)
    zero_k = jnp.zeros((HEAD_DIM, 2 * blk), BF16)
    ones_bd = jnp.concatenate([(lane < HEAD_DIM).astype(BF16), (lane >= HEAD_DIM).astype(BF16)], axis=0)

    for g in range(KV_HEADS):
        hd_rows = slice(g * HEAD_DIM, (g + 1) * HEAD_DIM)
        kt_g = jnp.concatenate([kt_prev[hd_rows, :], kt_cur[hd_rows, :]], axis=1)
        rhs = jnp.concatenate([jnp.concatenate([kt_g, zero_k], axis=0),
                               jnp.concatenate([zero_k, kt_g], axis=0)], axis=1)
        v_lanes = slice((g // 2) * LANES, (g // 2 + 1) * LANES)
        vt = jnp.concatenate([v_prev[:, v_lanes], v_cur[:, v_lanes]], axis=0)
        vr = pltpu.roll(vt, HEAD_DIM, axis=1)
        v_lo = jnp.where(lane < HEAD_DIM, vt if g % 2 == 0 else vr, 0.0)
        v_hi = jnp.where(lane >= HEAD_DIM, vr if g % 2 == 0 else vt, 0.0)
        v_bd = jnp.concatenate([jnp.concatenate([v_lo, v_hi], axis=0).astype(BF16), ones_bd], axis=1)
        for pair in range(GROUP // 2):
            jp = g * (GROUP // 2) + pair
            logits = _dot(q[:, jp * LANES:(jp + 1) * LANES], rhs)
            ps, sk = [], []
            for i in range(2):
                hd = 2 * jp + i
                lg = jnp.where(from_prev, logits[:, 2 * i * blk:(2 * i + 1) * blk],
                               logits[:, (2 * i + 1) * blk:(2 * i + 2) * blk]) + bias_ref[hd]
                if in_range is not None:
                    lg = jnp.where(in_range, lg, NEG)
                sink = sink_ref[hd]
                mx = jnp.maximum(jnp.max(lg, axis=1, keepdims=True), sink)
                p = jnp.exp(lg - mx)
                ps += [jnp.where(from_prev, p, 0.0).astype(BF16), jnp.where(from_prev, 0.0, p).astype(BF16)]
                sk.append(jnp.exp(sink - mx))
            pv = _dot(jnp.concatenate(ps, axis=1), v_bd)
            den = pv[:, LANES:] + jnp.where(lane_q < HEAD_DIM, sk[0], sk[1])
            out_ref[rows, jp * LANES:(jp + 1) * LANES] = (pv[:, :LANES] * (1.0 / den)).astype(out_ref.dtype)


def _swa_post_body(q_ref, kt_ref, ktp_ref, v_ref, vp_ref, bias_ref, sink_ref, h_ref, p_ref, *rest, tiles_per_seq):
    w_refs, out_ref, attn_scr = rest[:-2], rest[-2], rest[-1]
    s = pl.program_id(0)
    nt = pl.num_programs(0) - 1
    tile = jnp.minimum(s, nt - 1)

    def post():
        out_ref[...] = _post_rows(attn_scr[(s + 1) % 2], h_ref[...], p_ref[...], *w_refs)

    def mixer():
        attn_slot = attn_scr.at[s % 2]
        blk = WINDOW
        qi = lax.broadcasted_iota(jnp.int32, (blk, blk), 0)
        kj = lax.broadcasted_iota(jnp.int32, (blk, blk), 1)
        has_prev = (kj <= qi) | (tile % tiles_per_seq != 0)
        for b in range(q_ref.shape[0] // blk):
            rows, prev_rows = slice(b * blk, (b + 1) * blk), slice((b - 1) * blk, b * blk)
            _swa_block(q_ref[rows, :],
                       ktp_ref[...] if b == 0 else kt_ref[:, prev_rows], kt_ref[:, rows],
                       vp_ref[...] if b == 0 else v_ref[prev_rows, :], v_ref[rows, :],
                       bias_ref, sink_ref, has_prev if b == 0 else None, attn_slot, rows)

    _skewed_step(s, nt, mixer, post)


def _swa_post(q, kt, v, bias, sinks, h, p, wo, w, *, layer, mixer, batch, seq):
    t = batch * seq
    tm = ROW_TILE
    nt = t // tm
    blocks = tm // WINDOW
    mix_i, mix_row, mix_col, post_row, p_spec = _pipelined_specs(tm, nt, layer)
    prev_blk = lambda s: jnp.maximum(mix_i(s) * blocks - 1, 0)
    w_specs, w_args = _post_weights(wo, w, layer, mixer)
    return pl.pallas_call(
        functools.partial(_swa_post_body, tiles_per_seq=seq // tm),
        grid=(nt + 1,),
        in_specs=[mix_row(D_MODEL), mix_col(KV_DIM), pl.BlockSpec((KV_DIM, WINDOW), lambda s: (0, prev_blk(s))),
                  mix_row(KV_DIM), pl.BlockSpec((WINDOW, KV_DIM), lambda s: (prev_blk(s), 0)),
                  _resident(bias.shape), pl.BlockSpec(memory_space=pltpu.SMEM), post_row(D_MODEL), p_spec] + w_specs,
        out_specs=post_row(D_MODEL),
        out_shape=jax.ShapeDtypeStruct((t, D_MODEL), F32),
        scratch_shapes=[pltpu.VMEM((2, tm, D_MODEL), BF16)],
        compiler_params=_params(1), name="swa_post",
    )(q, kt, kt, v, v, bias, sinks[mixer], h, p, *w_args)


SAMPLE_BB = 16


def _swa_sample_body(q_ref, kc_ref, vc_ref, kn_ref, vn_ref, bias_ref, sink_ref, out_ref, ko_ref, vo_ref):
    w = WINDOW
    row = lax.broadcasted_iota(jnp.int32, (w, KV_DIM), 0)
    rg = lax.broadcasted_iota(jnp.int32, (Q_HEADS, KV_DIM), 0) // GROUP
    lg = lax.broadcasted_iota(jnp.int32, (Q_HEADS, KV_DIM), 1) // HEAD_DIM
    rsel = lax.broadcasted_iota(jnp.int32, (Q_HEADS, HEAD_DIM), 0) // GROUP
    logits, v_wins = [], []
    for i in range(SAMPLE_BB):
        k_win = jnp.where(row == w - 1, kn_ref[i], pltpu.roll(kc_ref[i], w - 1, axis=0))
        v_win = jnp.where(row == w - 1, vn_ref[i], pltpu.roll(vc_ref[i], w - 1, axis=0))
        ko_ref[i] = k_win
        vo_ref[i] = v_win
        qb = q_ref[i].astype(F32)
        q_bd = jnp.where(rg == lg, jnp.concatenate([qb] * KV_HEADS, axis=1), 0.0).astype(BF16)
        logits.append(lax.dot_general(q_bd, k_win.astype(BF16), _NT, preferred_element_type=F32))
        v_wins.append(v_win.astype(BF16))
    lg_all = jnp.concatenate(logits, axis=0) + bias_ref[...]
    sink = sink_ref[...]
    mx = jnp.maximum(jnp.max(lg_all, axis=1, keepdims=True), sink)
    p = jnp.exp(lg_all - mx)
    probs = (p * (1.0 / (jnp.sum(p, axis=1, keepdims=True) + jnp.exp(sink - mx)))).astype(BF16)
    for i in range(SAMPLE_BB):
        o = _dot(probs[i * Q_HEADS:(i + 1) * Q_HEADS], v_wins[i])
        sel = o[:, 0:HEAD_DIM]
        for g in range(1, KV_HEADS):
            sel = jnp.where(rsel == g, o[:, g * HEAD_DIM:(g + 1) * HEAD_DIM], sel)
        out_ref[i] = sel.astype(out_ref.dtype)


def _swa_sample(q, kc_all, vc_all, kn, vn, bias, sinks, *, mixer):
    b = q.shape[0]
    bb = SAMPLE_BB
    blk = lambda shape: pl.BlockSpec((bb,) + shape, lambda i: (i, 0, 0))
    cache = pl.BlockSpec((None, bb, WINDOW, KV_DIM), lambda i: (mixer, i, 0, 0))
    return pl.pallas_call(
        _swa_sample_body,
        grid=(b // bb,),
        in_specs=[blk((Q_HEADS, HEAD_DIM)), cache, cache, blk((1, KV_DIM)), blk((1, KV_DIM)), _resident(bias.shape),
                  _layer(sinks, mixer)],
        out_specs=[blk((Q_HEADS, HEAD_DIM)), cache, cache],
        out_shape=[jax.ShapeDtypeStruct((b, Q_HEADS, HEAD_DIM), BF16), jax.ShapeDtypeStruct(kc_all.shape, F32),
                   jax.ShapeDtypeStruct(vc_all.shape, F32)],
        input_output_aliases={1: 1, 2: 2},
        compiler_params=_params(1), name="swa_sample",
    )(q, kc_all, vc_all, kn, vn, bias, sinks)


def kernel(x_prompt, x_sample, state_mlstm_C, state_mlstm_n, state_mlstm_m, cache_swa_k, cache_swa_v, p_prompt, p_sample, rel_bias, norm_pre_mix, norm_post_mix, norm_pre_ffn, norm_post_ffn, mlstm_w_in, mlstm_b_i, mlstm_b_f, mlstm_norm, mlstm_w_out, swa_w_qkv, swa_sinks, swa_w_o, ffn_w_up, ffn_w_down, ple_w_proj, ple_norm, ple_w_gate):
    bp, sp, _ = x_prompt.shape
    bs = x_sample.shape[0]
    win = cache_swa_k.shape[2]
    assert win == WINDOW and x_sample.shape[1] == 1
    bf = lambda a: a.astype(BF16)
    vecs = lambda a: a.astype(F32)[:, None, :]

    post_w = dict(npm=vecs(norm_post_mix), npf=vecs(norm_pre_ffn), wup=bf(ffn_w_up), wdn=bf(ffn_w_down),
                  npff=vecs(norm_post_ffn), wpj=bf(ple_w_proj), pln=vecs(ple_norm), wgate=bf(ple_w_gate))
    n_pre = vecs(norm_pre_mix)
    m_w = bf(mlstm_w_in)
    m_bias = jnp.concatenate([mlstm_b_i, mlstm_b_f], axis=1).astype(F32)
    m_wout, m_norm = bf(mlstm_w_out), vecs(mlstm_norm)
    s_w, s_wo = bf(swa_w_qkv), bf(swa_w_o)
    sinks = swa_sinks.astype(F32)

    qi = jnp.arange(WINDOW)[:, None]
    kj = jnp.arange(WINDOW)[None, :]
    dist_p = jnp.where(kj > qi, qi + WINDOW - kj, qi - kj)
    bias_p = _bias_table(_rel_bucket(dist_p).astype(jnp.int32), rel_bias.astype(F32))
    dist_s = (win - 1) - jnp.arange(win)
    bias_s = _bias_table(jnp.broadcast_to(_rel_bucket(dist_s)[None, :], (8, win)).astype(jnp.int32),
                         rel_bias.astype(F32))[:, 0, :]
    bias_s = jnp.tile(bias_s, (SAMPLE_BB, 1))
    sinks_s = jnp.tile(sinks[:, :, None], (1, SAMPLE_BB, 1))

    hp = x_prompt.reshape(bp * sp, D_MODEL)
    hs = x_sample.reshape(bs, D_MODEL)
    pp = p_prompt.reshape(DEPTH, bp * sp, PLE_DIM)
    ps = p_sample.reshape(DEPTH, bs, PLE_DIM)
    c_old = state_mlstm_C.astype(F32).reshape(-1, bs, HK, DV)
    k_win = cache_swa_k.astype(F32).reshape(-1, bs, win, KV_DIM)
    v_win = cache_swa_v.astype(F32).reshape(-1, bs, win, KV_DIM)
    c_new = None
    outs = dict(Cp=[], np=[], mp=[], kp=[], vp=[], ns=[], ms=[])

    for i in range(DEPTH):
        j = i // N_MIXERS
        if i % N_MIXERS == 0:
            wo = m_wout
            q, kt, v, og, gt = _mlstm_in(hp, n_pre, m_w, m_bias[j].reshape(-1, 1), m_norm,
                                         layer=i, mixer=j, prompt=True, act_dtype=BF16)
            hp, c_fin, n_fin, m_fin = _mlstm_post(q, kt, v, og, gt, hp, pp, wo, post_w,
                                                  layer=i, mixer=j, batch=bp, seq=sp)
            outs["Cp"].append(c_fin)
            outs["np"].append(n_fin[..., 0])
            outs["mp"].append(m_fin[..., 0])
            q, k, v, og, g = _mlstm_in(hs, n_pre, m_w, m_bias[j].reshape(1, -1), m_norm,
                                       layer=i, mixer=j, prompt=False, act_dtype=F32)
            a_s, c_new, n_new, m_new = _mlstm_step(
                c_old, c_new, state_mlstm_n[j].astype(F32), state_mlstm_m[j].astype(F32),
                q, k, v, og, g, mixer=j)
            outs["ns"].append(n_new)
            outs["ms"].append(m_new)
        else:
            wo = s_wo
            q, k, kt, v = _swa_in(hp, n_pre, s_w, layer=i, mixer=j)
            hp = _swa_post(q, kt, v, bias_p, sinks, hp, pp, wo, post_w, layer=i, mixer=j, batch=bp, seq=sp)
            last = lambda a: a.reshape(bp, sp, KV_DIM)[:, sp - win:].reshape(bp, win, KV_HEADS, HEAD_DIM)
            outs["kp"].append(last(k))
            outs["vp"].append(last(v))
            q, k, _, v = _swa_in(hs, n_pre, s_w, layer=i, mixer=j)
            a_s, k_win, v_win = _swa_sample(
                q.reshape(bs, Q_HEADS, HEAD_DIM), k_win, v_win, k.reshape(bs, 1, KV_DIM), v.reshape(bs, 1, KV_DIM),
                bias_s, sinks_s, mixer=j)
            a_s = a_s.reshape(bs, D_MODEL)
        hs = _post(a_s, hs, ps, wo, post_w, layer=i, mixer=j)

    st = lambda key, like: jnp.stack(outs[key]).astype(like.dtype)
    kv_out = lambda a, like: a.reshape(-1, bs, win, KV_HEADS, HEAD_DIM).astype(like.dtype)
    return (hp.reshape(x_prompt.shape), hs.reshape(x_sample.shape),
            st("Cp", state_mlstm_C), st("np", state_mlstm_n), st("mp", state_mlstm_m),
            st("kp", cache_swa_k), st("vp", cache_swa_v),
            c_new.reshape(state_mlstm_C.shape).astype(state_mlstm_C.dtype), st("ns", state_mlstm_n),
            st("ms", state_mlstm_m), kv_out(k_win, cache_swa_k), kv_out(v_win, cache_swa_v))
```

```python
import functools
import math

import jax
import jax.numpy as jnp
from jax import lax
from jax.experimental import pallas as pl
from jax.experimental.pallas import tpu as pltpu

F32 = jnp.float32
BF16 = jnp.bfloat16

D_MODEL = 1024
DEPTH = 4
N_MIXERS = 2
HEADS = 8
DQK = 64
DV = 128
CHUNK = 128
Q_HEADS = 16
KV_HEADS = 4
GROUP = Q_HEADS // KV_HEADS
HEAD_DIM = 64
KV_DIM = KV_HEADS * HEAD_DIM
WINDOW = 128
REL_BUCKETS = 32
REL_MAX_DIST = 128
PLE_DIM = 256
D_FF = 4 * D_MODEL
EPS = 1e-6
NEG = -1e30

HK, HV = HEADS * DQK, HEADS * DV
Q0, K0, V0, O0, G0 = 0, HK, 2 * HK, 2 * HK + HV, 2 * HK + HV + D_MODEL
MLSTM_IN = G0 + 2 * HEADS

V7X_VMEM_BYTES = 64 * 1024 * 1024
VMEM_LIMIT = V7X_VMEM_BYTES - 8 * 1024 * 1024
LANES = 128
ROW_TILE = 512
IN_ROW_TILE = 1024

_NT = (((1,), (1,)), ((), ()))


def _row_tile(t, tile=ROW_TILE):
    return tile if t % tile == 0 else t


def _params(n_grid):
    return pltpu.CompilerParams(dimension_semantics=("arbitrary",) * n_grid, vmem_limit_bytes=VMEM_LIMIT)


def _resident(shape):
    nd = len(shape)
    return pl.BlockSpec(shape, lambda *_: (0,) * nd, pipeline_mode=pl.Buffered(1))


def _layer(arr, layer):
    tail = arr.shape[1:]
    return pl.BlockSpec((None,) + tail, lambda *_: (layer,) + (0,) * len(tail), pipeline_mode=pl.Buffered(1))


def _skip_ref(body, pos):
    def wrapped(*refs):
        return body(*refs[:pos], *refs[pos + 1:])
    return wrapped


def _rms(x, w=None):
    y = x * lax.rsqrt(jnp.mean(x * x, axis=-1, keepdims=True) + EPS)
    return y if w is None else y * w


def _log_sigmoid(x):
    return -(jnp.maximum(-x, 0.0) + jnp.log1p(jnp.exp(-jnp.abs(x))))


def _sigmoid(x):
    return 1.0 / (1.0 + jnp.exp(-x))


def _dot(a, b):
    return jnp.dot(a, b, preferred_element_type=F32)


def _mlstm_in_body(h_ref, nw_ref, w_ref, bg_ref, hnw_ref, q_ref, k_ref, v_ref, o_ref, g_ref, *scratch, prompt):
    if prompt:
        wkgt_scr, = scratch

        @pl.when(pl.program_id(0) == 0)
        def _():
            wkgt_scr[0:HK, :] = w_ref[:, K0:V0].astype(F32).T.astype(BF16)
            wg = jnp.concatenate([w_ref[:, G0:MLSTM_IN].astype(F32),
                                  jnp.zeros((D_MODEL, LANES - 2 * HEADS), F32)], axis=1)
            wkgt_scr[HK:, :] = wg.T.astype(BF16)

    u = _rms(h_ref[...], nw_ref[...]).astype(BF16)
    q_ref[...] = (_dot(u, w_ref[:, Q0:K0]) * (DQK ** -0.5)).astype(q_ref.dtype)
    v_ref[...] = _dot(u, w_ref[:, V0:O0]).astype(v_ref.dtype)
    o_ref[...] = _sigmoid(_dot(u, w_ref[:, O0:G0])) * hnw_ref[...]
    if prompt:
        kgt = lax.dot_general(wkgt_scr[...], u, _NT, preferred_element_type=F32)
        k_ref[...] = kgt[:HK].astype(k_ref.dtype)
        gt = kgt[HK:HK + 2 * HEADS] + bg_ref[...]
        ig, lf = gt[:HEADS], _log_sigmoid(gt[HEADS:])
        r = lax.broadcasted_iota(jnp.int32, (CHUNK, CHUNK), 0)
        t = lax.broadcasted_iota(jnp.int32, (CHUNK, CHUNK), 1)
        upper = (r <= t).astype(F32)
        b = jnp.concatenate([jnp.dot(lf[:, c:c + CHUNK], upper, precision=lax.Precision.HIGHEST,
                                     preferred_element_type=F32) for c in range(0, lf.shape[1], CHUNK)], axis=1)
        g_ref[...] = jnp.concatenate([ig, lf, b], axis=0)
    else:
        k_ref[...] = _dot(u, w_ref[:, K0:V0]).astype(k_ref.dtype)
        g = _dot(u, w_ref[:, G0:MLSTM_IN]) + bg_ref[...]
        lane = lax.broadcasted_iota(jnp.int32, g.shape, 1)
        g_ref[...] = jnp.where(lane < HEADS, g, _log_sigmoid(g))


def _mlstm_in(h, nw, w, bg, hnw, *, layer, mixer, prompt, act_dtype):
    t = h.shape[0]
    tm = _row_tile(t, IN_ROW_TILE)
    row = lambda n: pl.BlockSpec((tm, n), lambda i: (i, 0))
    col = lambda n: pl.BlockSpec((n, tm), lambda i: (0, i))
    kshape, kspec = ((HK, t), col(HK)) if prompt else ((t, HK), row(HK))
    gshape, gspec = ((3 * HEADS, t), col(3 * HEADS)) if prompt else ((t, 2 * HEADS), row(2 * HEADS))
    return pl.pallas_call(
        functools.partial(_mlstm_in_body, prompt=prompt),
        grid=(t // tm,),
        in_specs=[row(D_MODEL), _layer(nw, layer), _layer(w, mixer), _resident(bg.shape), _layer(hnw, mixer)],
        out_specs=[row(HK), kspec, row(HV), row(D_MODEL), gspec],
        out_shape=[jax.ShapeDtypeStruct((t, HK), act_dtype), jax.ShapeDtypeStruct(kshape, act_dtype),
                   jax.ShapeDtypeStruct((t, HV), act_dtype), jax.ShapeDtypeStruct((t, D_MODEL), F32),
                   jax.ShapeDtypeStruct(gshape, F32)],
        scratch_shapes=[pltpu.VMEM((HK + LANES, D_MODEL), BF16)] if prompt else [],
        compiler_params=_params(1), name="mlstm_in",
    )(h, nw, w, bg, hnw)


def _swa_in_body(h_ref, nw_ref, w_ref, q_ref, k_ref, kt_ref, v_ref):
    u = _rms(h_ref[...], nw_ref[...]).astype(BF16)
    q_ref[...] = (_dot(u, w_ref[:, :D_MODEL]) * (HEAD_DIM ** -0.5)).astype(q_ref.dtype)
    k = _dot(u, w_ref[:, D_MODEL:D_MODEL + KV_DIM])
    k_ref[...] = k
    kt_ref[...] = k.T.astype(kt_ref.dtype)
    v_ref[...] = _dot(u, w_ref[:, D_MODEL + KV_DIM:])


def _swa_in(h, nw, w, *, layer, mixer):
    t = h.shape[0]
    tm = _row_tile(t, 2 * IN_ROW_TILE)
    row = lambda n: pl.BlockSpec((tm, n), lambda i: (i, 0))
    return pl.pallas_call(
        _swa_in_body,
        grid=(t // tm,),
        in_specs=[row(D_MODEL), _layer(nw, layer), _layer(w, mixer)],
        out_specs=[row(D_MODEL), row(KV_DIM), pl.BlockSpec((KV_DIM, tm), lambda i: (0, i)), row(KV_DIM)],
        out_shape=[jax.ShapeDtypeStruct((t, D_MODEL), BF16), jax.ShapeDtypeStruct((t, KV_DIM), F32),
                   jax.ShapeDtypeStruct((KV_DIM, t), BF16), jax.ShapeDtypeStruct((t, KV_DIM), F32)],
        compiler_params=_params(1), name="swa_in",
    )(h, nw, w)


FF_CHUNK = 1024


def _post_rows(a, h, p, wo_ref, npm_ref, npf_ref, wup_ref, wdn_ref, npff_ref, wpj_ref, pln_ref, wgate_ref):
    y = _dot(a.astype(BF16), wo_ref[...])
    h1 = h + _rms(y, npm_ref[...])
    u = _rms(h1, npf_ref[...]).astype(BF16)
    acc = jnp.zeros(h1.shape, F32)
    for c in range(D_FF // FF_CHUNK):
        hid = jnp.maximum(_dot(u, wup_ref[:, c * FF_CHUNK:(c + 1) * FF_CHUNK]), 0.0)
        acc = acc + _dot((hid * hid).astype(BF16), wdn_ref[c * FF_CHUNK:(c + 1) * FF_CHUNK, :])
    h2 = h1 + _rms(acc, npff_ref[...])
    e = _rms(_dot(p.astype(BF16), wpj_ref[...]), pln_ref[...])
    g = _sigmoid(_dot(_rms(h2).astype(BF16), wgate_ref[...]))
    return h2 + e * g


def _post_body(a_ref, h_ref, p_ref, *rest):
    w_refs, out_ref = rest[:-1], rest[-1]
    out_ref[...] = _post_rows(a_ref[...], h_ref[...], p_ref[...], *w_refs)


_POST_KEYS = ("npm", "npf", "wup", "wdn", "npff", "wpj", "pln", "wgate")


def _post_weights(wo, w, layer, mixer):
    return ([_layer(wo, mixer)] + [_layer(w[k], layer) for k in _POST_KEYS], [wo] + [w[k] for k in _POST_KEYS])


def _post(a, h, p, wo, w, *, layer, mixer):
    t = h.shape[0]
    tm = _row_tile(t)
    row = lambda n: pl.BlockSpec((tm, n), lambda i: (i, 0))
    w_specs, w_args = _post_weights(wo, w, layer, mixer)
    return pl.pallas_call(
        _post_body,
        grid=(t // tm,),
        in_specs=[row(D_MODEL), row(D_MODEL), pl.BlockSpec((None, tm, PLE_DIM), lambda i: (layer, i, 0))] + w_specs,
        out_specs=row(D_MODEL),
        out_shape=jax.ShapeDtypeStruct((t, D_MODEL), F32),
        compiler_params=_params(1), name="post",
    )(a, h, p, *w_args)


def _skewed_step(s, nt, mixer, post):
    @pl.when(s == 0)
    def _():
        mixer()

    @pl.when((s > 0) & (s < nt))
    def _():
        post()
        mixer()

    @pl.when(s == nt)
    def _():
        post()


def _pipelined_specs(tm, nt, layer):
    mix_i = lambda s: jnp.minimum(s, nt - 1)
    post_i = lambda s: jnp.maximum(s - 1, 0)
    mix_row = lambda n: pl.BlockSpec((tm, n), lambda s: (mix_i(s), 0))
    mix_col = lambda n: pl.BlockSpec((n, tm), lambda s: (0, mix_i(s)))
    post_row = lambda n: pl.BlockSpec((tm, n), lambda s: (post_i(s), 0))
    p_spec = pl.BlockSpec((None, tm, PLE_DIM), lambda s: (layer, post_i(s), 0))
    return mix_i, mix_row, mix_col, post_row, p_spec


STATE_W = 2 * DV


def _scan_chunk(q_ref, kt_ref, v_ref, og_ref, gt_ref, hg_ref, cn_scr, m_scr, r0):
    L = CHUNK
    rows = slice(r0, r0 + L)
    ri = lax.broadcasted_iota(jnp.int32, (L, L), 0)
    ci = lax.broadcasted_iota(jnp.int32, (L, L), 1)
    tri = ci <= ri
    gt = gt_ref[:, rows]
    ig_rows, lf_rows, b_rows = gt[:HEADS], gt[HEADS:2 * HEADS], gt[2 * HEADS:]
    a_rows = ig_rows - b_rows
    m_prev = m_scr[...]
    b_last = jnp.broadcast_to(b_rows[:, L - 1:L], (HEADS, L))
    m_new = b_last + jnp.maximum(m_prev, jnp.max(a_rows, axis=1, keepdims=True))
    decay_rows = jnp.exp(b_last + m_prev - m_new)
    wk_rows = jnp.exp(b_last - b_rows + ig_rows - m_new)
    m_scr[...] = m_new

    ones_rhs = jnp.ones((L, DV), BF16)
    zeros_kt = jnp.zeros((DQK, L), BF16)
    zeros_cn = jnp.zeros((DQK, STATE_W), BF16)
    for j in range(HEADS // 2):
        q_pair = q_ref[rows, j * 128:(j + 1) * 128]
        kt_pair = [kt_ref[h * DQK:(h + 1) * DQK, rows] for h in (2 * j, 2 * j + 1)]
        s_pair = _dot(q_pair, jnp.concatenate([jnp.concatenate([kt_pair[0], zeros_kt], axis=0),
                                               jnp.concatenate([zeros_kt, kt_pair[1]], axis=0)], axis=1))
        for par in range(2):
            h = 2 * j + par
            kt_h = kt_pair[par]
            v_ext = jnp.concatenate([v_ref[rows, h * DV:(h + 1) * DV], ones_rhs], axis=1)
            cn = cn_scr[h]
            cn16 = cn.astype(BF16)
            a_row, m_row = a_rows[h:h + 1], m_prev[h:h + 1]

            big_m = jnp.maximum(jnp.max(jnp.where(tri, a_row, -jnp.inf), axis=1, keepdims=True), m_row)
            b_col = jnp.sum(jnp.where(tri, lf_rows[h:h + 1], 0.0), axis=1, keepdims=True)
            w_inter = jnp.exp(m_row - big_m)
            s = s_pair[:, par * L:(par + 1) * L] * jnp.exp(jnp.where(tri, a_row - big_m, NEG))
            q_w = (q_pair.astype(F32) * w_inter).astype(BF16)
            state = jnp.concatenate([cn16, zeros_cn] if par == 0 else [zeros_cn, cn16], axis=0)
            nd = _dot(jnp.concatenate([s.astype(BF16), q_w], axis=1), jnp.concatenate([v_ext, state], axis=0))
            hh = nd[:, :DV] * (1.0 / jnp.maximum(jnp.abs(nd[:, DV:]), jnp.exp(-(b_col + big_m))))

            kw_t = kt_h.astype(F32) * wk_rows[h:h + 1]
            dec = decay_rows[h:h + 1]
            cn_scr[h] = jnp.concatenate([dec, dec], axis=1) * cn + _dot(kw_t.astype(BF16), v_ext)

            hg_ref[rows, h * DV:(h + 1) * DV] = (_rms(hh) * og_ref[rows, h * DV:(h + 1) * DV]).astype(hg_ref.dtype)


def _mlstm_post_body(q_ref, kt_ref, v_ref, og_ref, gt_ref, h_ref, p_ref, *rest, tiles_per_seq):
    w_refs = rest[:-7]
    out_ref, c_out_ref, n_out_ref, m_out_ref, hg_scr, cn_scr, m_scr = rest[-7:]
    s = pl.program_id(0)
    nt = pl.num_programs(0) - 1
    tile = jnp.minimum(s, nt - 1)

    @pl.when((tile % tiles_per_seq == 0) & (s < nt))
    def _():
        cn_scr[...] = jnp.zeros(cn_scr.shape, F32)
        m_scr[...] = jnp.zeros(m_scr.shape, F32)

    def post():
        out_ref[...] = _post_rows(hg_scr[(s + 1) % 2], h_ref[...], p_ref[...], *w_refs)

    def mixer():
        hg_slot = hg_scr.at[s % 2]
        for r0 in range(0, q_ref.shape[0], CHUNK):
            _scan_chunk(q_ref, kt_ref, v_ref, og_ref, gt_ref, hg_slot, cn_scr, m_scr, r0)

    _skewed_step(s, nt, mixer, post)

    @pl.when((tile % tiles_per_seq == tiles_per_seq - 1) & (s < nt))
    def _():
        for h in range(HEADS):
            c_out_ref[0, h] = cn_scr[h, :, 0:DV]
            n_out_ref[0, h] = cn_scr[h, :, DV:STATE_W]
        m_out_ref[0] = m_scr[...]


def _mlstm_post(q, kt, v, og, gt, h, p, wo, w, *, layer, mixer, batch, seq):
    t = batch * seq
    tm = ROW_TILE
    nt = t // tm
    tiles_per_seq = seq // tm
    mix_i, mix_row, mix_col, post_row, p_spec = _pipelined_specs(tm, nt, layer)
    seq_i = lambda s: mix_i(s) // tiles_per_seq
    w_specs, w_args = _post_weights(wo, w, layer, mixer)
    return pl.pallas_call(
        functools.partial(_mlstm_post_body, tiles_per_seq=tiles_per_seq),
        grid=(nt + 1,),
        in_specs=[mix_row(HK), mix_col(HK), mix_row(HV), mix_row(D_MODEL), mix_col(3 * HEADS),
                  post_row(D_MODEL), p_spec] + w_specs,
        out_specs=[post_row(D_MODEL),
                   pl.BlockSpec((1, HEADS, DQK, DV), lambda s: (seq_i(s), 0, 0, 0)),
                   pl.BlockSpec((1, HEADS, DQK, STATE_W - DV), lambda s: (seq_i(s), 0, 0, 0)),
                   pl.BlockSpec((1, HEADS, LANES), lambda s: (seq_i(s), 0, 0))],
        out_shape=[jax.ShapeDtypeStruct((t, D_MODEL), F32),
                   jax.ShapeDtypeStruct((batch, HEADS, DQK, DV), F32),
                   jax.ShapeDtypeStruct((batch, HEADS, DQK, STATE_W - DV), F32),
                   jax.ShapeDtypeStruct((batch, HEADS, LANES), F32)],
        scratch_shapes=[pltpu.VMEM((2, tm, HV), BF16), pltpu.VMEM((HEADS, DQK, STATE_W), F32),
                        pltpu.VMEM((HEADS, LANES), F32)],
        compiler_params=_params(1), name="mlstm_post",
    )(q, kt, v, og, gt, h, p, *w_args)


STEP_BB = 16
_TN = (((0,), (0,)), ((), ()))


def _mlstm_step_body(c_ref, n_ref, m_ref, q_ref, k_ref, v_ref, og_ref, g_ref,
                     hg_ref, c_out_ref, n_out_ref, m_out_ref, *, own):
    bb = STEP_BB
    for slab in range(c_out_ref.shape[0]):
        if slab != own:
            c_out_ref[slab] = jnp.zeros(c_out_ref.shape[1:], F32)
    g = g_ref[...]
    ig, lf, m_prev = g[:, :HEADS], g[:, HEADS:], m_ref[...]
    inter = lf + m_prev
    m_t = jnp.maximum(inter, ig)
    w_inter = jnp.exp(inter - m_t)
    in_w = jnp.exp(ig - m_t)
    decay = jnp.exp(lf + m_prev - m_t)
    floor = jnp.exp(-m_t)
    m_out_ref[...] = m_t
    diag = (lax.broadcasted_iota(jnp.int32, (bb, bb * DV), 0)
            == lax.broadcasted_iota(jnp.int32, (bb, bb * DV), 1) // DV)
    row_v = lax.broadcasted_iota(jnp.int32, (bb, DV), 0)

    for h in range(HEADS):
        col = lambda a: a[:, h:h + 1]
        q = q_ref[:, h * DQK:(h + 1) * DQK]
        k = k_ref[:, h * DQK:(h + 1) * DQK]
        v = v_ref[:, h * DV:(h + 1) * DV]
        n = n_ref[:, h, :]
        s = jnp.sum(q * k, axis=1, keepdims=True) * col(in_w)
        kw = k * col(in_w)
        c_tiles = [c_ref[i, h * DQK:(h + 1) * DQK, :] for i in range(bb)]
        qc_all = _dot(q.astype(BF16), jnp.concatenate(c_tiles, axis=1).astype(BF16))
        v_bd = jnp.where(diag, jnp.concatenate([v] * bb, axis=1), 0.0).astype(BF16)
        outer = lax.dot_general(kw.astype(BF16), v_bd, _TN, preferred_element_type=F32)
        qc = jnp.zeros((bb, DV), F32)
        for i in range(bb):
            qc = jnp.where(row_v == i, qc_all[:, i * DV:(i + 1) * DV], qc)
            c_out_ref[own, i, h * DQK:(h + 1) * DQK, :] = (decay[i:i + 1, h:h + 1] * c_tiles[i]
                                                           + outer[:, i * DV:(i + 1) * DV])
        num = col(w_inter) * qc + s * v
        den = col(w_inter) * jnp.sum(q * n, axis=1, keepdims=True) + s
        hh = num * (1.0 / jnp.maximum(jnp.abs(den), col(floor)))
        n_out_ref[:, h, :] = col(decay) * n + kw
        hg_ref[:, h * DV:(h + 1) * DV] = _rms(hh) * og_ref[:, h * DV:(h + 1) * DV]


def _mlstm_step(c_all, c_new_prev, n, m, q, k, v, og, g, *, mixer):
    n_slabs, b = c_all.shape[:2]
    bb = STEP_BB
    rows = lambda width: pl.BlockSpec((bb, width), lambda i: (i, 0))
    cspec = pl.BlockSpec((None, bb, HK, DV), lambda i: (mixer, i, 0, 0))
    nspec = pl.BlockSpec((bb, HEADS, DQK), lambda i: (i, 0, 0))
    in_specs = [cspec, nspec, rows(HEADS), rows(HK), rows(HK), rows(HV), rows(D_MODEL), rows(2 * HEADS)]
    args = [c_all, n, m, q, k, v, og, g]
    if c_new_prev is None:
        c_out_spec = pl.BlockSpec((n_slabs, bb, HK, DV), lambda i: (0, i, 0, 0))
        body, aliases = functools.partial(_mlstm_step_body, own=mixer), {}
    else:
        c_out_spec = pl.BlockSpec((1, bb, HK, DV), lambda i: (mixer, i, 0, 0))
        in_specs.append(pl.BlockSpec(memory_space=pl.ANY))
        body, aliases = _skip_ref(functools.partial(_mlstm_step_body, own=0), len(args)), {len(args): 1}
        args.append(c_new_prev)
    return pl.pallas_call(
        body,
        grid=(b // bb,),
        in_specs=in_specs,
        out_specs=[rows(HV), c_out_spec, nspec, rows(HEADS)],
        out_shape=[jax.ShapeDtypeStruct((b, HV), F32), jax.ShapeDtypeStruct(c_all.shape, F32),
                   jax.ShapeDtypeStruct(n.shape, F32), jax.ShapeDtypeStruct((b, HEADS), F32)],
        input_output_aliases=aliases,
        compiler_params=_params(1), name="mlstm_step",
    )(*args)


def _bias_body(bucket_ref, rel_ref, out_ref):
    bucket = bucket_ref[...]
    for h in range(Q_HEADS):
        acc = jnp.zeros(bucket.shape, F32)
        for b in range(REL_BUCKETS):
            acc = jnp.where(bucket == b, rel_ref[b, h], acc)
        out_ref[h] = acc


def _bias_table(bucket, rel_bias):
    shape = bucket.shape
    return pl.pallas_call(
        _bias_body,
        in_specs=[pl.BlockSpec(memory_space=pltpu.VMEM), pl.BlockSpec(memory_space=pltpu.SMEM)],
        out_specs=pl.BlockSpec(memory_space=pltpu.VMEM),
        out_shape=jax.ShapeDtypeStruct((Q_HEADS,) + shape, F32),
        name="rel_bias_table",
    )(bucket, rel_bias)


def _rel_bucket(dist):
    max_exact = REL_BUCKETS // 2
    d = jnp.maximum(dist, 0)
    df = jnp.maximum(d, 1).astype(F32)
    large = max_exact + (jnp.log(df / max_exact) / math.log(REL_MAX_DIST / max_exact)
                         * (REL_BUCKETS - max_exact)).astype(jnp.int32)
    large = jnp.minimum(large, REL_BUCKETS - 1)
    return jnp.where(d < max_exact, d, large)


def _swa_block(q, kt_prev, kt_cur, v_prev, v_cur, bias_ref, sink_ref, in_range, out_ref, rows):
    blk = WINDOW
    qi = lax.broadcasted_iota(jnp.int32, (blk, blk), 0)
    kj = lax.broadcasted_iota(jnp.int32, (blk, blk), 1)
    from_prev = kj > qi
    lane = lax.broadcasted_iota(jnp.int32, (2 * blk, LANES), 1)
    lane_q = lax.broadcasted_iota(jnp.int32, (blk, LANES), 1)
    zero_k = jnp.zeros((HEAD_DIM, 2 * blk), BF16)
    ones_bd = jnp.concatenate([(lane < HEAD_DIM).astype(BF16), (lane >= HEAD_DIM).astype(BF16)], axis=0)

    for g in range(KV_HEADS):
        hd_rows = slice(g * HEAD_DIM, (g + 1) * HEAD_DIM)
        kt_g = jnp.concatenate([kt_prev[hd_rows, :], kt_cur[hd_rows, :]], axis=1)
        rhs = jnp.concatenate([jnp.concatenate([kt_g, zero_k], axis=0),
                               jnp.concatenate([zero_k, kt_g], axis=0)], axis=1)
        v_lanes = slice((g // 2) * LANES, (g // 2 + 1) * LANES)
        vt = jnp.concatenate([v_prev[:, v_lanes], v_cur[:, v_lanes]], axis=0)
        vr = pltpu.roll(vt, HEAD_DIM, axis=1)
        v_lo = jnp.where(lane < HEAD_DIM, vt if g % 2 == 0 else vr, 0.0)
        v_hi = jnp.where(lane >= HEAD_DIM, vr if g % 2 == 0 else vt, 0.0)
        v_bd = jnp.concatenate([jnp.concatenate([v_lo, v_hi], axis=0).astype(BF16), ones_bd], axis=1)
        for pair in range(GROUP // 2):
            jp = g * (GROUP // 2) + pair
            logits = _dot(q[:, jp * LANES:(jp + 1) * LANES], rhs)
            ps, sk = [], []
            for i in range(2):
                hd = 2 * jp + i
                lg = jnp.where(from_prev, logits[:, 2 * i * blk:(2 * i + 1) * blk],
                               logits[:, (2 * i + 1) * blk:(2 * i + 2) * blk]) + bias_ref[hd]
                if in_range is not None:
                    lg = jnp.where(in_range, lg, NEG)
                sink = sink_ref[hd]
                mx = jnp.maximum(jnp.max(lg, axis=1, keepdims=True), sink)
                p = jnp.exp(lg - mx)
                ps += [jnp.where(from_prev, p, 0.0).astype(BF16), jnp.where(from_prev, 0.0, p).astype(BF16)]
                sk.append(jnp.exp(sink - mx))
            pv = _dot(jnp.concatenate(ps, axis=1), v_bd)
            den = pv[:, LANES:] + jnp.where(lane_q < HEAD_DIM, sk[0], sk[1])
            out_ref[rows, jp * LANES:(jp + 1) * LANES] = (pv[:, :LANES] * (1.0 / den)).astype(out_ref.dtype)


def _swa_post_body(q_ref, kt_ref, ktp_ref, v_ref, vp_ref, bias_ref, sink_ref, h_ref, p_ref, *rest, tiles_per_seq):
    w_refs, out_ref, attn_scr = rest[:-2], rest[-2], rest[-1]
    s = pl.program_id(0)
    nt = pl.num_programs(0) - 1
    tile = jnp.minimum(s, nt - 1)

    def post():
        out_ref[...] = _post_rows(attn_scr[(s + 1) % 2], h_ref[...], p_ref[...], *w_refs)

    def mixer():
        attn_slot = attn_scr.at[s % 2]
        blk = WINDOW
        qi = lax.broadcasted_iota(jnp.int32, (blk, blk), 0)
        kj = lax.broadcasted_iota(jnp.int32, (blk, blk), 1)
        has_prev = (kj <= qi) | (tile % tiles_per_seq != 0)
        for b in range(q_ref.shape[0] // blk):
            rows, prev_rows = slice(b * blk, (b + 1) * blk), slice((b - 1) * blk, b * blk)
            _swa_block(q_ref[rows, :],
                       ktp_ref[...] if b == 0 else kt_ref[:, prev_rows], kt_ref[:, rows],
                       vp_ref[...] if b == 0 else v_ref[prev_rows, :], v_ref[rows, :],
                       bias_ref, sink_ref, has_prev if b == 0 else None, attn_slot, rows)

    _skewed_step(s, nt, mixer, post)


def _swa_post(q, kt, v, bias, sinks, h, p, wo, w, *, layer, mixer, batch, seq):
    t = batch * seq
    tm = ROW_TILE
    nt = t // tm
    blocks = tm // WINDOW
    mix_i, mix_row, mix_col, post_row, p_spec = _pipelined_specs(tm, nt, layer)
    prev_blk = lambda s: jnp.maximum(mix_i(s) * blocks - 1, 0)
    w_specs, w_args = _post_weights(wo, w, layer, mixer)
    return pl.pallas_call(
        functools.partial(_swa_post_body, tiles_per_seq=seq // tm),
        grid=(nt + 1,),
        in_specs=[mix_row(D_MODEL), mix_col(KV_DIM), pl.BlockSpec((KV_DIM, WINDOW), lambda s: (0, prev_blk(s))),
                  mix_row(KV_DIM), pl.BlockSpec((WINDOW, KV_DIM), lambda s: (prev_blk(s), 0)),
                  _resident(bias.shape), pl.BlockSpec(memory_space=pltpu.SMEM), post_row(D_MODEL), p_spec] + w_specs,
        out_specs=post_row(D_MODEL),
        out_shape=jax.ShapeDtypeStruct((t, D_MODEL), F32),
        scratch_shapes=[pltpu.VMEM((2, tm, D_MODEL), BF16)],
        compiler_params=_params(1), name="swa_post",
    )(q, kt, kt, v, v, bias, sinks[mixer], h, p, *w_args)


SAMPLE_BB = 16


def _swa_sample_body(q_ref, kc_ref, vc_ref, kn_ref, vn_ref, bias_ref, sink_ref, out_ref, ko_ref, vo_ref):
    w = WINDOW
    row = lax.broadcasted_iota(jnp.int32, (w, KV_DIM), 0)
    rg = lax.broadcasted_iota(jnp.int32, (Q_HEADS, KV_DIM), 0) // GROUP
    lg = lax.broadcasted_iota(jnp.int32, (Q_HEADS, KV_DIM), 1) // HEAD_DIM
    rsel = lax.broadcasted_iota(jnp.int32, (Q_HEADS, HEAD_DIM), 0) // GROUP
    logits, v_wins = [], []
    for i in range(SAMPLE_BB):
        k_win = jnp.where(row == w - 1, kn_ref[i], pltpu.roll(kc_ref[i], w - 1, axis=0))
        v_win = jnp.where(row == w - 1, vn_ref[i], pltpu.roll(vc_ref[i], w - 1, axis=0))
        ko_ref[i] = k_win
        vo_ref[i] = v_win
        qb = q_ref[i].astype(F32)
        q_bd = jnp.where(rg == lg, jnp.concatenate([qb] * KV_HEADS, axis=1), 0.0).astype(BF16)
        logits.append(lax.dot_general(q_bd, k_win.astype(BF16), _NT, preferred_element_type=F32))
        v_wins.append(v_win.astype(BF16))
    lg_all = jnp.concatenate(logits, axis=0) + bias_ref[...]
    sink = sink_ref[...]
    mx = jnp.maximum(jnp.max(lg_all, axis=1, keepdims=True), sink)
    p = jnp.exp(lg_all - mx)
    probs = (p * (1.0 / (jnp.sum(p, axis=1, keepdims=True) + jnp.exp(sink - mx)))).astype(BF16)
    for i in range(SAMPLE_BB):
        o = _dot(probs[i * Q_HEADS:(i + 1) * Q_HEADS], v_wins[i])
        sel = o[:, 0:HEAD_DIM]
        for g in range(1, KV_HEADS):
            sel = jnp.where(rsel == g, o[:, g * HEAD_DIM:(g + 1) * HEAD_DIM], sel)
        out_ref[i] = sel.astype(out_ref.dtype)


def _swa_sample(q, kc_all, vc_all, kn, vn, bias, sinks, *, mixer):
    b = q.shape[0]
    bb = SAMPLE_BB
    blk = lambda shape: pl.BlockSpec((bb,) + shape, lambda i: (i, 0, 0))
    cache = pl.BlockSpec((None, bb, WINDOW, KV_DIM), lambda i: (mixer, i, 0, 0))
    return pl.pallas_call(
        _swa_sample_body,
        grid=(b // bb,),
        in_specs=[blk((Q_HEADS, HEAD_DIM)), cache, cache, blk((1, KV_DIM)), blk((1, KV_DIM)), _resident(bias.shape),
                  _layer(sinks, mixer)],
        out_specs=[blk((Q_HEADS, HEAD_DIM)), cache, cache],
        out_shape=[jax.ShapeDtypeStruct((b, Q_HEADS, HEAD_DIM), BF16), jax.ShapeDtypeStruct(kc_all.shape, F32),
                   jax.ShapeDtypeStruct(vc_all.shape, F32)],
        input_output_aliases={1: 1, 2: 2},
        compiler_params=_params(1), name="swa_sample",
    )(q, kc_all, vc_all, kn, vn, bias, sinks)


def kernel(x_prompt, x_sample, state_mlstm_C, state_mlstm_n, state_mlstm_m, cache_swa_k, cache_swa_v, p_prompt, p_sample, rel_bias, norm_pre_mix, norm_post_mix, norm_pre_ffn, norm_post_ffn, mlstm_w_in, mlstm_b_i, mlstm_b_f, mlstm_norm, mlstm_w_out, swa_w_qkv, swa_sinks, swa_w_o, ffn_w_up, ffn_w_down, ple_w_proj, ple_norm, ple_w_gate):
    bp, sp, _ = x_prompt.shape
    bs = x_sample.shape[0]
    win = cache_swa_k.shape[2]
    assert win == WINDOW and x_sample.shape[1] == 1
    bf = lambda a: a.astype(BF16)
    vecs = lambda a: a.astype(F32)[:, None, :]

    post_w = dict(npm=vecs(norm_post_mix), npf=vecs(norm_pre_ffn), wup=bf(ffn_w_up), wdn=bf(ffn_w_down),
                  npff=vecs(norm_post_ffn), wpj=bf(ple_w_proj), pln=vecs(ple_norm), wgate=bf(ple_w_gate))
    n_pre = vecs(norm_pre_mix)
    m_w = bf(mlstm_w_in)
    m_bias = jnp.concatenate([mlstm_b_i, mlstm_b_f], axis=1).astype(F32)
    m_wout, m_norm = bf(mlstm_w_out), vecs(mlstm_norm)
    s_w, s_wo = bf(swa_w_qkv), bf(swa_w_o)
    sinks = swa_sinks.astype(F32)

    qi = jnp.arange(WINDOW)[:, None]
    kj = jnp.arange(WINDOW)[None, :]
    dist_p = jnp.where(kj > qi, qi + WINDOW - kj, qi - kj)
    bias_p = _bias_table(_rel_bucket(dist_p).astype(jnp.int32), rel_bias.astype(F32))
    dist_s = (win - 1) - jnp.arange(win)
    bias_s = _bias_table(jnp.broadcast_to(_rel_bucket(dist_s)[None, :], (8, win)).astype(jnp.int32),
                         rel_bias.astype(F32))[:, 0, :]
    bias_s = jnp.tile(bias_s, (SAMPLE_BB, 1))
    sinks_s = jnp.tile(sinks[:, :, None], (1, SAMPLE_BB, 1))

    hp = x_prompt.reshape(bp * sp, D_MODEL)
    hs = x_sample.reshape(bs, D_MODEL)
    pp = p_prompt.reshape(DEPTH, bp * sp, PLE_DIM)
    ps = p_sample.reshape(DEPTH, bs, PLE_DIM)
    c_old = state_mlstm_C.astype(F32).reshape(-1, bs, HK, DV)
    k_win = cache_swa_k.astype(F32).reshape(-1, bs, win, KV_DIM)
    v_win = cache_swa_v.astype(F32).reshape(-1, bs, win, KV_DIM)
    c_new = None
    outs = dict(Cp=[], np=[], mp=[], kp=[], vp=[], ns=[], ms=[])

    for i in range(DEPTH):
        j = i // N_MIXERS
        if i % N_MIXERS == 0:
            wo = m_wout
            q, kt, v, og, gt = _mlstm_in(hp, n_pre, m_w, m_bias[j].reshape(-1, 1), m_norm,
                                         layer=i, mixer=j, prompt=True, act_dtype=BF16)
            hp, c_fin, n_fin, m_fin = _mlstm_post(q, kt, v, og, gt, hp, pp, wo, post_w,
                                                  layer=i, mixer=j, batch=bp, seq=sp)
            outs["Cp"].append(c_fin)
            outs["np"].append(n_fin[..., 0])
            outs["mp"].append(m_fin[..., 0])
            q, k, v, og, g = _mlstm_in(hs, n_pre, m_w, m_bias[j].reshape(1, -1), m_norm,
                                       layer=i, mixer=j, prompt=False, act_dtype=F32)
            a_s, c_new, n_new, m_new = _mlstm_step(
                c_old, c_new, state_mlstm_n[j].astype(F32), state_mlstm_m[j].astype(F32),
                q, k, v, og, g, mixer=j)
            outs["ns"].append(n_new)
            outs["ms"].append(m_new)
        else:
            wo = s_wo
            q, k, kt, v = _swa_in(hp, n_pre, s_w, layer=i, mixer=j)
            hp = _swa_post(q, kt, v, bias_p, sinks, hp, pp, wo, post_w, layer=i, mixer=j, batch=bp, seq=sp)
            last = lambda a: a.reshape(bp, sp, KV_DIM)[:, sp - win:].reshape(bp, win, KV_HEADS, HEAD_DIM)
            outs["kp"].append(last(k))
            outs["vp"].append(last(v))
            q, k, _, v = _swa_in(hs, n_pre, s_w, layer=i, mixer=j)
            a_s, k_win, v_win = _swa_sample(
                q.reshape(bs, Q_HEADS, HEAD_DIM), k_win, v_win, k.reshape(bs, 1, KV_DIM), v.reshape(bs, 1, KV_DIM),
                bias_s, sinks_s, mixer=j)
            a_s = a_s.reshape(bs, D_MODEL)
        hs = _post(a_s, hs, ps, wo, post_w, layer=i, mixer=j)

    st = lambda key, like: jnp.stack(outs[key]).astype(like.dtype)
    kv_out = lambda a, like: a.reshape(-1, bs, win, KV_HEADS, HEAD_DIM).astype(like.dtype)
    return (hp.reshape(x_prompt.shape), hs.reshape(x_sample.shape),
            st("Cp", state_mlstm_C), st("np", state_mlstm_n), st("mp", state_mlstm_m),
            st("kp", cache_swa_k), st("vp", cache_swa_v),
            c_new.reshape(state_mlstm_C.shape).astype(state_mlstm_C.dtype), st("ns", state_mlstm_n),
            st("ms", state_mlstm_m), kv_out(k_win, cache_swa_k), kv_out(v_win, cache_swa_v))
```
